```python
import math
import jax, jax.numpy as jnp
from jax import lax
import numpy as np

D_MODEL = 1024
BATCH = 8
SEQ = 2048
DEPTH = 2

GRID_W = 64
CTX_LEN = 256
MLP_HIDDEN = 4 * D_MODEL
ADA_CHUNKS = 6
SHORT_CONV = 3
EPS = 1e-6

RW_WIDTH = D_MODEL // 2
RW_HEAD_DIM = 64
RW_HEADS = RW_WIDTH // RW_HEAD_DIM
RW_DECAY_RANK = 64
RW_AAA_RANK = 64
RW_GATE_RANK = 128
RW_COLS = 3 * RW_WIDTH + RW_DECAY_RANK + RW_AAA_RANK + RW_GATE_RANK
RW_SPLITS = (RW_WIDTH, 2 * RW_WIDTH, 3 * RW_WIDTH, 3 * RW_WIDTH + RW_DECAY_RANK,
             3 * RW_WIDTH + RW_DECAY_RANK + RW_AAA_RANK)
RW_GN_EPS = 64e-5
HY_WIDTH = D_MODEL - RW_WIDTH
HY_COLS = 3 * HY_WIDTH
HY_BANDS = 16
HY_EMB = 1 + 2 * HY_BANDS
HY_ORDER = 64
HY_FAST_DECAY = 0.3
HY_SLOW_DECAY = 1.5
HY_TARGET = 1e-2
AB_COLS = RW_COLS + HY_COLS
DN_HEADS = 8
DN_HEAD_DIM = D_MODEL // DN_HEADS
DN_DIM = DN_HEADS * DN_HEAD_DIM
DN_CHUNK = 64
DN_COLS = 4 * DN_DIM + 4 * DN_HEADS

kernel_name = 'hybrid_rwkv7_hyena_gdn_diffusion_trunk'


def rmsnorm(x, w):
    xf = x.astype(jnp.float32)
    y = xf * lax.rsqrt(jnp.mean(xf * xf, axis=-1, keepdims=True) + EPS)
    return y.astype(x.dtype) * w


def modulate(h, shift, scale):
    return h * (1 + scale) + shift


def l2norm(t):
    tf = t.astype(jnp.float32)
    return tf * lax.rsqrt(jnp.sum(tf * tf, axis=-1, keepdims=True) + EPS)


def short_conv(u, w):
    pad = SHORT_CONV // 2
    return lax.conv_general_dilated(u, w[:, None, :], window_strides=(1,), padding=((pad, pad),),
                                    dimension_numbers=('NWC', 'WIO', 'NWC'),
                                    feature_group_count=u.shape[-1])


def token_shift_mix(p, mu):
    prev = jnp.pad(p, ((0, 0), (1, 0), (0, 0)))[:, :-1]
    nxt = jnp.pad(p, ((0, 0), (0, 1), (0, 0)))[:, 1:]
    return p + mu * (0.5 * (prev + nxt) - p)


def rwkv_prep(p, mu, w0, w_up, a0, a_up, g_up, k_k, k_a):
    B, L, _ = p.shape
    p = token_shift_mix(p, mu).astype(jnp.float32)
    r, k, v, wl, al, gl = jnp.split(p, RW_SPLITS, axis=-1)
    heads = lambda t: t.reshape(B, L, RW_HEADS, RW_HEAD_DIM)
    kk = l2norm(heads(k * k_k))
    wl = jnp.tanh(wl)
    dirs = []
    for d in range(2):
        w_log = -jax.nn.softplus(-(w0[d] + wl @ w_up[d])) - 0.5
        a = jax.nn.sigmoid(a0[d] + al @ a_up[d])
        k_d = k * (1 + (a - 1) * k_a)
        dirs.append((heads(jnp.exp(-jnp.exp(w_log))), heads(k_d), heads(a) * kk))
    g = jax.nn.sigmoid(gl) @ g_up
    return heads(r), heads(v), kk, g, dirs


def rwkv_scan(S0, r, v, kk, decay, k, b, reverse):
    xs = tuple(jnp.moveaxis(t, 1, 0) for t in (r, v, kk, decay, k, b))

    def step(S, inp):
        r_t, v_t, kk_t, w_t, k_t, b_t = inp
        sa = -jnp.einsum('bhvk,bhk->bhv', S, kk_t)
        S = S * w_t[:, :, None, :] + sa[..., None] * b_t[:, :, None, :] + v_t[..., None] * k_t[:, :, None, :]
        return S, jnp.einsum('bhvk,bhk->bhv', S, r_t)

    S, ys = lax.scan(step, S0, xs, reverse=reverse)
    return jnp.moveaxis(ys, 0, 1), S


def rwkv_output(y, prep, r_k, ln_w, ln_b):
    r, v, _, g, dirs = prep
    B, L = y.shape[:2]
    mean = jnp.mean(y, axis=-1, keepdims=True)
    var = jnp.mean(jnp.square(y - mean), axis=-1, keepdims=True)
    yn = (y - mean) * lax.rsqrt(var + RW_GN_EPS)
    bonus = sum(jnp.sum(r * k_d * r_k, axis=-1, keepdims=True) for _, k_d, _ in dirs) * v
    return (yn.reshape(B, L, RW_WIDTH) * ln_w + ln_b + bonus.reshape(B, L, RW_WIDTH)) * g


def hyena_filter(L, f_w1, f_b1, f_w2, f_b2, f_w3, f_b3, f_w4, freq):
    t = jnp.linspace(0.0, 1.0, L, dtype=jnp.float32)[:, None]
    w = 2 * math.pi * jnp.arange(L, dtype=jnp.float32)[:, None] / L
    f = jnp.linspace(1e-4, HY_BANDS - 1, HY_BANDS, dtype=jnp.float32)[None, :]
    z = jnp.concatenate([t, jnp.cos(f * w), -jnp.sin(f * w)], axis=-1)
    h = jnp.sin(freq * (z @ f_w1 + f_b1))
    h = jnp.sin(freq * (h @ f_w2 + f_b2))
    h = jnp.sin(freq * (h @ f_w3 + f_b3))
    h = h @ f_w4
    deltas = jnp.abs(jnp.linspace(math.log(HY_TARGET) / HY_SLOW_DECAY, math.log(HY_TARGET) / HY_FAST_DECAY,
                                  HY_WIDTH, dtype=jnp.float32))
    h = h * jnp.exp(-t * jnp.tile(deltas, 2))
    h_f, h_b = jnp.split(h, 2, axis=-1)
    kern = jnp.concatenate([h_f, jnp.zeros((1, HY_WIDTH), h.dtype), h_b[:0:-1]], axis=0)
    return kern / jnp.sum(jnp.abs(kern), axis=0, keepdims=True)


def hyena_seq(u, conv_w, conv_b, skip, filt):
    L = u.shape[1]
    u = short_conv(u, conv_w) + conv_b
    x0, x1, v = jnp.split(u, 3, axis=-1)
    s = (x1 * v).astype(jnp.float32)
    kern = hyena_filter(L, *filt)
    y = jnp.fft.irfft(jnp.fft.rfft(s, n=2 * L, axis=1) * jnp.fft.rfft(kern, axis=0)[None], n=2 * L, axis=1)[:, :L]
    return x0 * (y + s * skip)


def rwkv_hyena_mixer(h_ctx, h_lat, w_in, w_out, rw_in, rw_out, hy_conv, hy_filt, with_ctx_out):
    p_ctx, p_lat = h_ctx @ w_in, h_lat @ w_in
    prep_c = rwkv_prep(p_ctx[..., :RW_COLS], *rw_in)
    prep_l = rwkv_prep(p_lat[..., :RW_COLS], *rw_in)
    S0 = jnp.zeros((h_lat.shape[0], RW_HEADS, RW_HEAD_DIM, RW_HEAD_DIM), jnp.float32)
    y_c, y_l = 0.0, 0.0
    for d, rev in enumerate((False, True)):
        yc_d, S_c = rwkv_scan(S0, prep_c[0], prep_c[1], prep_c[2], *prep_c[4][d], reverse=rev)
        yl_d, _ = rwkv_scan(S_c, prep_l[0], prep_l[1], prep_l[2], *prep_l[4][d], reverse=rev)
        y_c, y_l = y_c + yc_d, y_l + yl_d
    a_lat = rwkv_output(y_l, prep_l, *rw_out)
    b_lat = hyena_seq(p_lat[..., RW_COLS:], *hy_conv, hy_filt)
    y_lat = jnp.concatenate([a_lat, b_lat], axis=-1).astype(h_lat.dtype) @ w_out
    y_ctx = None
    if with_ctx_out:
        a_ctx = rwkv_output(y_c, prep_c, *rw_out)
        b_ctx = hyena_seq(p_ctx[..., RW_COLS:], *hy_conv, hy_filt)
        y_ctx = jnp.concatenate([a_ctx, b_ctx], axis=-1).astype(h_ctx.dtype) @ w_out
    return y_ctx, y_lat


def to_chunks(t):
    B, L, H = t.shape[:3]
    t = t.reshape((B, L // DN_CHUNK, DN_CHUNK, H) + t.shape[3:])
    return jnp.moveaxis(t, 3, 2).swapaxes(0, 1)


def from_chunks(t):
    t = jnp.moveaxis(t.swapaxes(0, 1), 2, 3)
    return t.reshape((t.shape[0], -1) + t.shape[3:])


def gdn_chunked(q, k, v, g, beta, S0):
    K = q.shape[-1]
    q = to_chunks(q.astype(jnp.float32) * K ** -0.5)
    k = to_chunks(k.astype(jnp.float32))
    v = to_chunks(v.astype(jnp.float32))
    g = jnp.cumsum(to_chunks(g.astype(jnp.float32)), axis=-1)
    beta = to_chunks(beta.astype(jnp.float32))
    idx = jnp.arange(DN_CHUNK)
    incl = idx[:, None] >= idx[None, :]
    strict = idx[:, None] > idx[None, :]
    diff = g[..., :, None] - g[..., None, :]
    decay = jnp.where(incl, jnp.exp(jnp.where(incl, diff, 0.0)), 0.0)
    kb = k * beta[..., None]
    lower = jnp.where(strict, jnp.einsum('nbhck,nbhsk->nbhcs', kb, k) * decay, 0.0)
    A = lower + jnp.eye(DN_CHUNK, dtype=jnp.float32)
    solve = lambda rhs: lax.linalg.triangular_solve(A, rhs, left_side=True, lower=True)
    u = solve(v * beta[..., None])
    w = solve(kb * jnp.exp(g)[..., None])
    qk = jnp.einsum('nbhck,nbhsk->nbhcs', q, k) * decay

    def step(S, inp):
        q_c, k_c, u_c, w_c, g_c, qk_c = inp
        v_new = u_c - jnp.einsum('bhck,bhkv->bhcv', w_c, S)
        o = (jnp.einsum('bhck,bhkv->bhcv', q_c * jnp.exp(g_c)[..., None], S)
             + jnp.einsum('bhcs,bhsv->bhcv', qk_c, v_new))
        g_last = g_c[..., -1:]
        S = S * jnp.exp(g_last)[..., None] + jnp.einsum(
            'bhck,bhcv->bhkv', k_c * jnp.exp(g_last - g_c)[..., None], v_new)
        return S, o

    S, o = lax.scan(step, S0, (q, k, u, w, g, qk))
    return from_chunks(o), S


def gdn_prep(p, conv_w, A_log, dt_bias):
    B, L, _ = p.shape
    qkv, z, a, b = jnp.split(p, (3 * DN_DIM, 4 * DN_DIM, 4 * DN_DIM + 2 * DN_HEADS), axis=-1)
    qkv = jax.nn.silu(short_conv(qkv, conv_w))
    q, k, v = [t.reshape(B, L, DN_HEADS, DN_HEAD_DIM) for t in jnp.split(qkv, 3, axis=-1)]
    a = a.reshape(B, L, 2, DN_HEADS).astype(jnp.float32)
    b = b.reshape(B, L, 2, DN_HEADS).astype(jnp.float32)
    g = -jnp.exp(A_log) * jax.nn.softplus(a + dt_bias)
    beta = jax.nn.sigmoid(b)
    return l2norm(q), l2norm(k), v, z, g, beta


def gated_rmsnorm(o, z, w):
    B, L = o.shape[:2]
    on = o * lax.rsqrt(jnp.mean(o * o, axis=-1, keepdims=True) + EPS) * w
    return (on.reshape(B, L, DN_DIM) * jax.nn.silu(z.astype(jnp.float32))).astype(z.dtype)


def deltanet_mixer(h_ctx, h_lat, w_in, conv_w, A_log, dt_bias, norm_w, w_out, with_ctx_out):
    q_c, k_c, v_c, z_c, g_c, b_c = gdn_prep(h_ctx @ w_in, conv_w, A_log, dt_bias)
    q_l, k_l, v_l, z_l, g_l, b_l = gdn_prep(h_lat @ w_in, conv_w, A_log, dt_bias)
    S0 = jnp.zeros((h_lat.shape[0], DN_HEADS, DN_HEAD_DIM, DN_HEAD_DIM), jnp.float32)
    o_c, o_l = 0.0, 0.0
    for d in range(2):
        fl = (lambda t: jnp.flip(t, axis=1)) if d == 1 else (lambda t: t)
        oc_d, S_c = gdn_chunked(fl(q_c), fl(k_c), fl(v_c), fl(g_c[:, :, d]), fl(b_c[:, :, d]), S0)
        ol_d, _ = gdn_chunked(fl(q_l), fl(k_l), fl(v_l), fl(g_l[:, :, d]), fl(b_l[:, :, d]), S_c)
        o_c, o_l = o_c + fl(oc_d), o_l + fl(ol_d)
    y_lat = gated_rmsnorm(o_l, z_l, norm_w) @ w_out
    y_ctx = gated_rmsnorm(o_c, z_c, norm_w) @ w_out if with_ctx_out else None
    return y_ctx, y_lat


def sq_relu_mlp(h, w1, w2):
    return jnp.square(jax.nn.relu(h @ w1)) @ w2


def setup_inputs(seed: int = 0) -> dict:
    key = jax.random.key(seed)
    keys = iter(jax.random.split(key, 48))
    f32 = jnp.float32
    nrm = lambda shape, scale=1.0: jax.random.normal(next(keys), shape, f32) * scale
    uni = lambda shape, lo, hi: jax.random.uniform(next(keys), shape, f32, lo, hi)
    D = D_MODEL
    NE = (DEPTH + 1) // 2
    NO = DEPTH // 2
    dt = jnp.exp(uni((NO, 2, DN_HEADS), math.log(1e-3), math.log(1e-1)))
    return {
        'x': nrm((BATCH, SEQ, D)),
        'c': nrm((BATCH, D)),
        'ctx': nrm((BATCH, CTX_LEN, D)),
        'c_ctx': nrm((D,)),
        'ada_w': nrm((DEPTH, D, ADA_CHUNKS * D), 0.5 * D ** -0.5),
        'ada_b': nrm((DEPTH, ADA_CHUNKS * D), 0.02),
        'norm_mix': 1.0 + nrm((DEPTH, D), 0.02),
        'norm_mlp': 1.0 + nrm((DEPTH, D), 0.02),
        'mlp_w1': nrm((DEPTH, D, MLP_HIDDEN), D ** -0.5),
        'mlp_w2': nrm((DEPTH, MLP_HIDDEN, D), MLP_HIDDEN ** -0.5),
        'final_norm': 1.0 + nrm((D,), 0.02),
        'ab_w_in': nrm((NE, D, AB_COLS), D ** -0.5),
        'ab_w_out': nrm((NE, D, D), D ** -0.5),
        'rw_mu': uni((NE, RW_COLS), 0.0, 1.0),
        'rw_w0': uni((NE, 2, RW_WIDTH), -6.0, -1.0),
        'rw_w_up': nrm((NE, 2, RW_DECAY_RANK, RW_WIDTH), 0.1),
        'rw_a0': nrm((NE, 2, RW_WIDTH), 0.5),
        'rw_a_up': nrm((NE, 2, RW_AAA_RANK, RW_WIDTH), RW_AAA_RANK ** -0.5),
        'rw_g_up': nrm((NE, RW_GATE_RANK, RW_WIDTH), RW_GATE_RANK ** -0.5),
        'rw_k_k': 0.85 + nrm((NE, RW_WIDTH), 0.02),
        'rw_k_a': 1.0 + nrm((NE, RW_WIDTH), 0.02),
        'rw_r_k': nrm((NE, RW_HEADS, RW_HEAD_DIM), 0.1),
        'rw_ln_w': 1.0 + nrm((NE, RW_WIDTH), 0.02),
        'rw_ln_b': nrm((NE, RW_WIDTH), 0.02),
        'hy_conv_w': nrm((NE, SHORT_CONV, HY_COLS), SHORT_CONV ** -0.5),
        'hy_conv_b': nrm((NE, HY_COLS), 0.02),
        'hy_f_w1': nrm((NE, HY_EMB, HY_ORDER), HY_EMB ** -0.5),
        'hy_f_b1': nrm((NE, HY_ORDER), 0.1),
        'hy_f_w2': nrm((NE, HY_ORDER, HY_ORDER), HY_ORDER ** -0.5),
        'hy_f_b2': nrm((NE, HY_ORDER), 0.1),
        'hy_f_w3': nrm((NE, HY_ORDER, HY_ORDER), HY_ORDER ** -0.5),
        'hy_f_b3': nrm((NE, HY_ORDER), 0.1),
        'hy_f_w4': nrm((NE, HY_ORDER, 2 * HY_WIDTH), HY_ORDER ** -0.5),
        'hy_freq': 1.0 + nrm((NE, HY_ORDER), 0.02),
        'hy_skip': nrm((NE, HY_WIDTH)),
        'dn_w_in': nrm((NO, D, DN_COLS), D ** -0.5),
        'dn_conv_w': nrm((NO, SHORT_CONV, 3 * DN_DIM), SHORT_CONV ** -0.5),
        'dn_A_log': jnp.log(uni((NO, 2, DN_HEADS), 1.0, 16.0)),
        'dn_dt_bias': dt + jnp.log(-jnp.expm1(-dt)),
        'dn_norm': 1.0 + nrm((NO, DN_HEAD_DIM), 0.02),
        'dn_w_out': nrm((NO, DN_DIM, D), DN_DIM ** -0.5),
    }


def reference(x, c, ctx, c_ctx, ada_w, ada_b, norm_mix, norm_mlp, mlp_w1, mlp_w2, final_norm,
              ab_w_in, ab_w_out, rw_mu, rw_w0, rw_w_up, rw_a0, rw_a_up, rw_g_up, rw_k_k, rw_k_a,
              rw_r_k, rw_ln_w, rw_ln_b, hy_conv_w, hy_conv_b, hy_f_w1, hy_f_b1, hy_f_w2, hy_f_b2,
              hy_f_w3, hy_f_b3, hy_f_w4, hy_freq, hy_skip, dn_w_in, dn_conv_w, dn_A_log, dn_dt_bias,
              dn_norm, dn_w_out):
    rows = x.shape[1] // GRID_W
    assert rows * GRID_W == x.shape[1]
    x_lat, x_ctx = x, ctx
    for l in range(DEPTH):
        last = l == DEPTH - 1
        mod_lat = jax.nn.silu(c) @ ada_w[l] + ada_b[l]
        mod_ctx = jax.nn.silu(c_ctx) @ ada_w[l] + ada_b[l]
        ml = jnp.split(mod_lat[:, None, :], ADA_CHUNKS, axis=-1)
        mc = jnp.split(mod_ctx, ADA_CHUNKS, axis=-1)
        h_lat = modulate(rmsnorm(x_lat, norm_mix[l]), ml[0], ml[1])
        h_ctx = modulate(rmsnorm(x_ctx, norm_mix[l]), mc[0], mc[1])
        if l % 2 == 0:
            e = l // 2
            y_ctx, y_lat = rwkv_hyena_mixer(
                h_ctx, h_lat, ab_w_in[e], ab_w_out[e],
                (rw_mu[e], rw_w0[e], rw_w_up[e], rw_a0[e], rw_a_up[e], rw_g_up[e], rw_k_k[e], rw_k_a[e]),
                (rw_r_k[e], rw_ln_w[e], rw_ln_b[e]),
                (hy_conv_w[e], hy_conv_b[e], hy_skip[e]),
                (hy_f_w1[e], hy_f_b1[e], hy_f_w2[e], hy_f_b2[e], hy_f_w3[e], hy_f_b3[e], hy_f_w4[e], hy_freq[e]),
                with_ctx_out=not last)
        else:
            o = l // 2
            y_ctx, y_lat = deltanet_mixer(h_ctx, h_lat, dn_w_in[o], dn_conv_w[o], dn_A_log[o], dn_dt_bias[o],
                                          dn_norm[o], dn_w_out[o], with_ctx_out=not last)
        x_lat = x_lat + ml[2] * y_lat
        x_lat = x_lat + ml[5] * sq_relu_mlp(modulate(rmsnorm(x_lat, norm_mlp[l]), ml[3], ml[4]), mlp_w1[l], mlp_w2[l])
        if not last:
            x_ctx = x_ctx + mc[2] * y_ctx
            x_ctx = x_ctx + mc[5] * sq_relu_mlp(modulate(rmsnorm(x_ctx, norm_mlp[l]), mc[3], mc[4]), mlp_w1[l], mlp_w2[l])
    return rmsnorm(x_lat, final_norm)
```

```python
import functools
import math

import jax
import jax.numpy as jnp
from jax import lax
from jax.experimental import pallas as pl
from jax.experimental.pallas import tpu as pltpu

F32, BF16 = jnp.float32, jnp.bfloat16

D_MODEL = 1024
SEQ = 2048
CTX_LEN = 256
ADA_CHUNKS = 6
EPS = 1e-6
RW_WIDTH = 512
RW_HEAD_DIM = 64
RW_HEADS = 8
RW_COLS = 1792
RW_GN_EPS = 64e-5
HY_WIDTH = 512
HY_COLS = 1536
HY_BANDS = 16
HY_EMB = 33
HY_ORDER = 64
HY_FAST_DECAY = 0.3
HY_SLOW_DECAY = 1.5
HY_TARGET = 1e-2
AB_COLS = RW_COLS + HY_COLS
DN_HEADS = 8
DN_HEAD_DIM = 128
DN_DIM = 1024
DN_COLS = 4 * DN_DIM + 4 * DN_HEADS
DN_COLS_PAD = 4 * DN_DIM + 128

TB = 256
CH = 64
T_ALL = SEQ + CTX_LEN
NB = T_ALL // TB
NB_LAT = SEQ // TB
NC = T_ALL // CH
NC_LAT = SEQ // CH
FB = 256
VMEM_LIMIT = 56 * 1024 * 1024

NN = ((1,), (0,))
NT = ((1,), (1,))
TN = ((0,), (0,))


def _dot(a, b, dims=NN):
    return lax.dot_general(a, b, (dims, ((), ())), preferred_element_type=F32)


def _split(x):
    hi = x.astype(BF16)
    lo = (x - hi.astype(F32)).astype(BF16)
    return hi, lo


def _mm(a, b, dims=NN, pa=1, pb=1):
    if a.dtype == BF16:
        a_hi, a_lo, pa = a, None, 1
    elif pa == 2:
        a_hi, a_lo = _split(a)
    else:
        a_hi, a_lo = a.astype(BF16), None
    if b.dtype == BF16:
        b_hi, b_lo, pb = b, None, 1
    elif pb == 2:
        b_hi, b_lo = _split(b)
    else:
        b_hi, b_lo = b.astype(BF16), None
    out = _dot(a_hi, b_hi, dims)
    if pa == 2:
        out = out + _dot(a_lo, b_hi, dims)
    if pb == 2:
        out = out + _dot(a_hi, b_lo, dims)
    return out


def _sigmoid(x):
    return 1.0 / (1.0 + jnp.exp(-x))


def _softplus(x):
    return jnp.maximum(x, 0.0) + jnp.log(1.0 + jnp.exp(-jnp.abs(x)))


def _rms_mod(x, nw, shift, scale):
    y = x * lax.rsqrt(jnp.mean(x * x, axis=-1, keepdims=True) + EPS)
    return (y * nw) * (1.0 + scale) + shift


def _cparams(*sem):
    return pltpu.CompilerParams(dimension_semantics=sem, vmem_limit_bytes=VMEM_LIMIT)


ADA_TN = 1536


def _ada_kernel(c_ref, w_ref, b_ref, o_ref):
    c = c_ref[...]
    a = c * _sigmoid(c)
    o_ref[0] = _mm(a, w_ref[0], NN, 2, 2) + b_ref[0]


def ada_modulation(cs, ada_w, ada_b):
    depth, d, n = ada_w.shape
    rows = cs.shape[0]
    return pl.pallas_call(
        _ada_kernel,
        grid=(depth, n // ADA_TN),
        in_specs=[pl.BlockSpec((rows, d), lambda l, j: (0, 0)),
                  pl.BlockSpec((1, d, ADA_TN), lambda l, j: (l, 0, j)),
                  pl.BlockSpec((1, 1, ADA_TN), lambda l, j: (l, 0, j))],
        out_specs=pl.BlockSpec((1, rows, ADA_TN), lambda l, j: (l, 0, j)),
        out_shape=jax.ShapeDtypeStruct((depth, rows, n), F32),
        compiler_params=_cparams("parallel", "parallel"),
        name="ada_modulation",
    )(cs, ada_w, ada_b.reshape(depth, 1, n))


def _nm_matmul_kernel(x_ref, nw_ref, mod_ref, w_ref, o_ref):
    m = mod_ref[0, 0]
    h = _rms_mod(x_ref[0], nw_ref[...], m[0:1], m[1:2])
    o_ref[0] = _dot(h.astype(BF16), w_ref[...])


def nm_matmul(xs, nw, modv, w):
    b, t, d = xs.shape
    n = w.shape[1]
    return pl.pallas_call(
        _nm_matmul_kernel,
        grid=(b, t // TB),
        in_specs=[pl.BlockSpec((1, TB, d), lambda i, j: (i, j, 0)),
                  pl.BlockSpec((1, d), lambda i, j: (0, 0)),
                  pl.BlockSpec((1, 1, ADA_CHUNKS, d), lambda i, j: (i, j // NB_LAT, 0, 0)),
                  pl.BlockSpec((d, n), lambda i, j: (0, 0))],
        out_specs=pl.BlockSpec((1, TB, n), lambda i, j: (i, j, 0)),
        out_shape=jax.ShapeDtypeStruct((b, t, n), F32),
        compiler_params=_cparams("parallel", "parallel"),
        name="nm_matmul",
    )(xs, nw.reshape(1, d), modv, w)


def _halo_specs(width, col_block=0):
    prev = pl.BlockSpec((1, 8, width), lambda i, j: (i, jnp.maximum(j * (TB // 8) - 1, 0), col_block))
    nxt = pl.BlockSpec((1, 8, width), lambda i, j: (i, jnp.minimum((j + 1) * (TB // 8), T_ALL // 8 - 1), col_block))
    return prev, nxt


def _neighbours(x, prev_ref, next_ref):
    j = pl.program_id(1)
    pv = jnp.where(jnp.logical_and(j != 0, j != NB - 1), 1.0, 0.0).astype(F32)
    nv = jnp.where(j < NB - 2, 1.0, 0.0).astype(F32)
    prow = prev_ref[0, 7:8, :] * pv
    nrow = next_ref[0, 0:1, :] * nv
    rows = lax.broadcasted_iota(jnp.int32, (TB, 1), 0)
    xprev = jnp.where(rows == 0, prow, pltpu.roll(x, 1, 0))
    xnext = jnp.where(rows == TB - 1, nrow, pltpu.roll(x, TB - 1, 0))
    return xprev, xnext


def _ab_prep_kernel(p_ref, pp_ref, pn_ref, mu_ref, lora_ref, w0_ref, a0_ref, gup_ref, kk_ref, ka_ref, rk_ref,
                    bd_ref, cw_ref, cb_ref,
                    r_ref, v_ref, kkn_ref, g_ref, bv_ref, lw_ref, kd_ref, b_ref, s_ref, x0_ref):
    x = p_ref[0]
    xprev, xnext = _neighbours(x, pp_ref, pn_ref)
    w = RW_WIDTH
    xr = x[:, :RW_COLS]
    ps = xr + mu_ref[...] * (0.5 * (xprev[:, :RW_COLS] + xnext[:, :RW_COLS]) - xr)
    r, k, v = ps[:, :w], ps[:, w:2 * w], ps[:, 2 * w:3 * w]
    slab = ps[:, 3 * w:3 * w + 128]
    gl = ps[:, 3 * w + 128:3 * w + 256]
    lane = lax.broadcasted_iota(jnp.int32, (1, 128), 1)
    z = jnp.where(lane < 64, jnp.tanh(slab), slab)
    lora = _dot(z.astype(BF16), lora_ref[...])
    bd = bd_ref[...]
    kkraw = k * kk_ref[...]
    kk = kkraw * lax.rsqrt(_mm(kkraw * kkraw, bd, NN, 2, 1) + EPS)
    ksum = jnp.zeros_like(k)
    for d in range(2):
        w_log = -_softplus(-(w0_ref[:, d * w:(d + 1) * w] + lora[:, d * w:(d + 1) * w])) - 0.5
        lw_ref[d, 0] = -jnp.exp(w_log)
        a = _sigmoid(a0_ref[:, d * w:(d + 1) * w] + lora[:, (2 + d) * w:(3 + d) * w])
        kd = k * (1.0 + (a - 1.0) * ka_ref[...])
        kd_ref[d, 0] = kd
        b_ref[d, 0] = a * kk
        ksum = ksum + kd
    r_ref[0] = r
    v_ref[0] = v
    kkn_ref[0] = kk
    g_ref[0] = _dot(_sigmoid(gl).astype(BF16), gup_ref[...])
    bv_ref[0] = _mm(r * ksum * rk_ref[...], bd, NN, 2, 1) * v
    u = x[:, RW_COLS:]
    c = (cw_ref[0:1, :] * xprev[:, RW_COLS:] + cw_ref[1:2, :] * u + cw_ref[2:3, :] * xnext[:, RW_COLS:]
         + cb_ref[...])
    hw = HY_WIDTH
    x0_ref[0] = c[:, :hw]
    s_ref[0] = c[:, hw:2 * hw] * c[:, 2 * hw:]


def ab_prep(p, mu, lora, w0, a0, gup, k_k, k_a, r_k, bd64, conv_w, conv_b):
    b, t, n = p.shape
    w = RW_WIDTH
    prev, nxt = _halo_specs(n)
    full = lambda a: pl.BlockSpec(a.shape, lambda i, j: (0,) * a.ndim)
    params = (mu, lora, w0, a0, gup, k_k, k_a, r_k, bd64, conv_w, conv_b)
    tok = pl.BlockSpec((1, TB, w), lambda i, j: (i, j, 0))
    tok2 = pl.BlockSpec((2, 1, TB, w), lambda i, j: (0, i, j, 0))
    s1 = jax.ShapeDtypeStruct((b, t, w), F32)
    s2 = jax.ShapeDtypeStruct((2, b, t, w), F32)
    return pl.pallas_call(
        _ab_prep_kernel,
        grid=(b, t // TB),
        in_specs=[pl.BlockSpec((1, TB, n), lambda i, j: (i, j, 0)), prev, nxt] + [full(a) for a in params],
        out_specs=[tok, tok, tok, tok, tok, tok2, tok2, tok2, tok, tok],
        out_shape=[s1, s1, s1, s1, s1, s2, s2, s2, s1, s1],
        compiler_params=_cparams("parallel", "parallel"),
        name="ab_prep",
    )(p, p, p, *params)


def _chunk_masks(reverse):
    c = CH
    ti = lax.broadcasted_iota(jnp.int32, (c, c), 0)
    ji = lax.broadcasted_iota(jnp.int32, (c, c), 1)
    strict = (ji > ti) if reverse else (ji < ti)
    incl = jnp.logical_or(strict, ti == ji)
    return ti, ji, strict, incl


def _tri_inverse(n_mat, ti, ji):
    eye = (ti == ji).astype(F32)
    dinv = eye - jnp.where((ti >> 1) == (ji >> 1), n_mat, 0.0)
    s = 1
    while (2 << s) <= CH:
        off = jnp.logical_and((ti >> (s + 1)) == (ji >> (s + 1)), (ti >> s) != (ji >> s))
        coff = jnp.where(off, n_mat, 0.0)
        dinv = dinv - _mm(_mm(dinv, coff, NN, 2, 2), dinv, NN, 2, 2)
        s += 1
    return dinv


def _chunk_update(s_ref, h, a_kk, a_bk, a_rk, a_rb, kkt, rt, kh, bh, vh, dec_c, ti, ji, kdim):
    tm = _tri_inverse(a_bk, ti, ji)
    akkv = _mm(a_kk, vh, NN, 2, 2)
    tw = _mm(tm, jnp.concatenate([kkt, akkv], axis=1), NN, 2, 2)
    wmat, u0 = tw[:, :kdim], tw[:, kdim:]
    y0 = _mm(a_rk, vh, NN, 2, 2)
    s = s_ref[h]
    p = _mm(jnp.concatenate([wmat, rt], axis=0), s, NT, 2, 2)
    u = p[:CH] + u0
    y = p[CH:] + y0 - _mm(a_rb, u, NN, 2, 2)
    s_ref[h] = s * dec_c + _mm(jnp.concatenate([vh, -u], axis=0), jnp.concatenate([kh, bh], axis=0), TN, 2, 2)
    return y


def _rwkv_scan_kernel(r_ref, v_ref, kk_ref, lw_ref, kd_ref, b_ref, y_ref, s_ref, *, reverse):
    c, kdim = CH, RW_HEAD_DIM

    @pl.when(pl.program_id(1) == 0)
    def _():
        s_ref[...] = jnp.zeros_like(s_ref)

    ti, ji, strict, incl = _chunk_masks(reverse)
    lw = lw_ref[0, 0]
    g = _mm(incl.astype(BF16), lw, NN, 1, 2)
    gx = g - lw
    last = 0 if reverse else c - 1
    gc = g[last:last + 1, :]
    r, v, kk, kd, bb = r_ref[0], v_ref[0], kk_ref[0], kd_ref[0, 0], b_ref[0, 0]
    eng = jnp.exp(-g)
    egc = jnp.exp(gc - g)
    rt_all = r * jnp.exp(g)
    kkt_all = kk * jnp.exp(gx)
    kn_all, bn_all = kd * eng, bb * eng
    kh_all, bh_all = kd * egc, bb * egc
    dec_all = jnp.exp(gc)
    ri = lax.broadcasted_iota(jnp.int32, (2 * c, 2 * c), 0)
    ci = lax.broadcasted_iota(jnp.int32, (2 * c, 2 * c), 1)
    rt_, ct_ = ri & (c - 1), ci & (c - 1)
    before = (ct_ > rt_) if reverse else (ct_ < rt_)
    amask = jnp.logical_or(before, jnp.logical_and(rt_ == ct_, ri >= c))
    for h in range(RW_HEADS):
        sl = slice(h * kdim, (h + 1) * kdim)
        lhs = jnp.concatenate([kkt_all[:, sl], rt_all[:, sl]], axis=0)
        rhs = jnp.concatenate([kn_all[:, sl], bn_all[:, sl]], axis=0)
        a = jnp.where(amask, _mm(lhs, rhs, NT, 2, 2), 0.0)
        y_ref[0, :, sl] = _chunk_update(s_ref, h, a[:c, :c], a[:c, c:], a[c:, :c], a[c:, c:],
                                        kkt_all[:, sl], rt_all[:, sl], kh_all[:, sl], bh_all[:, sl], v[:, sl],
                                        dec_all[:, sl], ti, ji, kdim)


def _chunk_index(reverse):
    if reverse:
        return lambda i: NC - 1 - i
    return lambda i: (i + NC_LAT) % NC


def rwkv_scan(r, v, kk, lw, kd, bb, direction):
    b, t, w = r.shape
    reverse = direction == 1
    cidx = _chunk_index(reverse)
    tok = pl.BlockSpec((1, CH, w), lambda i, j: (i, cidx(j), 0))
    tokd = pl.BlockSpec((1, 1, CH, w), lambda i, j: (direction, i, cidx(j), 0))
    return pl.pallas_call(
        functools.partial(_rwkv_scan_kernel, reverse=reverse),
        grid=(b, NC),
        in_specs=[tok, tok, tok, tokd, tokd, tokd],
        out_specs=tok,
        out_shape=jax.ShapeDtypeStruct((b, t, w), F32),
        scratch_shapes=[pltpu.VMEM((RW_HEADS, RW_HEAD_DIM, RW_HEAD_DIM), F32)],
        compiler_params=_cparams("parallel", "arbitrary"),
        name="rwkv_scan_rev" if reverse else "rwkv_scan_fwd",
    )(r, v, kk, lw, kd, bb)


def _hy_filter_kernel(z_ref, t_ref, w1_ref, b1_ref, w2_ref, b2_ref, w3_ref, b3_ref, w4_ref, fr_ref, dl_ref, o_ref):
    fr = fr_ref[...]
    h = jnp.sin(fr * (_mm(z_ref[...], w1_ref[...], NN, 2, 2) + b1_ref[...]))
    h = jnp.sin(fr * (_mm(h, w2_ref[...], NN, 2, 2) + b2_ref[...]))
    h = jnp.sin(fr * (_mm(h, w3_ref[...], NN, 2, 2) + b3_ref[...]))
    h = _mm(h, w4_ref[...], NN, 2, 2)
    h = h * jnp.exp(-t_ref[...] * dl_ref[...])
    hw = HY_WIDTH
    rows = lax.broadcasted_iota(jnp.int32, (h.shape[0], 1), 0)
    hf = h[:, :hw]
    hb = jnp.where(rows == 0, 0.0, h[:, hw:])
    norm = jnp.sum(jnp.abs(hf), axis=0, keepdims=True) + jnp.sum(jnp.abs(hb), axis=0, keepdims=True)
    o_ref[:, :hw] = (hf + hb) / norm
    o_ref[:, hw:] = (hf - hb) / norm


def hyena_filter_halves(length, w1, b1, w2, b2, w3, b3, w4, freq):
    t = jnp.linspace(0.0, 1.0, length, dtype=F32)[:, None]
    wv = 2 * math.pi * jnp.arange(length, dtype=F32)[:, None] / length
    f = jnp.linspace(1e-4, HY_BANDS - 1, HY_BANDS, dtype=F32)[None, :]
    z = jnp.concatenate([t, jnp.cos(f * wv), -jnp.sin(f * wv)], axis=-1)
    z = jnp.pad(z, ((0, 0), (0, HY_ORDER - HY_EMB)))
    w1p = jnp.pad(w1, ((0, HY_ORDER - HY_EMB), (0, 0)))
    deltas = jnp.abs(jnp.linspace(math.log(HY_TARGET) / HY_SLOW_DECAY, math.log(HY_TARGET) / HY_FAST_DECAY,
                                  HY_WIDTH, dtype=F32))
    dl = jnp.tile(deltas, 2)[None, :]
    row = lambda a: a.reshape(1, -1)
    return pl.pallas_call(
        _hy_filter_kernel,
        out_shape=jax.ShapeDtypeStruct((length, 2 * HY_WIDTH), F32),
        compiler_params=pltpu.CompilerParams(vmem_limit_bytes=VMEM_LIMIT),
        name="hyena_filter",
    )(z, t, w1p, row(b1), w2, row(b2), w3, row(b3), w4, row(freq), dl)


def dft_matrix(length):
    n = 2 * length
    f = jnp.arange(length, dtype=jnp.int32)[:, None]
    t = jnp.arange(length, dtype=jnp.int32)[None, :]
    ang = ((f * t) % n).astype(F32) * (2 * math.pi / n)
    cosm = jnp.cos(ang)
    msin = -jnp.sin(ang)
    nyq = jnp.where(t % 2 == 0, 1.0, -1.0).astype(F32)
    msin = jnp.where(f == 0, nyq, msin)
    nfb = length // FB
    wf = jnp.concatenate([cosm.reshape(nfb, FB, length), msin.reshape(nfb, FB, length)], axis=1)
    return wf.reshape(n, length).astype(BF16)


def _spectrum_kernel(w_ref, h_ref, o_ref):
    o_ref[...] = _mm(w_ref[...], h_ref[...], NN, 1, 2)


def filter_spectrum(wf, hpm):
    n, length = wf.shape
    cols = hpm.shape[1]
    return pl.pallas_call(
        _spectrum_kernel,
        grid=(n // (2 * FB),),
        in_specs=[pl.BlockSpec((2 * FB, length), lambda i: (i, 0)),
                  pl.BlockSpec((length, cols), lambda i: (0, 0))],
        out_specs=pl.BlockSpec((2 * FB, cols), lambda i: (i, 0)),
        out_shape=jax.ShapeDtypeStruct((n, cols), F32),
        compiler_params=_cparams("parallel"),
        name="filter_spectrum",
    )(wf, hpm)


def _hy_conv_kernel(s_ref, wf_ref, wft_ref, h_ref, o_ref, sb_ref, *, n_fft):
    fb = pl.program_id(1)

    @pl.when(fb == 0)
    def _():
        sb_ref[...] = s_ref[0].astype(BF16)
        o_ref[...] = jnp.zeros_like(o_ref)

    hw = HY_WIDTH
    x = _dot(wf_ref[...], sb_ref[...])
    xr, xi = x[:FB], x[FB:]
    hraw = h_ref[...]
    hr, hi, hn = hraw[:FB, :hw], hraw[FB:, hw:], hraw[FB:, :hw]
    fidx = lax.broadcasted_iota(jnp.int32, (FB, 1), 0) + fb * FB
    m = (fidx != 0).astype(F32)
    him = hi * m
    hc = hr * m + hn * (1.0 - m)
    cf = jnp.where(fidx != 0, 2.0 / n_fft, 1.0 / n_fft).astype(F32)
    yr = (xr * hr - xi * him) * cf
    yi = (xr * him + xi * hc) * cf
    y = jnp.concatenate([yr, yi], axis=0).astype(BF16)
    o_ref[0] += _dot(wft_ref[...], y)


def hyena_long_conv(s, wf, wft, hspec, length, block):
    b = s.shape[0]
    hw = HY_WIDTH
    return pl.pallas_call(
        functools.partial(_hy_conv_kernel, n_fft=2 * length),
        grid=(b, length // FB),
        in_specs=[pl.BlockSpec((1, length, hw), lambda i, j: (i, block, 0)),
                  pl.BlockSpec((2 * FB, length), lambda i, j: (j, 0)),
                  pl.BlockSpec((length, 2 * FB), lambda i, j: (0, j)),
                  pl.BlockSpec((2 * FB, 2 * hw), lambda i, j: (j, 0))],
        out_specs=pl.BlockSpec((1, length, hw), lambda i, j: (i, 0, 0)),
        out_shape=jax.ShapeDtypeStruct((b, length, hw), F32),
        scratch_shapes=[pltpu.VMEM((length, hw), BF16)],
        compiler_params=_cparams("parallel", "arbitrary"),
        name="hyena_long_conv",
    )(s, wf, wft, hspec)


def _ab_out_kernel(y0_ref, y1_ref, bv_ref, g_ref, x0_ref, cv_ref, s_ref, x_ref, mod_ref, lnw_ref, lnb_ref, skip_ref,
                   bd_ref, wo_ref, o_ref):
    bd = bd_ref[...]
    inv = 1.0 / RW_HEAD_DIM
    y = y0_ref[0] + y1_ref[0]
    mean = _mm(y, bd, NN, 2, 1) * inv
    yc = y - mean
    var = _mm(yc * yc, bd, NN, 2, 1) * inv
    yn = yc * lax.rsqrt(var + RW_GN_EPS)
    a = (yn * lnw_ref[...] + lnb_ref[...] + bv_ref[0]) * g_ref[0]
    s = s_ref[0]
    bh = x0_ref[0] * (cv_ref[0] + s * skip_ref[...])
    w = RW_WIDTH
    out = _dot(a.astype(BF16), wo_ref[:w, :]) + _dot(bh.astype(BF16), wo_ref[w:, :])
    o_ref[0] = x_ref[0] + mod_ref[0, 0][2:3] * out


def ab_out(y0, y1, bv, g, x0, cv, s, xs, modv, ln_w, ln_b, skip, bd64, w_out):
    b, t, d = xs.shape
    w = RW_WIDTH
    tok = pl.BlockSpec((1, TB, w), lambda i, j: (i, j, 0))
    full = lambda a: pl.BlockSpec(a.shape, lambda i, j: (0,) * a.ndim)
    params = (ln_w, ln_b, skip, bd64, w_out)
    return pl.pallas_call(
        _ab_out_kernel,
        grid=(b, t // TB),
        in_specs=[tok] * 7 + [pl.BlockSpec((1, TB, d), lambda i, j: (i, j, 0)),
                              pl.BlockSpec((1, 1, ADA_CHUNKS, d), lambda i, j: (i, j // NB_LAT, 0, 0))]
        + [full(a) for a in params],
        out_specs=pl.BlockSpec((1, TB, d), lambda i, j: (i, j, 0)),
        out_shape=jax.ShapeDtypeStruct((b, t, d), F32),
        compiler_params=_cparams("parallel", "parallel"),
        name="ab_out",
    )(y0, y1, bv, g, x0, cv, s, xs, modv, *params)


MLP_HC = 1024


def _mlp_kernel(x_ref, mod_ref, nw_ref, w1_ref, w2_ref, fw_ref, o_ref, *, final):
    x = x_ref[0]
    m = mod_ref[0, 0]
    h = _rms_mod(x, nw_ref[...], m[3:4], m[4:5]).astype(BF16)
    acc = jnp.zeros_like(x)
    for c in range(w1_ref.shape[1] // MLP_HC):
        hid = _dot(h, w1_ref[:, c * MLP_HC:(c + 1) * MLP_HC])
        hid = jnp.square(jnp.maximum(hid, 0.0))
        acc = acc + _dot(hid.astype(BF16), w2_ref[c * MLP_HC:(c + 1) * MLP_HC, :])
    o = x + m[5:6] * acc
    if final:
        o = o * lax.rsqrt(jnp.mean(o * o, axis=-1, keepdims=True) + EPS) * fw_ref[...]
    o_ref[0] = o


def mlp_block(xs, modv, nw, w1, w2, fw, n_blocks, final):
    b, t, d = xs.shape
    hdim = w1.shape[1]
    return pl.pallas_call(
        functools.partial(_mlp_kernel, final=final),
        grid=(b, n_blocks),
        in_specs=[pl.BlockSpec((1, TB, d), lambda i, j: (i, j, 0)),
                  pl.BlockSpec((1, 1, ADA_CHUNKS, d), lambda i, j: (i, j // NB_LAT, 0, 0)),
                  pl.BlockSpec((1, d), lambda i, j: (0, 0)),
                  pl.BlockSpec((d, hdim), lambda i, j: (0, 0)),
                  pl.BlockSpec((hdim, d), lambda i, j: (0, 0)),
                  pl.BlockSpec((1, d), lambda i, j: (0, 0))],
        out_specs=pl.BlockSpec((1, TB, d), lambda i, j: (i, j, 0)),
        out_shape=jax.ShapeDtypeStruct((b, n_blocks * TB, d), F32),
        compiler_params=_cparams("parallel", "parallel"),
        name="mlp_block",
    )(xs, modv, nw.reshape(1, d), w1, w2, fw.reshape(1, d))


def _dn_prep_kernel(p_ref, pp_ref, pn_ref, ab_ref, cw_ref, alog_ref, dtb_ref, bd_ref,
                    q_ref, k_ref, v_ref, gb_ref):
    x = p_ref[0]
    xprev, xnext = _neighbours(x, pp_ref, pn_ref)
    c = cw_ref[0:1, :] * xprev + cw_ref[1:2, :] * x + cw_ref[2:3, :] * xnext
    c = c * _sigmoid(c)
    dd = DN_DIM
    bd = bd_ref[...]
    q, k = c[:, :dd], c[:, dd:2 * dd]
    q_ref[0] = q * lax.rsqrt(_mm(q * q, bd, NN, 2, 1) + EPS) * (DN_HEAD_DIM ** -0.5)
    k_ref[0] = k * lax.rsqrt(_mm(k * k, bd, NN, 2, 1) + EPS)
    v_ref[0] = c[:, 2 * dd:]
    slab = ab_ref[0]
    lane = lax.broadcasted_iota(jnp.int32, (1, 128), 1)
    gdec = -jnp.exp(alog_ref[...]) * _softplus(slab + dtb_ref[...])
    gb_ref[0] = jnp.where(lane < 2 * DN_HEADS, gdec, _sigmoid(slab))


def dn_prep(p, conv_w, alog_row, dtb_row, bd128):
    b, t, _ = p.shape
    wq = 3 * DN_DIM
    prev, nxt = _halo_specs(wq)
    full = lambda a: pl.BlockSpec(a.shape, lambda i, j: (0,) * a.ndim)
    params = (conv_w, alog_row, dtb_row, bd128)
    tok = pl.BlockSpec((1, TB, DN_DIM), lambda i, j: (i, j, 0))
    s1 = jax.ShapeDtypeStruct((b, t, DN_DIM), F32)
    return pl.pallas_call(
        _dn_prep_kernel,
        grid=(b, t // TB),
        in_specs=[pl.BlockSpec((1, TB, wq), lambda i, j: (i, j, 0)), prev, nxt,
                  pl.BlockSpec((1, TB, 128), lambda i, j: (i, j, 4 * DN_DIM // 128))] + [full(a) for a in params],
        out_specs=[tok, tok, tok, pl.BlockSpec((1, TB, 128), lambda i, j: (i, j, 0))],
        out_shape=[s1, s1, s1, jax.ShapeDtypeStruct((b, t, 128), F32)],
        compiler_params=_cparams("parallel", "parallel"),
        name="dn_prep",
    )(p, p, p, p, *params)


def _dn_scan_kernel(q_ref, k_ref, v_ref, gb_ref, gbt_ref, y_ref, s_ref, *, reverse, direction):
    c, kdim = CH, DN_HEAD_DIM

    @pl.when(pl.program_id(1) == 0)
    def _():
        s_ref[...] = jnp.zeros_like(s_ref)

    ti, ji, strict, incl = _chunk_masks(reverse)
    gb = gb_ref[0]
    gcol = _mm(incl.astype(BF16), gb, NN, 1, 2)
    inclt = (jnp.logical_or(ji < ti, ti == ji) if reverse else jnp.logical_or(ji > ti, ti == ji)).astype(BF16)
    grow = _mm(gbt_ref[0, 0], inclt, NN, 2, 1)
    last = 0 if reverse else c - 1
    q, k, v = q_ref[0], k_ref[0], v_ref[0]
    for h in range(DN_HEADS):
        sl = slice(h * kdim, (h + 1) * kdim)
        col = direction * DN_HEADS + h
        g_t = gb[:, col:col + 1]
        beta = gb[:, 2 * DN_HEADS + col:2 * DN_HEADS + col + 1]
        gt = gcol[:, col:col + 1]
        gxt = gt - g_t
        gj = grow[col:col + 1, :]
        gc = gcol[last:last + 1, col:col + 1]
        d3 = jnp.where(incl, jnp.exp(jnp.where(incl, gt - gj, 0.0)), 0.0)
        d1 = jnp.where(strict, jnp.exp(jnp.where(strict, gxt - gj, 0.0)), 0.0)
        kh_, qh_, vh = k[:, sl], q[:, sl], v[:, sl]
        kp = kh_ * beta
        bb = kp * jnp.exp(g_t)
        a = _mm(jnp.concatenate([kh_, qh_], axis=0), jnp.concatenate([kp, bb], axis=0), NT, 2, 2)
        egc = jnp.exp(gc - gt)
        y_ref[0, :, sl] = _chunk_update(s_ref, h, a[:c, :c] * d1, a[:c, c:] * d1, a[c:, :c] * d3, a[c:, c:] * d3,
                                        kh_ * jnp.exp(gxt), qh_ * jnp.exp(gt), kp * egc, bb * egc, vh,
                                        jnp.exp(gc), ti, ji, kdim)


def dn_scan(q, k, v, gb, gbt, direction):
    b, t, w = q.shape
    reverse = direction == 1
    cidx = _chunk_index(reverse)
    tok = pl.BlockSpec((1, CH, w), lambda i, j: (i, cidx(j), 0))
    return pl.pallas_call(
        functools.partial(_dn_scan_kernel, reverse=reverse, direction=direction),
        grid=(b, NC),
        in_specs=[tok, tok, tok,
                  pl.BlockSpec((1, CH, 128), lambda i, j: (i, cidx(j), 0)),
                  pl.BlockSpec((1, 1, 4 * DN_HEADS, CH), lambda i, j: (i, cidx(j), 0, 0))],
        out_specs=tok,
        out_shape=jax.ShapeDtypeStruct((b, t, w), F32),
        scratch_shapes=[pltpu.VMEM((DN_HEADS, DN_HEAD_DIM, DN_HEAD_DIM), F32)],
        compiler_params=_cparams("parallel", "arbitrary"),
        name="dn_scan_rev" if reverse else "dn_scan_fwd",
    )(q, k, v, gb, gbt)


def _dn_out_kernel(o0_ref, o1_ref, z_ref, x_ref, mod_ref, nw_ref, bd_ref, wo_ref, o_ref):
    o = o0_ref[0] + o1_ref[0]
    ms = _mm(o * o, bd_ref[...], NN, 2, 1) * (1.0 / DN_HEAD_DIM)
    on = o * lax.rsqrt(ms + EPS) * nw_ref[...]
    z = z_ref[0]
    gated = on * (z * _sigmoid(z))
    o_ref[0] = x_ref[0] + mod_ref[0, 0][2:3] * _dot(gated.astype(BF16), wo_ref[...])


def dn_out(o0, o1, p, xs, modv, nw_tiled, bd128, w_out, n_blocks):
    b, _, d = xs.shape
    tok = pl.BlockSpec((1, TB, DN_DIM), lambda i, j: (i, j, 0))
    full = lambda a: pl.BlockSpec(a.shape, lambda i, j: (0,) * a.ndim)
    params = (nw_tiled, bd128, w_out)
    return pl.pallas_call(
        _dn_out_kernel,
        grid=(b, n_blocks),
        in_specs=[tok, tok, pl.BlockSpec((1, TB, DN_DIM), lambda i, j: (i, j, 3)),
                  pl.BlockSpec((1, TB, d), lambda i, j: (i, j, 0)),
                  pl.BlockSpec((1, 1, ADA_CHUNKS, d), lambda i, j: (i, j // NB_LAT, 0, 0))]
        + [full(a) for a in params],
        out_specs=pl.BlockSpec((1, TB, d), lambda i, j: (i, j, 0)),
        out_shape=jax.ShapeDtypeStruct((b, n_blocks * TB, d), F32),
        compiler_params=_cparams("parallel", "parallel"),
        name="dn_out",
    )(o0, o1, p, xs, modv, *params)


def _block_diag_ones(width, head):
    i = jnp.arange(width) // head
    return (i[:, None] == i[None, :]).astype(BF16)


def kernel(x, c, ctx, c_ctx, ada_w, ada_b, norm_mix, norm_mlp, mlp_w1, mlp_w2, final_norm, ab_w_in, ab_w_out, rw_mu, rw_w0, rw_w_up, rw_a0, rw_a_up, rw_g_up, rw_k_k, rw_k_a, rw_r_k, rw_ln_w, rw_ln_b, hy_conv_w, hy_conv_b, hy_f_w1, hy_f_b1, hy_f_w2, hy_f_b2, hy_f_w3, hy_f_b3, hy_f_w4, hy_freq, hy_skip, dn_w_in, dn_conv_w, dn_A_log, dn_dt_bias, dn_norm, dn_w_out):
    bsz = x.shape[0]
    d = D_MODEL
    w = RW_WIDTH
    row = lambda a: a.reshape(1, -1)
    xs = jnp.concatenate([x, ctx], axis=1)

    cs = jnp.concatenate([c, c_ctx[None, :], jnp.zeros((16 - bsz - 1, d), F32)], axis=0)
    mod = ada_modulation(cs, ada_w, ada_b)

    def mod_vectors(layer):
        lat = mod[layer, :bsz].reshape(bsz, 1, ADA_CHUNKS, d)
        cx = jnp.broadcast_to(mod[layer, bsz].reshape(1, 1, ADA_CHUNKS, d), (bsz, 1, ADA_CHUNKS, d))
        return jnp.concatenate([lat, cx], axis=1)

    bd64 = _block_diag_ones(w, RW_HEAD_DIM)
    bd128 = _block_diag_ones(DN_DIM, DN_HEAD_DIM)

    modv = mod_vectors(0)
    p = nm_matmul(xs, norm_mix[0], modv, ab_w_in[0].astype(BF16))
    lora = jnp.zeros((128, 4 * w), F32)
    lora = lora.at[:64, :w].set(rw_w_up[0, 0]).at[:64, w:2 * w].set(rw_w_up[0, 1])
    lora = lora.at[64:, 2 * w:3 * w].set(rw_a_up[0, 0]).at[64:, 3 * w:].set(rw_a_up[0, 1])
    r, v, kk, g, bv, lw, kd, bb, s, x0 = ab_prep(
        p, row(rw_mu[0]), lora.astype(BF16), row(rw_w0[0]), row(rw_a0[0]), rw_g_up[0].astype(BF16),
        row(rw_k_k[0]), row(rw_k_a[0]), row(rw_r_k[0]), bd64, hy_conv_w[0], row(hy_conv_b[0]))
    y0 = rwkv_scan(r, v, kk, lw, kd, bb, 0)
    y1 = rwkv_scan(r, v, kk, lw, kd, bb, 1)
    filt = (hy_f_w1[0], hy_f_b1[0], hy_f_w2[0], hy_f_b2[0], hy_f_w3[0], hy_f_b3[0], hy_f_w4[0], hy_freq[0])
    convs = []
    for length, block in ((SEQ, 0), (CTX_LEN, SEQ // CTX_LEN)):
        wf = dft_matrix(length)
        hspec = filter_spectrum(wf, hyena_filter_halves(length, *filt))
        convs.append(hyena_long_conv(s, wf, wf.T, hspec, length, block))
    cv = jnp.concatenate(convs, axis=1)
    xs = ab_out(y0, y1, bv, g, x0, cv, s, xs, modv, row(rw_ln_w[0]), row(rw_ln_b[0]), row(hy_skip[0]), bd64,
                ab_w_out[0].astype(BF16))
    xs = mlp_block(xs, modv, norm_mlp[0], mlp_w1[0].astype(BF16), mlp_w2[0].astype(BF16), final_norm, NB, False)

    modv = mod_vectors(1)
    w_in = jnp.pad(dn_w_in[0], ((0, 0), (0, DN_COLS_PAD - DN_COLS))).astype(BF16)
    p = nm_matmul(xs, norm_mix[1], modv, w_in)
    pad_row = lambda a: jnp.pad(a.reshape(1, -1), ((0, 0), (0, 128 - 2 * DN_HEADS)))
    q, k, vv, gb = dn_prep(p, dn_conv_w[0], pad_row(dn_A_log[0]), pad_row(dn_dt_bias[0]), bd128)
    gbt = jnp.swapaxes(gb[:, :, :4 * DN_HEADS].reshape(bsz, NC, CH, 4 * DN_HEADS), 2, 3)
    o0 = dn_scan(q, k, vv, gb, gbt, 0)
    o1 = dn_scan(q, k, vv, gb, gbt, 1)
    xl = dn_out(o0, o1, p, xs, modv, row(jnp.tile(dn_norm[0], DN_HEADS)), bd128, dn_w_out[0].astype(BF16), NB_LAT)
    return mlp_block(xl, modv, norm_mlp[1], mlp_w1[1].astype(BF16), mlp_w2[1].astype(BF16), final_norm, NB_LAT, True)
```

```python
import functools
import math

import jax
import jax.numpy as jnp
from jax import lax
from jax.experimental import pallas as pl
from jax.experimental.pallas import tpu as pltpu

F32, BF16 = jnp.float32, jnp.bfloat16

D_MODEL = 1024
SEQ = 2048
CTX_LEN = 256
ADA_CHUNKS = 6
EPS = 1e-6
RW_WIDTH = 512
RW_HEAD_DIM = 64
RW_HEADS = 8
RW_COLS = 1792
RW_GN_EPS = 64e-5
HY_WIDTH = 512
HY_COLS = 1536
HY_BANDS = 16
HY_EMB = 33
HY_ORDER = 64
HY_FAST_DECAY = 0.3
HY_SLOW_DECAY = 1.5
HY_TARGET = 1e-2
AB_COLS = RW_COLS + HY_COLS
DN_HEADS = 8
DN_HEAD_DIM = 128
DN_DIM = 1024
DN_COLS = 4 * DN_DIM + 4 * DN_HEADS
DN_COLS_PAD = 4 * DN_DIM + 128

TB = 256
CH = 64
T_ALL = SEQ + CTX_LEN
NB = T_ALL // TB
NB_LAT = SEQ // TB
NC = T_ALL // CH
NC_LAT = SEQ // CH
FB = 256
VMEM_LIMIT = 56 * 1024 * 1024

NN = ((1,), (0,))
NT = ((1,), (1,))
TN = ((0,), (0,))


def _dot(a, b, dims=NN):
    return lax.dot_general(a, b, (dims, ((), ())), preferred_element_type=F32)


def _split(x):
    hi = x.astype(BF16)
    lo = (x - hi.astype(F32)).astype(BF16)
    return hi, lo


def _mm(a, b, dims=NN, pa=1, pb=1):
    if a.dtype == BF16:
        a_hi, a_lo, pa = a, None, 1
    elif pa == 2:
        a_hi, a_lo = _split(a)
    else:
        a_hi, a_lo = a.astype(BF16), None
    if b.dtype == BF16:
        b_hi, b_lo, pb = b, None, 1
    elif pb == 2:
        b_hi, b_lo = _split(b)
    else:
        b_hi, b_lo = b.astype(BF16), None
    out = _dot(a_hi, b_hi, dims)
    if pa == 2:
        out = out + _dot(a_lo, b_hi, dims)
    if pb == 2:
        out = out + _dot(a_hi, b_lo, dims)
    return out


def _sigmoid(x):
    return 1.0 / (1.0 + jnp.exp(-x))


def _softplus(x):
    return jnp.maximum(x, 0.0) + jnp.log(1.0 + jnp.exp(-jnp.abs(x)))


def _rms_mod(x, nw, shift, scale):
    y = x * lax.rsqrt(jnp.mean(x * x, axis=-1, keepdims=True) + EPS)
    return (y * nw) * (1.0 + scale) + shift


def _cparams(*sem):
    return pltpu.CompilerParams(dimension_semantics=sem, vmem_limit_bytes=VMEM_LIMIT)


ADA_TN = 1536


def _ada_kernel(c_ref, w_ref, b_ref, o_ref):
    c = c_ref[...]
    a = c * _sigmoid(c)
    o_ref[0] = _mm(a, w_ref[0], NN, 2, 2) + b_ref[0]


def ada_modulation(cs, ada_w, ada_b):
    depth, d, n = ada_w.shape
    rows = cs.shape[0]
    return pl.pallas_call(
        _ada_kernel,
        grid=(depth, n // ADA_TN),
        in_specs=[pl.BlockSpec((rows, d), lambda l, j: (0, 0)),
                  pl.BlockSpec((1, d, ADA_TN), lambda l, j: (l, 0, j)),
                  pl.BlockSpec((1, 1, ADA_TN), lambda l, j: (l, 0, j))],
        out_specs=pl.BlockSpec((1, rows, ADA_TN), lambda l, j: (l, 0, j)),
        out_shape=jax.ShapeDtypeStruct((depth, rows, n), F32),
        compiler_params=_cparams("parallel", "parallel"),
        name="ada_modulation",
    )(cs, ada_w, ada_b.reshape(depth, 1, n))


def _nm_matmul_kernel(x_ref, nw_ref, mod_ref, w_ref, o_ref):
    m = mod_ref[0, 0]
    h = _rms_mod(x_ref[0], nw_ref[...], m[0:1], m[1:2])
    o_ref[0] = _dot(h.astype(BF16), w_ref[...])


def nm_matmul(xs, nw, modv, w):
    b, t, d = xs.shape
    n = w.shape[1]
    return pl.pallas_call(
        _nm_matmul_kernel,
        grid=(b, t // TB),
        in_specs=[pl.BlockSpec((1, TB, d), lambda i, j: (i, j, 0)),
                  pl.BlockSpec((1, d), lambda i, j: (0, 0)),
                  pl.BlockSpec((1, 1, ADA_CHUNKS, d), lambda i, j: (i, j // NB_LAT, 0, 0)),
                  pl.BlockSpec((d, n), lambda i, j: (0, 0))],
        out_specs=pl.BlockSpec((1, TB, n), lambda i, j: (i, j, 0)),
        out_shape=jax.ShapeDtypeStruct((b, t, n), F32),
        compiler_params=_cparams("parallel", "parallel"),
        name="nm_matmul",
    )(xs, nw.reshape(1, d), modv, w)


def _halo_specs(width, col_block=0):
    prev = pl.BlockSpec((1, 8, width), lambda i, j: (i, jnp.maximum(j * (TB // 8) - 1, 0), col_block))
    nxt = pl.BlockSpec((1, 8, width), lambda i, j: (i, jnp.minimum((j + 1) * (TB // 8), T_ALL // 8 - 1), col_block))
    return prev, nxt


def _neighbours(x, prev_ref, next_ref):
    j = pl.program_id(1)
    pv = jnp.where(jnp.logical_and(j != 0, j != NB - 1), 1.0, 0.0).astype(F32)
    nv = jnp.where(j < NB - 2, 1.0, 0.0).astype(F32)
    prow = prev_ref[0, 7:8, :] * pv
    nrow = next_ref[0, 0:1, :] * nv
    rows = lax.broadcasted_iota(jnp.int32, (TB, 1), 0)
    xprev = jnp.where(rows == 0, prow, pltpu.roll(x, 1, 0))
    xnext = jnp.where(rows == TB - 1, nrow, pltpu.roll(x, TB - 1, 0))
    return xprev, xnext


def _ab_prep_kernel(p_ref, pp_ref, pn_ref, mu_ref, lora_ref, w0_ref, a0_ref, gup_ref, kk_ref, ka_ref, rk_ref,
                    bd_ref, cw_ref, cb_ref,
                    r_ref, v_ref, kkn_ref, g_ref, bv_ref, lw_ref, kd_ref, b_ref, s_ref, x0_ref):
    x = p_ref[0]
    xprev, xnext = _neighbours(x, pp_ref, pn_ref)
    w = RW_WIDTH
    xr = x[:, :RW_COLS]
    ps = xr + mu_ref[...] * (0.5 * (xprev[:, :RW_COLS] + xnext[:, :RW_COLS]) - xr)
    r, k, v = ps[:, :w], ps[:, w:2 * w], ps[:, 2 * w:3 * w]
    slab = ps[:, 3 * w:3 * w + 128]
    gl = ps[:, 3 * w + 128:3 * w + 256]
    lane = lax.broadcasted_iota(jnp.int32, (1, 128), 1)
    z = jnp.where(lane < 64, jnp.tanh(slab), slab)
    lora = _dot(z.astype(BF16), lora_ref[...])
    bd = bd_ref[...]
    kkraw = k * kk_ref[...]
    kk = kkraw * lax.rsqrt(_mm(kkraw * kkraw, bd, NN, 2, 1) + EPS)
    ksum = jnp.zeros_like(k)
    for d in range(2):
        w_log = -_softplus(-(w0_ref[:, d * w:(d + 1) * w] + lora[:, d * w:(d + 1) * w])) - 0.5
        lw_ref[d, 0] = -jnp.exp(w_log)
        a = _sigmoid(a0_ref[:, d * w:(d + 1) * w] + lora[:, (2 + d) * w:(3 + d) * w])
        kd = k * (1.0 + (a - 1.0) * ka_ref[...])
        kd_ref[d, 0] = kd
        b_ref[d, 0] = a * kk
        ksum = ksum + kd
    r_ref[0] = r
    v_ref[0] = v
    kkn_ref[0] = kk
    g_ref[0] = _dot(_sigmoid(gl).astype(BF16), gup_ref[...])
    bv_ref[0] = _mm(r * ksum * rk_ref[...], bd, NN, 2, 1) * v
    u = x[:, RW_COLS:]
    c = (cw_ref[0:1, :] * xprev[:, RW_COLS:] + cw_ref[1:2, :] * u + cw_ref[2:3, :] * xnext[:, RW_COLS:]
         + cb_ref[...])
    hw = HY_WIDTH
    x0_ref[0] = c[:, :hw]
    s_ref[0] = c[:, hw:2 * hw] * c[:, 2 * hw:]


def ab_prep(p, mu, lora, w0, a0, gup, k_k, k_a, r_k, bd64, conv_w, conv_b):
    b, t, n = p.shape
    w = RW_WIDTH
    prev, nxt = _halo_specs(n)
    full = lambda a: pl.BlockSpec(a.shape, lambda i, j: (0,) * a.ndim)
    params = (mu, lora, w0, a0, gup, k_k, k_a, r_k, bd64, conv_w, conv_b)
    tok = pl.BlockSpec((1, TB, w), lambda i, j: (i, j, 0))
    tok2 = pl.BlockSpec((2, 1, TB, w), lambda i, j: (0, i, j, 0))
    s1 = jax.ShapeDtypeStruct((b, t, w), F32)
    s2 = jax.ShapeDtypeStruct((2, b, t, w), F32)
    return pl.pallas_call(
        _ab_prep_kernel,
        grid=(b, t // TB),
        in_specs=[pl.BlockSpec((1, TB, n), lambda i, j: (i, j, 0)), prev, nxt] + [full(a) for a in params],
        out_specs=[tok, tok, tok, tok, tok, tok2, tok2, tok2, tok, tok],
        out_shape=[s1, s1, s1, s1, s1, s2, s2, s2, s1, s1],
        compiler_params=_cparams("parallel", "parallel"),
        name="ab_prep",
    )(p, p, p, *params)


def _chunk_masks(reverse):
    c = CH
    ti = lax.broadcasted_iota(jnp.int32, (c, c), 0)
    ji = lax.broadcasted_iota(jnp.int32, (c, c), 1)
    strict = (ji > ti) if reverse else (ji < ti)
    incl = jnp.logical_or(strict, ti == ji)
    return ti, ji, strict, incl


def _mm_many(a_list, b_list, dims=NN, pa=2, pb=2):
    ops = []
    for a, b in zip(a_list, b_list):
        a_hi, a_lo = (a, None) if a.dtype == BF16 else (_split(a) if pa == 2 else (a.astype(BF16), None))
        b_hi, b_lo = (b, None) if b.dtype == BF16 else (_split(b) if pb == 2 else (b.astype(BF16), None))
        ops.append((a_hi, a_lo, b_hi, b_lo))
    outs = [_dot(a_hi, b_hi, dims) for a_hi, _, b_hi, _ in ops]
    outs = [o if a_lo is None else o + _dot(a_lo, b_hi, dims) for o, (_, a_lo, b_hi, _) in zip(outs, ops)]
    outs = [o if b_lo is None else o + _dot(a_hi, b_lo, dims) for o, (a_hi, _, _, b_lo) in zip(outs, ops)]
    return outs


def _tri_inverse_many(n_list, ti, ji):
    eye = (ti == ji).astype(F32)
    pair = (ti >> 1) == (ji >> 1)
    dinvs = [eye - jnp.where(pair, n, 0.0) for n in n_list]
    s = 1
    while (2 << s) <= CH:
        off = jnp.logical_and((ti >> (s + 1)) == (ji >> (s + 1)), (ti >> s) != (ji >> s))
        coffs = [jnp.where(off, n, 0.0) for n in n_list]
        corr = _mm_many(_mm_many(dinvs, coffs), dinvs)
        dinvs = [d - t for d, t in zip(dinvs, corr)]
        s += 1
    return dinvs


def _chunk_update_many(s_ref, a_kk, a_bk, a_rk, a_rb, kkt, rt, kh, bh, vh, dec_c, ti, ji, kdim):
    nh = len(a_kk)
    tms = _tri_inverse_many(a_bk, ti, ji)
    akkv = _mm_many(a_kk, vh)
    tw = _mm_many(tms, [jnp.concatenate([kkt[h], akkv[h]], axis=1) for h in range(nh)])
    y0 = _mm_many(a_rk, vh)
    states = [s_ref[h] for h in range(nh)]
    p = _mm_many([jnp.concatenate([tw[h][:, :kdim], rt[h]], axis=0) for h in range(nh)], states, NT)
    u = [p[h][:CH] + tw[h][:, kdim:] for h in range(nh)]
    au = _mm_many(a_rb, u)
    ds = _mm_many([jnp.concatenate([vh[h], -u[h]], axis=0) for h in range(nh)],
                  [jnp.concatenate([kh[h], bh[h]], axis=0) for h in range(nh)], TN)
    for h in range(nh):
        s_ref[h] = states[h] * dec_c[h] + ds[h]
    return [p[h][CH:] + y0[h] - au[h] for h in range(nh)]


def _rwkv_scan_kernel(r_ref, v_ref, kk_ref, lw_ref, kd_ref, b_ref, y_ref, s_ref, *, reverse):
    c, kdim = CH, RW_HEAD_DIM

    @pl.when(pl.program_id(1) == 0)
    def _():
        s_ref[...] = jnp.zeros_like(s_ref)

    ti, ji, strict, incl = _chunk_masks(reverse)
    lw = lw_ref[0, 0]
    g = _mm(incl.astype(BF16), lw, NN, 1, 2)
    gx = g - lw
    last = 0 if reverse else c - 1
    gc = g[last:last + 1, :]
    r, v, kk, kd, bb = r_ref[0], v_ref[0], kk_ref[0], kd_ref[0, 0], b_ref[0, 0]
    eng = jnp.exp(-g)
    egc = jnp.exp(gc - g)
    rt_all = r * jnp.exp(g)
    kkt_all = kk * jnp.exp(gx)
    kn_all, bn_all = kd * eng, bb * eng
    kh_all, bh_all = kd * egc, bb * egc
    dec_all = jnp.exp(gc)
    ri = lax.broadcasted_iota(jnp.int32, (2 * c, 2 * c), 0)
    ci = lax.broadcasted_iota(jnp.int32, (2 * c, 2 * c), 1)
    rt_, ct_ = ri & (c - 1), ci & (c - 1)
    before = (ct_ > rt_) if reverse else (ct_ < rt_)
    amask = jnp.logical_or(before, jnp.logical_and(rt_ == ct_, ri >= c))
    sls = [slice(h * kdim, (h + 1) * kdim) for h in range(RW_HEADS)]
    heads = lambda x: [x[:, sl] for sl in sls]
    kkt, rt = heads(kkt_all), heads(rt_all)
    a = _mm_many([jnp.concatenate([kkt[h], rt[h]], axis=0) for h in range(RW_HEADS)],
                 [jnp.concatenate([kn_all[:, sl], bn_all[:, sl]], axis=0) for sl in sls], NT)
    a = [jnp.where(amask, x, 0.0) for x in a]
    ys = _chunk_update_many(s_ref, [x[:c, :c] for x in a], [x[:c, c:] for x in a], [x[c:, :c] for x in a],
                            [x[c:, c:] for x in a], kkt, rt, heads(kh_all), heads(bh_all), heads(v),
                            heads(dec_all), ti, ji, kdim)
    for sl, y in zip(sls, ys):
        y_ref[0, :, sl] = y


def _chunk_index(reverse):
    if reverse:
        return lambda i: NC - 1 - i
    return lambda i: (i + NC_LAT) % NC


def rwkv_scan(r, v, kk, lw, kd, bb, direction):
    b, t, w = r.shape
    reverse = direction == 1
    cidx = _chunk_index(reverse)
    tok = pl.BlockSpec((1, CH, w), lambda i, j: (i, cidx(j), 0))
    tokd = pl.BlockSpec((1, 1, CH, w), lambda i, j: (direction, i, cidx(j), 0))
    return pl.pallas_call(
        functools.partial(_rwkv_scan_kernel, reverse=reverse),
        grid=(b, NC),
        in_specs=[tok, tok, tok, tokd, tokd, tokd],
        out_specs=tok,
        out_shape=jax.ShapeDtypeStruct((b, t, w), F32),
        scratch_shapes=[pltpu.VMEM((RW_HEADS, RW_HEAD_DIM, RW_HEAD_DIM), F32)],
        compiler_params=_cparams("parallel", "arbitrary"),
        name="rwkv_scan_rev" if reverse else "rwkv_scan_fwd",
    )(r, v, kk, lw, kd, bb)


def _hy_filter_kernel(z_ref, t_ref, w1_ref, b1_ref, w2_ref, b2_ref, w3_ref, b3_ref, w4_ref, fr_ref, dl_ref, o_ref):
    fr = fr_ref[...]
    h = jnp.sin(fr * (_mm(z_ref[...], w1_ref[...], NN, 2, 2) + b1_ref[...]))
    h = jnp.sin(fr * (_mm(h, w2_ref[...], NN, 2, 2) + b2_ref[...]))
    h = jnp.sin(fr * (_mm(h, w3_ref[...], NN, 2, 2) + b3_ref[...]))
    h = _mm(h, w4_ref[...], NN, 2, 2)
    h = h * jnp.exp(-t_ref[...] * dl_ref[...])
    hw = HY_WIDTH
    rows = lax.broadcasted_iota(jnp.int32, (h.shape[0], 1), 0)
    hf = h[:, :hw]
    hb = jnp.where(rows == 0, 0.0, h[:, hw:])
    norm = jnp.sum(jnp.abs(hf), axis=0, keepdims=True) + jnp.sum(jnp.abs(hb), axis=0, keepdims=True)
    o_ref[:, :hw] = (hf + hb) / norm
    o_ref[:, hw:] = (hf - hb) / norm


def hyena_filter_halves(length, w1, b1, w2, b2, w3, b3, w4, freq):
    t = jnp.linspace(0.0, 1.0, length, dtype=F32)[:, None]
    wv = 2 * math.pi * jnp.arange(length, dtype=F32)[:, None] / length
    f = jnp.linspace(1e-4, HY_BANDS - 1, HY_BANDS, dtype=F32)[None, :]
    z = jnp.concatenate([t, jnp.cos(f * wv), -jnp.sin(f * wv)], axis=-1)
    z = jnp.pad(z, ((0, 0), (0, HY_ORDER - HY_EMB)))
    w1p = jnp.pad(w1, ((0, HY_ORDER - HY_EMB), (0, 0)))
    deltas = jnp.abs(jnp.linspace(math.log(HY_TARGET) / HY_SLOW_DECAY, math.log(HY_TARGET) / HY_FAST_DECAY,
                                  HY_WIDTH, dtype=F32))
    dl = jnp.tile(deltas, 2)[None, :]
    row = lambda a: a.reshape(1, -1)
    return pl.pallas_call(
        _hy_filter_kernel,
        out_shape=jax.ShapeDtypeStruct((length, 2 * HY_WIDTH), F32),
        compiler_params=pltpu.CompilerParams(vmem_limit_bytes=VMEM_LIMIT),
        name="hyena_filter",
    )(z, t, w1p, row(b1), w2, row(b2), w3, row(b3), w4, row(freq), dl)


def dft_matrix(length):
    n = 2 * length
    f = jnp.arange(length, dtype=jnp.int32)[:, None]
    t = jnp.arange(length, dtype=jnp.int32)[None, :]
    ang = ((f * t) % n).astype(F32) * (2 * math.pi / n)
    cosm = jnp.cos(ang)
    msin = -jnp.sin(ang)
    nyq = jnp.where(t % 2 == 0, 1.0, -1.0).astype(F32)
    msin = jnp.where(f == 0, nyq, msin)
    nfb = length // FB
    wf = jnp.concatenate([cosm.reshape(nfb, FB, length), msin.reshape(nfb, FB, length)], axis=1)
    return wf.reshape(n, length).astype(BF16)


def _spectrum_kernel(w_ref, h_ref, o_ref):
    o_ref[...] = _mm(w_ref[...], h_ref[...], NN, 1, 2)


def filter_spectrum(wf, hpm):
    n, length = wf.shape
    cols = hpm.shape[1]
    return pl.pallas_call(
        _spectrum_kernel,
        grid=(n // (2 * FB),),
        in_specs=[pl.BlockSpec((2 * FB, length), lambda i: (i, 0)),
                  pl.BlockSpec((length, cols), lambda i: (0, 0))],
        out_specs=pl.BlockSpec((2 * FB, cols), lambda i: (i, 0)),
        out_shape=jax.ShapeDtypeStruct((n, cols), F32),
        compiler_params=_cparams("parallel"),
        name="filter_spectrum",
    )(wf, hpm)


def _hy_conv_kernel(s_ref, wf_ref, wft_ref, h_ref, o_ref, sb_ref, *, n_fft):
    fb = pl.program_id(1)

    @pl.when(fb == 0)
    def _():
        sb_ref[...] = s_ref[0].astype(BF16)
        o_ref[...] = jnp.zeros_like(o_ref)

    hw = HY_WIDTH
    x = _dot(wf_ref[...], sb_ref[...])
    xr, xi = x[:FB], x[FB:]
    hraw = h_ref[...]
    hr, hi, hn = hraw[:FB, :hw], hraw[FB:, hw:], hraw[FB:, :hw]
    fidx = lax.broadcasted_iota(jnp.int32, (FB, 1), 0) + fb * FB
    m = (fidx != 0).astype(F32)
    him = hi * m
    hc = hr * m + hn * (1.0 - m)
    cf = jnp.where(fidx != 0, 2.0 / n_fft, 1.0 / n_fft).astype(F32)
    yr = (xr * hr - xi * him) * cf
    yi = (xr * him + xi * hc) * cf
    y = jnp.concatenate([yr, yi], axis=0).astype(BF16)
    o_ref[0] += _dot(wft_ref[...], y)


def hyena_long_conv(s, wf, wft, hspec, length, block):
    b = s.shape[0]
    hw = HY_WIDTH
    return pl.pallas_call(
        functools.partial(_hy_conv_kernel, n_fft=2 * length),
        grid=(b, length // FB),
        in_specs=[pl.BlockSpec((1, length, hw), lambda i, j: (i, block, 0)),
                  pl.BlockSpec((2 * FB, length), lambda i, j: (j, 0)),
                  pl.BlockSpec((length, 2 * FB), lambda i, j: (0, j)),
                  pl.BlockSpec((2 * FB, 2 * hw), lambda i, j: (j, 0))],
        out_specs=pl.BlockSpec((1, length, hw), lambda i, j: (i, 0, 0)),
        out_shape=jax.ShapeDtypeStruct((b, length, hw), F32),
        scratch_shapes=[pltpu.VMEM((length, hw), BF16)],
        compiler_params=_cparams("parallel", "arbitrary"),
        name="hyena_long_conv",
    )(s, wf, wft, hspec)


def _ab_out_kernel(y0_ref, y1_ref, bv_ref, g_ref, x0_ref, cv_ref, s_ref, x_ref, mod_ref, lnw_ref, lnb_ref, skip_ref,
                   bd_ref, wo_ref, o_ref):
    bd = bd_ref[...]
    inv = 1.0 / RW_HEAD_DIM
    y = y0_ref[0] + y1_ref[0]
    mean = _mm(y, bd, NN, 2, 1) * inv
    yc = y - mean
    var = _mm(yc * yc, bd, NN, 2, 1) * inv
    yn = yc * lax.rsqrt(var + RW_GN_EPS)
    a = (yn * lnw_ref[...] + lnb_ref[...] + bv_ref[0]) * g_ref[0]
    s = s_ref[0]
    bh = x0_ref[0] * (cv_ref[0] + s * skip_ref[...])
    w = RW_WIDTH
    out = _dot(a.astype(BF16), wo_ref[:w, :]) + _dot(bh.astype(BF16), wo_ref[w:, :])
    o_ref[0] = x_ref[0] + mod_ref[0, 0][2:3] * out


def ab_out(y0, y1, bv, g, x0, cv, s, xs, modv, ln_w, ln_b, skip, bd64, w_out):
    b, t, d = xs.shape
    w = RW_WIDTH
    tok = pl.BlockSpec((1, TB, w), lambda i, j: (i, j, 0))
    full = lambda a: pl.BlockSpec(a.shape, lambda i, j: (0,) * a.ndim)
    params = (ln_w, ln_b, skip, bd64, w_out)
    return pl.pallas_call(
        _ab_out_kernel,
        grid=(b, t // TB),
        in_specs=[tok] * 7 + [pl.BlockSpec((1, TB, d), lambda i, j: (i, j, 0)),
                              pl.BlockSpec((1, 1, ADA_CHUNKS, d), lambda i, j: (i, j // NB_LAT, 0, 0))]
        + [full(a) for a in params],
        out_specs=pl.BlockSpec((1, TB, d), lambda i, j: (i, j, 0)),
        out_shape=jax.ShapeDtypeStruct((b, t, d), F32),
        compiler_params=_cparams("parallel", "parallel"),
        name="ab_out",
    )(y0, y1, bv, g, x0, cv, s, xs, modv, *params)


MLP_HC = 1024


def _mlp_kernel(x_ref, mod_ref, nw_ref, w1_ref, w2_ref, fw_ref, o_ref, *, final):
    x = x_ref[0]
    m = mod_ref[0, 0]
    h = _rms_mod(x, nw_ref[...], m[3:4], m[4:5]).astype(BF16)
    acc = jnp.zeros_like(x)
    for c in range(w1_ref.shape[1] // MLP_HC):
        hid = _dot(h, w1_ref[:, c * MLP_HC:(c + 1) * MLP_HC])
        hid = jnp.square(jnp.maximum(hid, 0.0))
        acc = acc + _dot(hid.astype(BF16), w2_ref[c * MLP_HC:(c + 1) * MLP_HC, :])
    o = x + m[5:6] * acc
    if final:
        o = o * lax.rsqrt(jnp.mean(o * o, axis=-1, keepdims=True) + EPS) * fw_ref[...]
    o_ref[0] = o


def mlp_block(xs, modv, nw, w1, w2, fw, n_blocks, final):
    b, t, d = xs.shape
    hdim = w1.shape[1]
    return pl.pallas_call(
        functools.partial(_mlp_kernel, final=final),
        grid=(b, n_blocks),
        in_specs=[pl.BlockSpec((1, TB, d), lambda i, j: (i, j, 0)),
                  pl.BlockSpec((1, 1, ADA_CHUNKS, d), lambda i, j: (i, j // NB_LAT, 0, 0)),
                  pl.BlockSpec((1, d), lambda i, j: (0, 0)),
                  pl.BlockSpec((d, hdim), lambda i, j: (0, 0)),
                  pl.BlockSpec((hdim, d), lambda i, j: (0, 0)),
                  pl.BlockSpec((1, d), lambda i, j: (0, 0))],
        out_specs=pl.BlockSpec((1, TB, d), lambda i, j: (i, j, 0)),
        out_shape=jax.ShapeDtypeStruct((b, n_blocks * TB, d), F32),
        compiler_params=_cparams("parallel", "parallel"),
        name="mlp_block",
    )(xs, modv, nw.reshape(1, d), w1, w2, fw.reshape(1, d))


def _dn_prep_kernel(p_ref, pp_ref, pn_ref, ab_ref, cw_ref, alog_ref, dtb_ref, bd_ref,
                    q_ref, k_ref, v_ref, gb_ref):
    x = p_ref[0]
    xprev, xnext = _neighbours(x, pp_ref, pn_ref)
    c = cw_ref[0:1, :] * xprev + cw_ref[1:2, :] * x + cw_ref[2:3, :] * xnext
    c = c * _sigmoid(c)
    dd = DN_DIM
    bd = bd_ref[...]
    q, k = c[:, :dd], c[:, dd:2 * dd]
    q_ref[0] = q * lax.rsqrt(_mm(q * q, bd, NN, 2, 1) + EPS) * (DN_HEAD_DIM ** -0.5)
    k_ref[0] = k * lax.rsqrt(_mm(k * k, bd, NN, 2, 1) + EPS)
    v_ref[0] = c[:, 2 * dd:]
    slab = ab_ref[0]
    lane = lax.broadcasted_iota(jnp.int32, (1, 128), 1)
    gdec = -jnp.exp(alog_ref[...]) * _softplus(slab + dtb_ref[...])
    gb_ref[0] = jnp.where(lane < 2 * DN_HEADS, gdec, _sigmoid(slab))


def dn_prep(p, conv_w, alog_row, dtb_row, bd128):
    b, t, _ = p.shape
    wq = 3 * DN_DIM
    prev, nxt = _halo_specs(wq)
    full = lambda a: pl.BlockSpec(a.shape, lambda i, j: (0,) * a.ndim)
    params = (conv_w, alog_row, dtb_row, bd128)
    tok = pl.BlockSpec((1, TB, DN_DIM), lambda i, j: (i, j, 0))
    s1 = jax.ShapeDtypeStruct((b, t, DN_DIM), F32)
    return pl.pallas_call(
        _dn_prep_kernel,
        grid=(b, t // TB),
        in_specs=[pl.BlockSpec((1, TB, wq), lambda i, j: (i, j, 0)), prev, nxt,
                  pl.BlockSpec((1, TB, 128), lambda i, j: (i, j, 4 * DN_DIM // 128))] + [full(a) for a in params],
        out_specs=[tok, tok, tok, pl.BlockSpec((1, TB, 128), lambda i, j: (i, j, 0))],
        out_shape=[s1, s1, s1, jax.ShapeDtypeStruct((b, t, 128), F32)],
        compiler_params=_cparams("parallel", "parallel"),
        name="dn_prep",
    )(p, p, p, p, *params)


def _dn_scan_kernel(q_ref, k_ref, v_ref, gb_ref, gbt_ref, y_ref, s_ref, *, reverse, direction):
    c, kdim = CH, DN_HEAD_DIM

    @pl.when(pl.program_id(1) == 0)
    def _():
        s_ref[...] = jnp.zeros_like(s_ref)

    ti, ji, strict, incl = _chunk_masks(reverse)
    gb = gb_ref[0]
    gcol = _mm(incl.astype(BF16), gb, NN, 1, 2)
    inclt = (jnp.logical_or(ji < ti, ti == ji) if reverse else jnp.logical_or(ji > ti, ti == ji)).astype(BF16)
    grow = _mm(gbt_ref[0, 0], inclt, NN, 2, 1)
    last = 0 if reverse else c - 1
    q, k, v = q_ref[0], k_ref[0], v_ref[0]
    sls = [slice(h * kdim, (h + 1) * kdim) for h in range(DN_HEADS)]
    lhs, rhs, d1s, d3s, kkt, rt, khs, bhs, vhs, decs = ([] for _ in range(10))
    for h, sl in enumerate(sls):
        col = direction * DN_HEADS + h
        g_t = gb[:, col:col + 1]
        beta = gb[:, 2 * DN_HEADS + col:2 * DN_HEADS + col + 1]
        gt = gcol[:, col:col + 1]
        gxt = gt - g_t
        gj = grow[col:col + 1, :]
        gc = gcol[last:last + 1, col:col + 1]
        d3s.append(jnp.where(incl, jnp.exp(jnp.where(incl, gt - gj, 0.0)), 0.0))
        d1s.append(jnp.where(strict, jnp.exp(jnp.where(strict, gxt - gj, 0.0)), 0.0))
        kh_, qh_ = k[:, sl], q[:, sl]
        kp = kh_ * beta
        bb = kp * jnp.exp(g_t)
        egc = jnp.exp(gc - gt)
        lhs.append(jnp.concatenate([kh_, qh_], axis=0))
        rhs.append(jnp.concatenate([kp, bb], axis=0))
        kkt.append(kh_ * jnp.exp(gxt))
        rt.append(qh_ * jnp.exp(gt))
        khs.append(kp * egc)
        bhs.append(bb * egc)
        vhs.append(v[:, sl])
        decs.append(jnp.exp(gc))
    a = _mm_many(lhs, rhs, NT)
    ys = _chunk_update_many(s_ref, [x[:c, :c] * d for x, d in zip(a, d1s)], [x[:c, c:] * d for x, d in zip(a, d1s)],
                            [x[c:, :c] * d for x, d in zip(a, d3s)], [x[c:, c:] * d for x, d in zip(a, d3s)],
                            kkt, rt, khs, bhs, vhs, decs, ti, ji, kdim)
    for sl, y in zip(sls, ys):
        y_ref[0, :, sl] = y


def dn_scan(q, k, v, gb, gbt, direction):
    b, t, w = q.shape
    reverse = direction == 1
    cidx = _chunk_index(reverse)
    tok = pl.BlockSpec((1, CH, w), lambda i, j: (i, cidx(j), 0))
    return pl.pallas_call(
        functools.partial(_dn_scan_kernel, reverse=reverse, direction=direction),
        grid=(b, NC),
        in_specs=[tok, tok, tok,
                  pl.BlockSpec((1, CH, 128), lambda i, j: (i, cidx(j), 0)),
                  pl.BlockSpec((1, 1, 4 * DN_HEADS, CH), lambda i, j: (i, cidx(j), 0, 0))],
        out_specs=tok,
        out_shape=jax.ShapeDtypeStruct((b, t, w), F32),
        scratch_shapes=[pltpu.VMEM((DN_HEADS, DN_HEAD_DIM, DN_HEAD_DIM), F32)],
        compiler_params=_cparams("parallel", "arbitrary"),
        name="dn_scan_rev" if reverse else "dn_scan_fwd",
    )(q, k, v, gb, gbt)


def _dn_out_kernel(o0_ref, o1_ref, z_ref, x_ref, mod_ref, nw_ref, bd_ref, wo_ref, o_ref):
    o = o0_ref[0] + o1_ref[0]
    ms = _mm(o * o, bd_ref[...], NN, 2, 1) * (1.0 / DN_HEAD_DIM)
    on = o * lax.rsqrt(ms + EPS) * nw_ref[...]
    z = z_ref[0]
    gated = on * (z * _sigmoid(z))
    o_ref[0] = x_ref[0] + mod_ref[0, 0][2:3] * _dot(gated.astype(BF16), wo_ref[...])


def dn_out(o0, o1, p, xs, modv, nw_tiled, bd128, w_out, n_blocks):
    b, _, d = xs.shape
    tok = pl.BlockSpec((1, TB, DN_DIM), lambda i, j: (i, j, 0))
    full = lambda a: pl.BlockSpec(a.shape, lambda i, j: (0,) * a.ndim)
    params = (nw_tiled, bd128, w_out)
    return pl.pallas_call(
        _dn_out_kernel,
        grid=(b, n_blocks),
        in_specs=[tok, tok, pl.BlockSpec((1, TB, DN_DIM), lambda i, j: (i, j, 3)),
                  pl.BlockSpec((1, TB, d), lambda i, j: (i, j, 0)),
                  pl.BlockSpec((1, 1, ADA_CHUNKS, d), lambda i, j: (i, j // NB_LAT, 0, 0))]
        + [full(a) for a in params],
        out_specs=pl.BlockSpec((1, TB, d), lambda i, j: (i, j, 0)),
        out_shape=jax.ShapeDtypeStruct((b, n_blocks * TB, d), F32),
        compiler_params=_cparams("parallel", "parallel"),
        name="dn_out",
    )(o0, o1, p, xs, modv, *params)


def _block_diag_ones(width, head):
    i = jnp.arange(width) // head
    return (i[:, None] == i[None, :]).astype(BF16)


def kernel(x, c, ctx, c_ctx, ada_w, ada_b, norm_mix, norm_mlp, mlp_w1, mlp_w2, final_norm, ab_w_in, ab_w_out, rw_mu, rw_w0, rw_w_up, rw_a0, rw_a_up, rw_g_up, rw_k_k, rw_k_a, rw_r_k, rw_ln_w, rw_ln_b, hy_conv_w, hy_conv_b, hy_f_w1, hy_f_b1, hy_f_w2, hy_f_b2, hy_f_w3, hy_f_b3, hy_f_w4, hy_freq, hy_skip, dn_w_in, dn_conv_w, dn_A_log, dn_dt_bias, dn_norm, dn_w_out):
    bsz = x.shape[0]
    d = D_MODEL
    w = RW_WIDTH
    row = lambda a: a.reshape(1, -1)
    xs = jnp.concatenate([x, ctx], axis=1)

    cs = jnp.concatenate([c, c_ctx[None, :], jnp.zeros((16 - bsz - 1, d), F32)], axis=0)
    mod = ada_modulation(cs, ada_w, ada_b)

    def mod_vectors(layer):
        lat = mod[layer, :bsz].reshape(bsz, 1, ADA_CHUNKS, d)
        cx = jnp.broadcast_to(mod[layer, bsz].reshape(1, 1, ADA_CHUNKS, d), (bsz, 1, ADA_CHUNKS, d))
        return jnp.concatenate([lat, cx], axis=1)

    bd64 = _block_diag_ones(w, RW_HEAD_DIM)
    bd128 = _block_diag_ones(DN_DIM, DN_HEAD_DIM)

    modv = mod_vectors(0)
    p = nm_matmul(xs, norm_mix[0], modv, ab_w_in[0].astype(BF16))
    lora = jnp.zeros((128, 4 * w), F32)
    lora = lora.at[:64, :w].set(rw_w_up[0, 0]).at[:64, w:2 * w].set(rw_w_up[0, 1])
    lora = lora.at[64:, 2 * w:3 * w].set(rw_a_up[0, 0]).at[64:, 3 * w:].set(rw_a_up[0, 1])
    r, v, kk, g, bv, lw, kd, bb, s, x0 = ab_prep(
        p, row(rw_mu[0]), lora.astype(BF16), row(rw_w0[0]), row(rw_a0[0]), rw_g_up[0].astype(BF16),
        row(rw_k_k[0]), row(rw_k_a[0]), row(rw_r_k[0]), bd64, hy_conv_w[0], row(hy_conv_b[0]))
    y0 = rwkv_scan(r, v, kk, lw, kd, bb, 0)
    y1 = rwkv_scan(r, v, kk, lw, kd, bb, 1)
    filt = (hy_f_w1[0], hy_f_b1[0], hy_f_w2[0], hy_f_b2[0], hy_f_w3[0], hy_f_b3[0], hy_f_w4[0], hy_freq[0])
    convs = []
    for length, block in ((SEQ, 0), (CTX_LEN, SEQ // CTX_LEN)):
        wf = dft_matrix(length)
        hspec = filter_spectrum(wf, hyena_filter_halves(length, *filt))
        convs.append(hyena_long_conv(s, wf, wf.T, hspec, length, block))
    cv = jnp.concatenate(convs, axis=1)
    xs = ab_out(y0, y1, bv, g, x0, cv, s, xs, modv, row(rw_ln_w[0]), row(rw_ln_b[0]), row(hy_skip[0]), bd64,
                ab_w_out[0].astype(BF16))
    xs = mlp_block(xs, modv, norm_mlp[0], mlp_w1[0].astype(BF16), mlp_w2[0].astype(BF16), final_norm, NB, False)

    modv = mod_vectors(1)
    w_in = jnp.pad(dn_w_in[0], ((0, 0), (0, DN_COLS_PAD - DN_COLS))).astype(BF16)
    p = nm_matmul(xs, norm_mix[1], modv, w_in)
    pad_row = lambda a: jnp.pad(a.reshape(1, -1), ((0, 0), (0, 128 - 2 * DN_HEADS)))
    q, k, vv, gb = dn_prep(p, dn_conv_w[0], pad_row(dn_A_log[0]), pad_row(dn_dt_bias[0]), bd128)
    gbt = jnp.swapaxes(gb[:, :, :4 * DN_HEADS].reshape(bsz, NC, CH, 4 * DN_HEADS), 2, 3)
    o0 = dn_scan(q, k, vv, gb, gbt, 0)
    o1 = dn_scan(q, k, vv, gb, gbt, 1)
    xl = dn_out(o0, o1, p, xs, modv, row(jnp.tile(dn_norm[0], DN_HEADS)), bd128, dn_w_out[0].astype(BF16), NB_LAT)
    return mlp_block(xl, modv, norm_mlp[1], mlp_w1[1].astype(BF16), mlp_w2[1].astype(BF16), final_norm, NB_LAT, True)
```

```python
import functools
import math

import jax
import jax.numpy as jnp
from jax import lax
from jax.experimental import pallas as pl
from jax.experimental.pallas import tpu as pltpu

F32, BF16 = jnp.float32, jnp.bfloat16

D_MODEL = 1024
SEQ = 2048
CTX_LEN = 256
ADA_CHUNKS = 6
EPS = 1e-6
RW_WIDTH = 512
RW_HEAD_DIM = 64
RW_HEADS = 8
RW_COLS = 1792
RW_GN_EPS = 64e-5
HY_WIDTH = 512
HY_COLS = 1536
HY_BANDS = 16
HY_EMB = 33
HY_ORDER = 64
HY_FAST_DECAY = 0.3
HY_SLOW_DECAY = 1.5
HY_TARGET = 1e-2
AB_COLS = RW_COLS + HY_COLS
DN_HEADS = 8
DN_HEAD_DIM = 128
DN_DIM = 1024
DN_COLS = 4 * DN_DIM + 4 * DN_HEADS
DN_COLS_PAD = 4 * DN_DIM + 128

TB = 256
CH = 64
T_ALL = SEQ + CTX_LEN
NB = T_ALL // TB
NB_LAT = SEQ // TB
NC = T_ALL // CH
NC_LAT = SEQ // CH
FB = 256
SCAN_NB = 2
VMEM_LIMIT = 56 * 1024 * 1024

NN = ((1,), (0,))
NT = ((1,), (1,))
TN = ((0,), (0,))


def _dot(a, b, dims=NN):
    return lax.dot_general(a, b, (dims, ((), ())), preferred_element_type=F32)


def _split(x):
    hi = x.astype(BF16)
    lo = (x - hi.astype(F32)).astype(BF16)
    return hi, lo


def _mm(a, b, dims=NN, pa=1, pb=1):
    if a.dtype == BF16:
        a_hi, a_lo, pa = a, None, 1
    elif pa == 2:
        a_hi, a_lo = _split(a)
    else:
        a_hi, a_lo = a.astype(BF16), None
    if b.dtype == BF16:
        b_hi, b_lo, pb = b, None, 1
    elif pb == 2:
        b_hi, b_lo = _split(b)
    else:
        b_hi, b_lo = b.astype(BF16), None
    out = _dot(a_hi, b_hi, dims)
    if pa == 2:
        out = out + _dot(a_lo, b_hi, dims)
    if pb == 2:
        out = out + _dot(a_hi, b_lo, dims)
    return out


def _segsum(x, bd):
    parts = [_dot(x[:, g * 128:(g + 1) * 128].astype(BF16), bd) for g in range(x.shape[1] // 128)]
    return jnp.concatenate(parts, axis=1)


def _sigmoid(x):
    return 1.0 / (1.0 + jnp.exp(-x))


def _softplus(x):
    return jnp.maximum(x, 0.0) + jnp.log(1.0 + jnp.exp(-jnp.abs(x)))


def _rms_mod(x, nw, shift, scale):
    y = x * lax.rsqrt(jnp.mean(x * x, axis=-1, keepdims=True) + EPS)
    return (y * nw) * (1.0 + scale) + shift


def _cparams(*sem):
    return pltpu.CompilerParams(dimension_semantics=sem, vmem_limit_bytes=VMEM_LIMIT)


ADA_TN = 1536


def _ada_kernel(c_ref, w_ref, b_ref, o_ref):
    c = c_ref[...]
    a = c * _sigmoid(c)
    o_ref[0] = _mm(a, w_ref[0], NN, 2, 2) + b_ref[0]


def ada_modulation(cs, ada_w, ada_b):
    depth, d, n = ada_w.shape
    rows = cs.shape[0]
    return pl.pallas_call(
        _ada_kernel,
        grid=(depth, n // ADA_TN),
        in_specs=[pl.BlockSpec((rows, d), lambda l, j: (0, 0)),
                  pl.BlockSpec((1, d, ADA_TN), lambda l, j: (l, 0, j)),
                  pl.BlockSpec((1, 1, ADA_TN), lambda l, j: (l, 0, j))],
        out_specs=pl.BlockSpec((1, rows, ADA_TN), lambda l, j: (l, 0, j)),
        out_shape=jax.ShapeDtypeStruct((depth, rows, n), F32),
        compiler_params=_cparams("parallel", "parallel"),
        name="ada_modulation",
    )(cs, ada_w, ada_b.reshape(depth, 1, n))


def _token_block(x_ref, ctx_ref):
    return jnp.where(pl.program_id(1) == NB - 1, ctx_ref[0], x_ref[0])


def _split_specs(d):
    return [pl.BlockSpec((1, TB, d), lambda i, j: (i, jnp.minimum(j, NB_LAT - 1), 0)),
            pl.BlockSpec((1, TB, d), lambda i, j: (i, 0, 0))]


def _nm_matmul_kernel(*refs, split):
    x = _token_block(refs[0], refs[1]) if split else refs[0][0]
    nw_ref, mod_ref, w_ref, o_ref = refs[-4:]
    m = mod_ref[0, 0]
    h = _rms_mod(x, nw_ref[...], m[0:1], m[1:2])
    o_ref[0] = _dot(h.astype(BF16), w_ref[...])


def nm_matmul(xs, nw, modv, w):
    split = isinstance(xs, tuple)
    xin = xs if split else (xs,)
    b, _, d = xin[0].shape
    n = w.shape[1]
    return pl.pallas_call(
        functools.partial(_nm_matmul_kernel, split=split),
        grid=(b, NB),
        in_specs=(_split_specs(d) if split else [pl.BlockSpec((1, TB, d), lambda i, j: (i, j, 0))])
        + [pl.BlockSpec((1, d), lambda i, j: (0, 0)),
           pl.BlockSpec((1, 1, ADA_CHUNKS, d), lambda i, j: (i, j // NB_LAT, 0, 0)),
           pl.BlockSpec((d, n), lambda i, j: (0, 0))],
        out_specs=pl.BlockSpec((1, TB, n), lambda i, j: (i, j, 0)),
        out_shape=jax.ShapeDtypeStruct((b, T_ALL, n), F32),
        compiler_params=_cparams("parallel", "parallel"),
        name="nm_matmul",
    )(*xin, nw.reshape(1, d), modv, w)


def _halo_specs(width, col_block=0):
    prev = pl.BlockSpec((1, 8, width), lambda i, j: (i, jnp.maximum(j * (TB // 8) - 1, 0), col_block))
    nxt = pl.BlockSpec((1, 8, width), lambda i, j: (i, jnp.minimum((j + 1) * (TB // 8), T_ALL // 8 - 1), col_block))
    return prev, nxt


def _neighbours(x, prev_ref, next_ref):
    j = pl.program_id(1)
    pv = jnp.where(jnp.logical_and(j != 0, j != NB - 1), 1.0, 0.0).astype(F32)
    nv = jnp.where(j < NB - 2, 1.0, 0.0).astype(F32)
    prow = prev_ref[0, 7:8, :] * pv
    nrow = next_ref[0, 0:1, :] * nv
    rows = lax.broadcasted_iota(jnp.int32, (TB, 1), 0)
    xprev = jnp.where(rows == 0, prow, pltpu.roll(x, 1, 0))
    xnext = jnp.where(rows == TB - 1, nrow, pltpu.roll(x, TB - 1, 0))
    return xprev, xnext


def _ab_prep_kernel(p_ref, pp_ref, pn_ref, mu_ref, lora_ref, w0_ref, a0_ref, gup_ref, kk_ref, ka_ref, rk_ref,
                    bd_ref, cw_ref, cb_ref,
                    r_ref, v_ref, kkn_ref, g_ref, bv_ref, lw_ref, kd_ref, b_ref, s_ref, x0_ref):
    x = p_ref[0]
    xprev, xnext = _neighbours(x, pp_ref, pn_ref)
    w = RW_WIDTH
    xr = x[:, :RW_COLS]
    ps = xr + mu_ref[...] * (0.5 * (xprev[:, :RW_COLS] + xnext[:, :RW_COLS]) - xr)
    r, k, v = ps[:, :w], ps[:, w:2 * w], ps[:, 2 * w:3 * w]
    slab = ps[:, 3 * w:3 * w + 128]
    gl = ps[:, 3 * w + 128:3 * w + 256]
    lane = lax.broadcasted_iota(jnp.int32, (1, 128), 1)
    z = jnp.where(lane < 64, jnp.tanh(slab), slab)
    lora = _dot(z.astype(BF16), lora_ref[...])
    bd = bd_ref[...]
    kkraw = k * kk_ref[...]
    kk = kkraw * lax.rsqrt(_segsum(kkraw * kkraw, bd) + EPS)
    ksum = jnp.zeros_like(k)
    for d in range(2):
        w_log = -_softplus(-(w0_ref[:, d * w:(d + 1) * w] + lora[:, d * w:(d + 1) * w])) - 0.5
        lw_ref[d, 0] = -jnp.exp(w_log)
        a = _sigmoid(a0_ref[:, d * w:(d + 1) * w] + lora[:, (2 + d) * w:(3 + d) * w])
        kd = k * (1.0 + (a - 1.0) * ka_ref[...])
        kd_ref[d, 0] = kd.astype(BF16)
        b_ref[d, 0] = (a * kk).astype(BF16)
        ksum = ksum + kd
    r_ref[0] = r.astype(BF16)
    v_ref[0] = v.astype(BF16)
    kkn_ref[0] = kk.astype(BF16)
    g_ref[0] = _dot(_sigmoid(gl).astype(BF16), gup_ref[...]).astype(BF16)
    bv_ref[0] = (_segsum(r * ksum * rk_ref[...], bd) * v).astype(BF16)
    u = x[:, RW_COLS:]
    c = (cw_ref[0:1, :] * xprev[:, RW_COLS:] + cw_ref[1:2, :] * u + cw_ref[2:3, :] * xnext[:, RW_COLS:]
         + cb_ref[...])
    hw = HY_WIDTH
    x0_ref[0] = c[:, :hw].astype(BF16)
    s_ref[0] = (c[:, hw:2 * hw] * c[:, 2 * hw:]).astype(BF16)


def ab_prep(p, mu, lora, w0, a0, gup, k_k, k_a, r_k, bd64, conv_w, conv_b):
    b, t, n = p.shape
    w = RW_WIDTH
    prev, nxt = _halo_specs(n)
    full = lambda a: pl.BlockSpec(a.shape, lambda i, j: (0,) * a.ndim)
    params = (mu, lora, w0, a0, gup, k_k, k_a, r_k, bd64, conv_w, conv_b)
    tok = pl.BlockSpec((1, TB, w), lambda i, j: (i, j, 0))
    tok2 = pl.BlockSpec((2, 1, TB, w), lambda i, j: (0, i, j, 0))
    s1 = jax.ShapeDtypeStruct((b, t, w), BF16)
    s2 = jax.ShapeDtypeStruct((2, b, t, w), BF16)
    s2f = jax.ShapeDtypeStruct((2, b, t, w), F32)
    return pl.pallas_call(
        _ab_prep_kernel,
        grid=(b, t // TB),
        in_specs=[pl.BlockSpec((1, TB, n), lambda i, j: (i, j, 0)), prev, nxt] + [full(a) for a in params],
        out_specs=[tok, tok, tok, tok, tok, tok2, tok2, tok2, tok, tok],
        out_shape=[s1, s1, s1, s1, s1, s2f, s2, s2, s1, s1],
        compiler_params=_cparams("parallel", "parallel"),
        name="ab_prep",
    )(p, p, p, *params)


def _chunk_masks(reverse):
    c = CH
    ti = lax.broadcasted_iota(jnp.int32, (c, c), 0)
    ji = lax.broadcasted_iota(jnp.int32, (c, c), 1)
    strict = (ji > ti) if reverse else (ji < ti)
    incl = jnp.logical_or(strict, ti == ji)
    return ti, ji, strict, incl


def _mm_many(a_list, b_list, dims=NN):
    ops = [(a.astype(BF16), b.astype(BF16)) for a, b in zip(a_list, b_list)]
    return [_dot(a, b, dims) for a, b in ops]


def _tri_inverse_many(n_list, ti, ji):
    eye = (ti == ji).astype(F32)
    pair = (ti >> 1) == (ji >> 1)
    dinvs = [eye - jnp.where(pair, n, 0.0) for n in n_list]
    s = 1
    while (2 << s) <= CH:
        off = jnp.logical_and((ti >> (s + 1)) == (ji >> (s + 1)), (ti >> s) != (ji >> s))
        coffs = [jnp.where(off, n, 0.0) for n in n_list]
        corr = _mm_many(_mm_many(dinvs, coffs), dinvs)
        dinvs = [d - t for d, t in zip(dinvs, corr)]
        s += 1
    return dinvs


def _chunk_update_many(s_ref, a, kkt, rt, kh, bh, vh, dec_c, kdim):
    n = len(a)
    c = CH
    ti = lax.broadcasted_iota(jnp.int32, (c, c), 0)
    ji = lax.broadcasted_iota(jnp.int32, (c, c), 1)
    tms = _tri_inverse_many([x[:c, c:] for x in a], ti, ji)
    av = _mm_many([x[:, :c] for x in a], vh)
    tw = _mm_many(tms, [jnp.concatenate([kkt[i], av[i][:c]], axis=1) for i in range(n)])
    states = [s_ref[i] for i in range(n)]
    p = _mm_many([jnp.concatenate([tw[i][:, :kdim], rt[i]], axis=0) for i in range(n)], states, NT)
    u = [p[i][:c] + tw[i][:, kdim:] for i in range(n)]
    au = _mm_many([x[c:, c:] for x in a], u)
    ds = _mm_many([jnp.concatenate([vh[i], -u[i]], axis=0) for i in range(n)],
                  [jnp.concatenate([kh[i], bh[i]], axis=0) for i in range(n)], TN)
    for i in range(n):
        s_ref[i] = states[i] * dec_c[i] + ds[i]
    return [p[i][c:] + av[i][c:] - au[i] for i in range(n)]


def _chunk_index(reverse):
    if reverse:
        return lambda i: NC - 1 - i
    return lambda i: (i + NC_LAT) % NC


def _rwkv_scan_kernel(rf_ref, vf_ref, kkf_ref, rr_ref, vr_ref, kkr_ref, lwf_ref, kdf_ref, bf_ref,
                      lwr_ref, kdr_ref, br_ref, yf_ref, yr_ref, s_ref):
    c, kdim = CH, RW_HEAD_DIM

    @pl.when(pl.program_id(1) == 0)
    def _():
        s_ref[...] = jnp.zeros_like(s_ref)

    sls = [slice(h * kdim, (h + 1) * kdim) for h in range(RW_HEADS)]
    heads = lambda x: [x[:, sl] for sl in sls]
    ri = lax.broadcasted_iota(jnp.int32, (2 * c, 2 * c), 0)
    ci = lax.broadcasted_iota(jnp.int32, (2 * c, 2 * c), 1)
    rt_, ct_ = ri & (c - 1), ci & (c - 1)
    lhs, rhs, amasks, kkt, rt, kh, bh, vh, dec = ([] for _ in range(9))
    dirs = ((rf_ref, vf_ref, kkf_ref, lwf_ref, kdf_ref, bf_ref), (rr_ref, vr_ref, kkr_ref, lwr_ref, kdr_ref, br_ref))
    for d, (r_ref, v_ref, kk_ref, lw_ref, kd_ref, b_ref) in enumerate(dirs):
        reverse = d == 1
        _, _, _, incl = _chunk_masks(reverse)
        before = (ct_ > rt_) if reverse else (ct_ < rt_)
        amask = jnp.logical_or(before, jnp.logical_and(rt_ == ct_, ri >= c))
        last = 0 if reverse else c - 1
        for bi in range(SCAN_NB):
            lw = lw_ref[0, bi]
            g = _mm(incl.astype(BF16), lw, NN, 1, 2)
            gc = g[last:last + 1, :]
            eng = jnp.exp(-g)
            egc = jnp.exp(gc - g)
            kd, bb = kd_ref[0, bi].astype(F32), b_ref[0, bi].astype(F32)
            kkt_all = kk_ref[bi].astype(F32) * jnp.exp(g - lw)
            rt_all = r_ref[bi].astype(F32) * jnp.exp(g)
            kn_all, bn_all = kd * eng, bb * eng
            lhs += [jnp.concatenate([x, y], axis=0) for x, y in zip(heads(kkt_all), heads(rt_all))]
            rhs += [jnp.concatenate([x, y], axis=0) for x, y in zip(heads(kn_all), heads(bn_all))]
            amasks += [amask] * RW_HEADS
            kkt += heads(kkt_all)
            rt += heads(rt_all)
            kh += heads(kd * egc)
            bh += heads(bb * egc)
            vh += heads(v_ref[bi])
            dec += heads(jnp.exp(gc))
    a = [jnp.where(m, x, 0.0) for m, x in zip(amasks, _mm_many(lhs, rhs, NT))]
    ys = _chunk_update_many(s_ref, a, kkt, rt, kh, bh, vh, dec, kdim)
    for d, y_ref in enumerate((yf_ref, yr_ref)):
        for bi in range(SCAN_NB):
            for h, sl in enumerate(sls):
                y_ref[bi, :, sl] = ys[(d * SCAN_NB + bi) * RW_HEADS + h]


def rwkv_scan(r, v, kk, lw, kd, bb):
    b, t, w = r.shape
    specs, specs_d = [], []
    for d in range(2):
        cidx = _chunk_index(d == 1)
        specs += [pl.BlockSpec((SCAN_NB, CH, w), lambda i, j, cidx=cidx: (i, cidx(j), 0))] * 3
        specs_d += [pl.BlockSpec((1, SCAN_NB, CH, w), lambda i, j, cidx=cidx, d=d: (d, i, cidx(j), 0))] * 3
    out = jax.ShapeDtypeStruct((b, t, w), F32)
    return pl.pallas_call(
        _rwkv_scan_kernel,
        grid=(b // SCAN_NB, NC),
        in_specs=specs + specs_d,
        out_specs=[specs[0], specs[3]],
        out_shape=[out, out],
        scratch_shapes=[pltpu.VMEM((2 * SCAN_NB * RW_HEADS, RW_HEAD_DIM, RW_HEAD_DIM), F32)],
        compiler_params=_cparams("parallel", "arbitrary"),
        name="rwkv_scan",
    )(r, v, kk, r, v, kk, lw, kd, bb, lw, kd, bb)


def _hy_filter_kernel(z_ref, t_ref, w1_ref, b1_ref, w2_ref, b2_ref, w3_ref, b3_ref, w4_ref, fr_ref, dl_ref, o_ref):
    fr = fr_ref[...]
    h = jnp.sin(fr * (_mm(z_ref[...], w1_ref[...], NN, 2, 2) + b1_ref[...]))
    h = jnp.sin(fr * (_mm(h, w2_ref[...], NN, 2, 2) + b2_ref[...]))
    h = jnp.sin(fr * (_mm(h, w3_ref[...], NN, 2, 2) + b3_ref[...]))
    h = _mm(h, w4_ref[...], NN, 2, 2)
    h = h * jnp.exp(-t_ref[...] * dl_ref[...])
    hw = HY_WIDTH
    rows = lax.broadcasted_iota(jnp.int32, (h.shape[0], 1), 0)
    hf = h[:, :hw]
    hb = jnp.where(rows == 0, 0.0, h[:, hw:])
    norm = jnp.sum(jnp.abs(hf), axis=0, keepdims=True) + jnp.sum(jnp.abs(hb), axis=0, keepdims=True)
    o_ref[:, :hw] = (hf + hb) / norm
    o_ref[:, hw:] = (hf - hb) / norm


def hyena_filter_halves(length, w1, b1, w2, b2, w3, b3, w4, freq):
    t = jnp.linspace(0.0, 1.0, length, dtype=F32)[:, None]
    wv = 2 * math.pi * jnp.arange(length, dtype=F32)[:, None] / length
    f = jnp.linspace(1e-4, HY_BANDS - 1, HY_BANDS, dtype=F32)[None, :]
    z = jnp.concatenate([t, jnp.cos(f * wv), -jnp.sin(f * wv)], axis=-1)
    z = jnp.pad(z, ((0, 0), (0, HY_ORDER - HY_EMB)))
    w1p = jnp.pad(w1, ((0, HY_ORDER - HY_EMB), (0, 0)))
    deltas = jnp.abs(jnp.linspace(math.log(HY_TARGET) / HY_SLOW_DECAY, math.log(HY_TARGET) / HY_FAST_DECAY,
                                  HY_WIDTH, dtype=F32))
    dl = jnp.tile(deltas, 2)[None, :]
    row = lambda a: a.reshape(1, -1)
    return pl.pallas_call(
        _hy_filter_kernel,
        out_shape=jax.ShapeDtypeStruct((length, 2 * HY_WIDTH), F32),
        compiler_params=pltpu.CompilerParams(vmem_limit_bytes=VMEM_LIMIT),
        name="hyena_filter",
    )(z, t, w1p, row(b1), w2, row(b2), w3, row(b3), w4, row(freq), dl)


def dft_matrices(length):
    n = 2 * length
    nfb = length // FB
    t = jnp.arange(length, dtype=jnp.int32)
    ang = lambda f: ((f[:, None] * t[None, :]) % n).astype(F32) * (2 * math.pi / n)
    a_hi = ang(jnp.arange(nfb, dtype=jnp.int32) * FB)
    a_lo = ang(jnp.arange(FB, dtype=jnp.int32))
    nyq = jnp.where(t % 2 == 0, 1.0, -1.0).astype(F32)

    def build(ch, sh, cl, sl, first, nyq_b, axis):
        cosm = ch * cl - sh * sl
        msin = -(sh * cl + ch * sl)
        msin = jnp.where(first, nyq_b, msin)
        return jnp.concatenate([cosm, msin], axis=axis).astype(BF16)

    bi = lax.broadcasted_iota
    first = jnp.logical_and(bi(jnp.int32, (nfb, FB, 1), 0) == 0, bi(jnp.int32, (nfb, FB, 1), 1) == 0)
    wf = build(jnp.cos(a_hi)[:, None, :], jnp.sin(a_hi)[:, None, :], jnp.cos(a_lo)[None], jnp.sin(a_lo)[None],
               first, nyq[None, None, :], 1).reshape(n, length)
    first_t = jnp.logical_and(bi(jnp.int32, (1, nfb, FB), 1) == 0, bi(jnp.int32, (1, nfb, FB), 2) == 0)
    wft = build(jnp.cos(a_hi).T[:, :, None], jnp.sin(a_hi).T[:, :, None], jnp.cos(a_lo).T[:, None, :],
                jnp.sin(a_lo).T[:, None, :], first_t, nyq[:, None, None], 2).reshape(length, n)
    return wf, wft


def _spectrum_kernel(w_ref, h_ref, o_ref):
    o_ref[...] = _mm(w_ref[...], h_ref[...], NN, 1, 2)


def filter_spectrum(wf, hpm):
    n, length = wf.shape
    cols = hpm.shape[1]
    return pl.pallas_call(
        _spectrum_kernel,
        grid=(n // (2 * FB),),
        in_specs=[pl.BlockSpec((2 * FB, length), lambda i: (i, 0)),
                  pl.BlockSpec((length, cols), lambda i: (0, 0))],
        out_specs=pl.BlockSpec((2 * FB, cols), lambda i: (i, 0)),
        out_shape=jax.ShapeDtypeStruct((n, cols), F32),
        compiler_params=_cparams("parallel"),
        name="filter_spectrum",
    )(wf, hpm)


def _hy_conv_kernel(s_ref, wf_ref, wft_ref, h_ref, o_ref, *, n_fft):
    fb = pl.program_id(1)

    @pl.when(fb == 0)
    def _():
        o_ref[...] = jnp.zeros_like(o_ref)

    hw = HY_WIDTH
    x = _dot(wf_ref[...], s_ref[0])
    xr, xi = x[:FB], x[FB:]
    hraw = h_ref[...]
    hr, hi, hn = hraw[:FB, :hw], hraw[FB:, hw:], hraw[FB:, :hw]
    fidx = lax.broadcasted_iota(jnp.int32, (FB, 1), 0) + fb * FB
    m = (fidx != 0).astype(F32)
    him = hi * m
    hc = hr * m + hn * (1.0 - m)
    cf = jnp.where(fidx != 0, 2.0 / n_fft, 1.0 / n_fft).astype(F32)
    yr = (xr * hr - xi * him) * cf
    yi = (xr * him + xi * hc) * cf
    y = jnp.concatenate([yr, yi], axis=0).astype(BF16)
    o_ref[0] += _dot(wft_ref[...], y)


def hyena_long_conv(s, wf, wft, hspec, length, block):
    b = s.shape[0]
    hw = HY_WIDTH
    return pl.pallas_call(
        functools.partial(_hy_conv_kernel, n_fft=2 * length),
        grid=(b, length // FB),
        in_specs=[pl.BlockSpec((1, length, hw), lambda i, j: (i, block, 0)),
                  pl.BlockSpec((2 * FB, length), lambda i, j: (j, 0)),
                  pl.BlockSpec((length, 2 * FB), lambda i, j: (0, j)),
                  pl.BlockSpec((2 * FB, 2 * hw), lambda i, j: (j, 0))],
        out_specs=pl.BlockSpec((1, length, hw), lambda i, j: (i, 0, 0)),
        out_shape=jax.ShapeDtypeStruct((b, length, hw), F32),
        compiler_params=_cparams("parallel", "arbitrary"),
        name="hyena_long_conv",
    )(s, wf, wft, hspec)


def _ab_out_kernel(y0_ref, y1_ref, bv_ref, g_ref, x0_ref, cv_ref, s_ref, x_ref, ctx_ref, mod_ref, lnw_ref, lnb_ref,
                   skip_ref, bd_ref, wo_ref, o_ref):
    bd = bd_ref[...]
    inv = 1.0 / RW_HEAD_DIM
    y = y0_ref[0] + y1_ref[0]
    mean = _segsum(y, bd) * inv
    yc = y - mean
    var = _segsum(yc * yc, bd) * inv
    yn = yc * lax.rsqrt(var + RW_GN_EPS)
    a = (yn * lnw_ref[...] + lnb_ref[...] + bv_ref[0].astype(F32)) * g_ref[0].astype(F32)
    s = s_ref[0].astype(F32)
    bh = x0_ref[0].astype(F32) * (cv_ref[0] + s * skip_ref[...])
    w = RW_WIDTH
    out = _dot(a.astype(BF16), wo_ref[:w, :]) + _dot(bh.astype(BF16), wo_ref[w:, :])
    o_ref[0] = _token_block(x_ref, ctx_ref) + mod_ref[0, 0][2:3] * out


def ab_out(y0, y1, bv, g, x0, cv, s, x, ctx, modv, ln_w, ln_b, skip, bd64, w_out):
    b, _, d = x.shape
    w = RW_WIDTH
    tok = pl.BlockSpec((1, TB, w), lambda i, j: (i, j, 0))
    full = lambda a: pl.BlockSpec(a.shape, lambda i, j: (0,) * a.ndim)
    params = (ln_w, ln_b, skip, bd64, w_out)
    return pl.pallas_call(
        _ab_out_kernel,
        grid=(b, NB),
        in_specs=[tok] * 7 + _split_specs(d)
        + [pl.BlockSpec((1, 1, ADA_CHUNKS, d), lambda i, j: (i, j // NB_LAT, 0, 0))] + [full(a) for a in params],
        out_specs=pl.BlockSpec((1, TB, d), lambda i, j: (i, j, 0)),
        out_shape=jax.ShapeDtypeStruct((b, T_ALL, d), F32),
        compiler_params=_cparams("parallel", "parallel"),
        name="ab_out",
    )(y0, y1, bv, g, x0, cv, s, x, ctx, modv, *params)


MLP_HC = 1024


def _mlp_kernel(x_ref, mod_ref, nw_ref, w1_ref, w2_ref, fw_ref, o_ref, *, final):
    x = x_ref[0]
    m = mod_ref[0, 0]
    h = _rms_mod(x, nw_ref[...], m[3:4], m[4:5]).astype(BF16)
    acc = jnp.zeros_like(x)
    for c in range(w1_ref.shape[1] // MLP_HC):
        hid = _dot(h, w1_ref[:, c * MLP_HC:(c + 1) * MLP_HC])
        hid = jnp.square(jnp.maximum(hid, 0.0))
        acc = acc + _dot(hid.astype(BF16), w2_ref[c * MLP_HC:(c + 1) * MLP_HC, :])
    o = x + m[5:6] * acc
    if final:
        o = o * lax.rsqrt(jnp.mean(o * o, axis=-1, keepdims=True) + EPS) * fw_ref[...]
    o_ref[0] = o


def mlp_block(xs, modv, nw, w1, w2, fw, n_blocks, final):
    b, t, d = xs.shape
    hdim = w1.shape[1]
    return pl.pallas_call(
        functools.partial(_mlp_kernel, final=final),
        grid=(b, n_blocks),
        in_specs=[pl.BlockSpec((1, TB, d), lambda i, j: (i, j, 0)),
                  pl.BlockSpec((1, 1, ADA_CHUNKS, d), lambda i, j: (i, j // NB_LAT, 0, 0)),
                  pl.BlockSpec((1, d), lambda i, j: (0, 0)),
                  pl.BlockSpec((d, hdim), lambda i, j: (0, 0)),
                  pl.BlockSpec((hdim, d), lambda i, j: (0, 0)),
                  pl.BlockSpec((1, d), lambda i, j: (0, 0))],
        out_specs=pl.BlockSpec((1, TB, d), lambda i, j: (i, j, 0)),
        out_shape=jax.ShapeDtypeStruct((b, n_blocks * TB, d), F32),
        compiler_params=_cparams("parallel", "parallel"),
        name="mlp_block",
    )(xs, modv, nw.reshape(1, d), w1, w2, fw.reshape(1, d))


def _dn_prep_kernel(p_ref, pp_ref, pn_ref, ab_ref, cw_ref, alog_ref, dtb_ref, bd_ref,
                    q_ref, k_ref, v_ref, gb_ref):
    x = p_ref[0]
    xprev, xnext = _neighbours(x, pp_ref, pn_ref)
    c = cw_ref[0:1, :] * xprev + cw_ref[1:2, :] * x + cw_ref[2:3, :] * xnext
    c = c * _sigmoid(c)
    dd = DN_DIM
    bd = bd_ref[...]
    q, k = c[:, :dd], c[:, dd:2 * dd]
    q_ref[0] = (q * lax.rsqrt(_segsum(q * q, bd) + EPS) * (DN_HEAD_DIM ** -0.5)).astype(BF16)
    k_ref[0] = (k * lax.rsqrt(_segsum(k * k, bd) + EPS)).astype(BF16)
    v_ref[0] = c[:, 2 * dd:].astype(BF16)
    slab = ab_ref[0]
    lane = lax.broadcasted_iota(jnp.int32, (1, 128), 1)
    gdec = -jnp.exp(alog_ref[...]) * _softplus(slab + dtb_ref[...])
    gb_ref[0] = jnp.where(lane < 2 * DN_HEADS, gdec, _sigmoid(slab))


def dn_prep(p, conv_w, alog_row, dtb_row, bd128):
    b, t, _ = p.shape
    wq = 3 * DN_DIM
    prev, nxt = _halo_specs(wq)
    full = lambda a: pl.BlockSpec(a.shape, lambda i, j: (0,) * a.ndim)
    params = (conv_w, alog_row, dtb_row, bd128)
    tok = pl.BlockSpec((1, TB, DN_DIM), lambda i, j: (i, j, 0))
    s1 = jax.ShapeDtypeStruct((b, t, DN_DIM), BF16)
    return pl.pallas_call(
        _dn_prep_kernel,
        grid=(b, t // TB),
        in_specs=[pl.BlockSpec((1, TB, wq), lambda i, j: (i, j, 0)), prev, nxt,
                  pl.BlockSpec((1, TB, 128), lambda i, j: (i, j, 4 * DN_DIM // 128))] + [full(a) for a in params],
        out_specs=[tok, tok, tok, pl.BlockSpec((1, TB, 128), lambda i, j: (i, j, 0))],
        out_shape=[s1, s1, s1, jax.ShapeDtypeStruct((b, t, 128), F32)],
        compiler_params=_cparams("parallel", "parallel"),
        name="dn_prep",
    )(p, p, p, p, *params)


def _dn_scan_kernel(qf_ref, kf_ref, vf_ref, gbf_ref, gbtf_ref, qr_ref, kr_ref, vr_ref, gbr_ref, gbtr_ref,
                    yf_ref, yr_ref, s_ref):
    c, kdim = CH, DN_HEAD_DIM

    @pl.when(pl.program_id(1) == 0)
    def _():
        s_ref[...] = jnp.zeros_like(s_ref)

    sls = [slice(h * kdim, (h + 1) * kdim) for h in range(DN_HEADS)]
    lhs, rhs, dfull, kkt, rt, khs, bhs, vhs, decs = ([] for _ in range(9))
    dirs = ((qf_ref, kf_ref, vf_ref, gbf_ref, gbtf_ref), (qr_ref, kr_ref, vr_ref, gbr_ref, gbtr_ref))
    for d, (q_ref, k_ref, v_ref, gb_ref, gbt_ref) in enumerate(dirs):
        reverse = d == 1
        ti, ji, strict, incl = _chunk_masks(reverse)
        inclt = (jnp.logical_or(ji < ti, ti == ji) if reverse else jnp.logical_or(ji > ti, ti == ji)).astype(BF16)
        last = 0 if reverse else c - 1
        for bi in range(SCAN_NB):
            gb = gb_ref[bi]
            gcol = _mm(incl.astype(BF16), gb, NN, 1, 2)
            grow = _mm(gbt_ref[bi, 0], inclt, NN, 2, 1)
            q, k, v = q_ref[bi].astype(F32), k_ref[bi].astype(F32), v_ref[bi]
            for h, sl in enumerate(sls):
                col = d * DN_HEADS + h
                g_t = gb[:, col:col + 1]
                beta = gb[:, 2 * DN_HEADS + col:2 * DN_HEADS + col + 1]
                gt = gcol[:, col:col + 1]
                gxt = gt - g_t
                gj = grow[col:col + 1, :]
                gc = gcol[last:last + 1, col:col + 1]
                d3 = jnp.where(incl, jnp.exp(jnp.where(incl, gt - gj, 0.0)), 0.0)
                d1 = jnp.where(strict, jnp.exp(jnp.where(strict, gxt - gj, 0.0)), 0.0)
                dfull.append(jnp.concatenate([jnp.concatenate([d1, d1], axis=1),
                                              jnp.concatenate([d3, d3], axis=1)], axis=0))
                kh_, qh_ = k[:, sl], q[:, sl]
                kp = kh_ * beta
                bb = kp * jnp.exp(g_t)
                egc = jnp.exp(gc - gt)
                lhs.append(jnp.concatenate([kh_, qh_], axis=0))
                rhs.append(jnp.concatenate([kp, bb], axis=0))
                kkt.append(kh_ * jnp.exp(gxt))
                rt.append(qh_ * jnp.exp(gt))
                khs.append(kp * egc)
                bhs.append(bb * egc)
                vhs.append(v[:, sl])
                decs.append(jnp.exp(gc))
    a = [x * dm for x, dm in zip(_mm_many(lhs, rhs, NT), dfull)]
    ys = _chunk_update_many(s_ref, a, kkt, rt, khs, bhs, vhs, decs, kdim)
    for d, y_ref in enumerate((yf_ref, yr_ref)):
        for bi in range(SCAN_NB):
            for h, sl in enumerate(sls):
                y_ref[bi, :, sl] = ys[(d * SCAN_NB + bi) * DN_HEADS + h]


def dn_scan(q, k, v, gb, gbt):
    b, t, w = q.shape
    specs = []
    for d in range(2):
        cidx = _chunk_index(d == 1)
        tok = pl.BlockSpec((SCAN_NB, CH, w), lambda i, j, cidx=cidx: (i, cidx(j), 0))
        specs += [tok, tok, tok,
                  pl.BlockSpec((SCAN_NB, CH, 128), lambda i, j, cidx=cidx: (i, cidx(j), 0)),
                  pl.BlockSpec((SCAN_NB, 1, 4 * DN_HEADS, CH), lambda i, j, cidx=cidx: (i, cidx(j), 0, 0))]
    out = jax.ShapeDtypeStruct((b, t, w), F32)
    return pl.pallas_call(
        _dn_scan_kernel,
        grid=(b // SCAN_NB, NC),
        in_specs=specs,
        out_specs=[specs[0], specs[5]],
        out_shape=[out, out],
        scratch_shapes=[pltpu.VMEM((2 * SCAN_NB * DN_HEADS, DN_HEAD_DIM, DN_HEAD_DIM), F32)],
        compiler_params=_cparams("parallel", "arbitrary"),
        name="dn_scan",
    )(q, k, v, gb, gbt, q, k, v, gb, gbt)


def _dn_out_kernel(o0_ref, o1_ref, z_ref, x_ref, mod_ref, nw_ref, bd_ref, wo_ref, o_ref):
    o = o0_ref[0] + o1_ref[0]
    ms = _segsum(o * o, bd_ref[...]) * (1.0 / DN_HEAD_DIM)
    on = o * lax.rsqrt(ms + EPS) * nw_ref[...]
    z = z_ref[0]
    gated = on * (z * _sigmoid(z))
    o_ref[0] = x_ref[0] + mod_ref[0, 0][2:3] * _dot(gated.astype(BF16), wo_ref[...])


def dn_out(o0, o1, p, xs, modv, nw_tiled, bd128, w_out, n_blocks):
    b, _, d = xs.shape
    tok = pl.BlockSpec((1, TB, DN_DIM), lambda i, j: (i, j, 0))
    full = lambda a: pl.BlockSpec(a.shape, lambda i, j: (0,) * a.ndim)
    params = (nw_tiled, bd128, w_out)
    return pl.pallas_call(
        _dn_out_kernel,
        grid=(b, n_blocks),
        in_specs=[tok, tok, pl.BlockSpec((1, TB, DN_DIM), lambda i, j: (i, j, 3)),
                  pl.BlockSpec((1, TB, d), lambda i, j: (i, j, 0)),
                  pl.BlockSpec((1, 1, ADA_CHUNKS, d), lambda i, j: (i, j // NB_LAT, 0, 0))]
        + [full(a) for a in params],
        out_specs=pl.BlockSpec((1, TB, d), lambda i, j: (i, j, 0)),
        out_shape=jax.ShapeDtypeStruct((b, n_blocks * TB, d), F32),
        compiler_params=_cparams("parallel", "parallel"),
        name="dn_out",
    )(o0, o1, p, xs, modv, *params)


def _block_diag_ones(head):
    i = jnp.arange(128) // head
    return (i[:, None] == i[None, :]).astype(BF16)


def kernel(x, c, ctx, c_ctx, ada_w, ada_b, norm_mix, norm_mlp, mlp_w1, mlp_w2, final_norm, ab_w_in, ab_w_out, rw_mu, rw_w0, rw_w_up, rw_a0, rw_a_up, rw_g_up, rw_k_k, rw_k_a, rw_r_k, rw_ln_w, rw_ln_b, hy_conv_w, hy_conv_b, hy_f_w1, hy_f_b1, hy_f_w2, hy_f_b2, hy_f_w3, hy_f_b3, hy_f_w4, hy_freq, hy_skip, dn_w_in, dn_conv_w, dn_A_log, dn_dt_bias, dn_norm, dn_w_out):
    bsz = x.shape[0]
    d = D_MODEL
    w = RW_WIDTH
    row = lambda a: a.reshape(1, -1)

    cs = jnp.concatenate([c, c_ctx[None, :], jnp.zeros((16 - bsz - 1, d), F32)], axis=0)
    mod = ada_modulation(cs, ada_w, ada_b)

    def mod_vectors(layer):
        lat = mod[layer, :bsz].reshape(bsz, 1, ADA_CHUNKS, d)
        cx = jnp.broadcast_to(mod[layer, bsz].reshape(1, 1, ADA_CHUNKS, d), (bsz, 1, ADA_CHUNKS, d))
        return jnp.concatenate([lat, cx], axis=1)

    bd64 = _block_diag_ones(RW_HEAD_DIM)
    bd128 = _block_diag_ones(DN_HEAD_DIM)

    modv = mod_vectors(0)
    p = nm_matmul((x, ctx), norm_mix[0], modv, ab_w_in[0].astype(BF16))
    zeros = jnp.zeros((64, 2 * w), F32)
    lora = jnp.concatenate([jnp.concatenate([rw_w_up[0, 0], rw_w_up[0, 1], zeros], axis=1),
                            jnp.concatenate([zeros, rw_a_up[0, 0], rw_a_up[0, 1]], axis=1)], axis=0)
    r, v, kk, g, bv, lw, kd, bb, s, x0 = ab_prep(
        p, row(rw_mu[0]), lora.astype(BF16), row(rw_w0[0]), row(rw_a0[0]), rw_g_up[0].astype(BF16),
        row(rw_k_k[0]), row(rw_k_a[0]), row(rw_r_k[0]), bd64, hy_conv_w[0], row(hy_conv_b[0]))
    y0, y1 = rwkv_scan(r, v, kk, lw, kd, bb)
    filt = (hy_f_w1[0], hy_f_b1[0], hy_f_w2[0], hy_f_b2[0], hy_f_w3[0], hy_f_b3[0], hy_f_w4[0], hy_freq[0])
    convs = []
    for length, block in ((SEQ, 0), (CTX_LEN, SEQ // CTX_LEN)):
        wf, wft = dft_matrices(length)
        hspec = filter_spectrum(wf, hyena_filter_halves(length, *filt))
        convs.append(hyena_long_conv(s, wf, wft, hspec, length, block))
    cv = jnp.concatenate(convs, axis=1)
    xs = ab_out(y0, y1, bv, g, x0, cv, s, x, ctx, modv, row(rw_ln_w[0]), row(rw_ln_b[0]), row(hy_skip[0]), bd64,
                ab_w_out[0].astype(BF16))
    xs = mlp_block(xs, modv, norm_mlp[0], mlp_w1[0].astype(BF16), mlp_w2[0].astype(BF16), final_norm, NB, False)

    modv = mod_vectors(1)
    w_in = jnp.pad(dn_w_in[0], ((0, 0), (0, DN_COLS_PAD - DN_COLS))).astype(BF16)
    p = nm_matmul(xs, norm_mix[1], modv, w_in)
    pad_row = lambda a: jnp.pad(a.reshape(1, -1), ((0, 0), (0, 128 - 2 * DN_HEADS)))
    q, k, vv, gb = dn_prep(p, dn_conv_w[0], pad_row(dn_A_log[0]), pad_row(dn_dt_bias[0]), bd128)
    gbt = jnp.swapaxes(gb[:, :, :4 * DN_HEADS].reshape(bsz, NC, CH, 4 * DN_HEADS), 2, 3)
    o0, o1 = dn_scan(q, k, vv, gb, gbt)
    xl = dn_out(o0, o1, p, xs, modv, row(jnp.tile(dn_norm[0], DN_HEADS)), bd128, dn_w_out[0].astype(BF16), NB_LAT)
    return mlp_block(xl, modv, norm_mlp[1], mlp_w1[1].astype(BF16), mlp_w2[1].astype(BF16), final_norm, NB_LAT, True)
```

```python
import functools
import math

import jax
import jax.numpy as jnp
from jax import lax
from jax.experimental import pallas as pl
from jax.experimental.pallas import tpu as pltpu

F32, BF16 = jnp.float32, jnp.bfloat16

D_MODEL = 1024
SEQ = 2048
CTX_LEN = 256
ADA_CHUNKS = 6
EPS = 1e-6
RW_WIDTH = 512
RW_HEAD_DIM = 64
RW_HEADS = 8
RW_COLS = 1792
RW_GN_EPS = 64e-5
HY_WIDTH = 512
HY_COLS = 1536
HY_BANDS = 16
HY_EMB = 33
HY_ORDER = 64
HY_FAST_DECAY = 0.3
HY_SLOW_DECAY = 1.5
HY_TARGET = 1e-2
AB_COLS = RW_COLS + HY_COLS
DN_HEADS = 8
DN_HEAD_DIM = 128
DN_DIM = 1024
DN_COLS = 4 * DN_DIM + 4 * DN_HEADS
DN_COLS_PAD = 4 * DN_DIM + 128

TB = 256
CH = 64
T_ALL = SEQ + CTX_LEN
NB = T_ALL // TB
NB_LAT = SEQ // TB
NC = T_ALL // CH
NC_LAT = SEQ // CH
FB = 256
SCAN_NB = 2
VMEM_LIMIT = 56 * 1024 * 1024

NN = ((1,), (0,))
NT = ((1,), (1,))
TN = ((0,), (0,))


def _dot(a, b, dims=NN):
    return lax.dot_general(a, b, (dims, ((), ())), preferred_element_type=F32)


def _split(x):
    hi = x.astype(BF16)
    lo = (x - hi.astype(F32)).astype(BF16)
    return hi, lo


def _mm(a, b, dims=NN, pa=1, pb=1):
    if a.dtype == BF16:
        a_hi, a_lo, pa = a, None, 1
    elif pa == 2:
        a_hi, a_lo = _split(a)
    else:
        a_hi, a_lo = a.astype(BF16), None
    if b.dtype == BF16:
        b_hi, b_lo, pb = b, None, 1
    elif pb == 2:
        b_hi, b_lo = _split(b)
    else:
        b_hi, b_lo = b.astype(BF16), None
    out = _dot(a_hi, b_hi, dims)
    if pa == 2:
        out = out + _dot(a_lo, b_hi, dims)
    if pb == 2:
        out = out + _dot(a_hi, b_lo, dims)
    return out


def _segsum(x, bd):
    parts = [_dot(x[:, g * 128:(g + 1) * 128].astype(BF16), bd) for g in range(x.shape[1] // 128)]
    return jnp.concatenate(parts, axis=1)


def _sigmoid(x):
    return 1.0 / (1.0 + jnp.exp(-x))


def _softplus(x):
    return jnp.maximum(x, 0.0) + jnp.log(1.0 + jnp.exp(-jnp.abs(x)))


def _rms_mod(x, nw, shift, scale):
    y = x * lax.rsqrt(jnp.mean(x * x, axis=-1, keepdims=True) + EPS)
    return (y * nw) * (1.0 + scale) + shift


def _cparams(*sem):
    return pltpu.CompilerParams(dimension_semantics=sem, vmem_limit_bytes=VMEM_LIMIT)


ADA_TN = 1536


def _ada_kernel(c_ref, w_ref, b_ref, o_ref):
    c = c_ref[...]
    a = c * _sigmoid(c)
    o_ref[0] = _mm(a, w_ref[0], NN, 2, 2) + b_ref[0]


def ada_modulation(cs, ada_w, ada_b):
    depth, d, n = ada_w.shape
    rows = cs.shape[0]
    return pl.pallas_call(
        _ada_kernel,
        grid=(depth, n // ADA_TN),
        in_specs=[pl.BlockSpec((rows, d), lambda l, j: (0, 0)),
                  pl.BlockSpec((1, d, ADA_TN), lambda l, j: (l, 0, j)),
                  pl.BlockSpec((1, 1, ADA_TN), lambda l, j: (l, 0, j))],
        out_specs=pl.BlockSpec((1, rows, ADA_TN), lambda l, j: (l, 0, j)),
        out_shape=jax.ShapeDtypeStruct((depth, rows, n), F32),
        compiler_params=_cparams("parallel", "parallel"),
        name="ada_modulation",
    )(cs, ada_w, ada_b.reshape(depth, 1, n))


def _token_block(x_ref, ctx_ref):
    return jnp.where(pl.program_id(1) == NB - 1, ctx_ref[0], x_ref[0])


def _split_specs(d):
    return [pl.BlockSpec((1, TB, d), lambda i, j: (i, jnp.minimum(j, NB_LAT - 1), 0)),
            pl.BlockSpec((1, TB, d), lambda i, j: (i, 0, 0))]


def _halo_specs(width, n_row_blocks):
    prev = pl.BlockSpec((1, 8, width), lambda i, j: (i, jnp.maximum(j * (TB // 8) - 1, 0), 0))
    nxt = pl.BlockSpec((1, 8, width), lambda i, j: (i, jnp.minimum((j + 1) * (TB // 8), n_row_blocks - 1), 0))
    return [prev, nxt]


def _project_with_neighbours(x, prev_ref, next_ref, nw_ref, mod_ref, w_ref):
    j = pl.program_id(1)
    m = mod_ref[0, 0]
    x_ext = jnp.concatenate([x, prev_ref[0], next_ref[0]], axis=0)
    h = _rms_mod(x_ext, nw_ref[...], m[0:1], m[1:2])
    p_ext = _dot(h.astype(BF16), w_ref[...])
    p = p_ext[:TB]
    pv = jnp.where(jnp.logical_and(j != 0, j != NB - 1), 1.0, 0.0).astype(F32)
    nv = jnp.where(j < NB - 2, 1.0, 0.0).astype(F32)
    prow = p_ext[TB + 7:TB + 8] * pv
    nrow = p_ext[TB + 8:TB + 9] * nv
    rows = lax.broadcasted_iota(jnp.int32, (TB, 1), 0)
    pprev = jnp.where(rows == 0, prow, pltpu.roll(p, 1, 0))
    pnext = jnp.where(rows == TB - 1, nrow, pltpu.roll(p, TB - 1, 0))
    return p, pprev, pnext


def _ab_front_kernel(x_ref, ctx_ref, xp_ref, xn_ref, nw_ref, mod_ref, w_ref,
                     mu_ref, lora_ref, w0_ref, a0_ref, gup_ref, kk_ref, ka_ref, rk_ref, bd_ref, cw_ref, cb_ref,
                     r_ref, v_ref, kkn_ref, g_ref, bv_ref, lw_ref, kd_ref, b_ref, s_ref, x0_ref):
    x, xprev, xnext = _project_with_neighbours(_token_block(x_ref, ctx_ref), xp_ref, xn_ref, nw_ref, mod_ref, w_ref)
    w = RW_WIDTH
    xr = x[:, :RW_COLS]
    ps = xr + mu_ref[...] * (0.5 * (xprev[:, :RW_COLS] + xnext[:, :RW_COLS]) - xr)
    r, k, v = ps[:, :w], ps[:, w:2 * w], ps[:, 2 * w:3 * w]
    slab = ps[:, 3 * w:3 * w + 128]
    gl = ps[:, 3 * w + 128:3 * w + 256]
    lane = lax.broadcasted_iota(jnp.int32, (1, 128), 1)
    z = jnp.where(lane < 64, jnp.tanh(slab), slab)
    lora = _dot(z.astype(BF16), lora_ref[...])
    bd = bd_ref[...]
    kkraw = k * kk_ref[...]
    kk = kkraw * lax.rsqrt(_segsum(kkraw * kkraw, bd) + EPS)
    ksum = jnp.zeros_like(k)
    for d in range(2):
        lw_ref[d, 0] = (-math.exp(-0.5)) * _sigmoid(w0_ref[:, d * w:(d + 1) * w] + lora[:, d * w:(d + 1) * w])
        a = _sigmoid(a0_ref[:, d * w:(d + 1) * w] + lora[:, (2 + d) * w:(3 + d) * w])
        kd = k * (1.0 + (a - 1.0) * ka_ref[...])
        kd_ref[d, 0] = kd.astype(BF16)
        b_ref[d, 0] = (a * kk).astype(BF16)
        ksum = ksum + kd
    r_ref[0] = r.astype(BF16)
    v_ref[0] = v.astype(BF16)
    kkn_ref[0] = kk.astype(BF16)
    g_ref[0] = _dot(_sigmoid(gl).astype(BF16), gup_ref[...]).astype(BF16)
    bv_ref[0] = (_segsum(r * ksum * rk_ref[...], bd) * v).astype(BF16)
    u = x[:, RW_COLS:]
    c = (cw_ref[0:1, :] * xprev[:, RW_COLS:] + cw_ref[1:2, :] * u + cw_ref[2:3, :] * xnext[:, RW_COLS:]
         + cb_ref[...])
    hw = HY_WIDTH
    x0_ref[0] = c[:, :hw].astype(BF16)
    s_ref[0] = (c[:, hw:2 * hw] * c[:, 2 * hw:]).astype(BF16)


def ab_front(x, ctx, nw, modv, w_in, mu, lora, w0, a0, gup, k_k, k_a, r_k, bd64, conv_w, conv_b):
    b, _, d = x.shape
    w = RW_WIDTH
    full = lambda a: pl.BlockSpec(a.shape, lambda i, j: (0,) * a.ndim)
    params = (nw.reshape(1, d), modv, w_in, mu, lora, w0, a0, gup, k_k, k_a, r_k, bd64, conv_w, conv_b)
    param_specs = [full(a) for a in params]
    param_specs[1] = pl.BlockSpec((1, 1, ADA_CHUNKS, d), lambda i, j: (i, j // NB_LAT, 0, 0))
    tok = pl.BlockSpec((1, TB, w), lambda i, j: (i, j, 0))
    tok2 = pl.BlockSpec((2, 1, TB, w), lambda i, j: (0, i, j, 0))
    s1 = jax.ShapeDtypeStruct((b, T_ALL, w), BF16)
    s2 = jax.ShapeDtypeStruct((2, b, T_ALL, w), BF16)
    s2f = jax.ShapeDtypeStruct((2, b, T_ALL, w), F32)
    return pl.pallas_call(
        _ab_front_kernel,
        grid=(b, NB),
        in_specs=_split_specs(d) + _halo_specs(d, SEQ // 8) + param_specs,
        out_specs=[tok, tok, tok, tok, tok, tok2, tok2, tok2, tok, tok],
        out_shape=[s1, s1, s1, s1, s1, s2f, s2, s2, s1, s1],
        compiler_params=_cparams("parallel", "parallel"),
        name="ab_front",
    )(x, ctx, x, x, *params)


def _chunk_masks(reverse):
    c = CH
    ti = lax.broadcasted_iota(jnp.int32, (c, c), 0)
    ji = lax.broadcasted_iota(jnp.int32, (c, c), 1)
    strict = (ji > ti) if reverse else (ji < ti)
    incl = jnp.logical_or(strict, ti == ji)
    return ti, ji, strict, incl


def _mm_many(a_list, b_list, dims=NN):
    ops = [(a.astype(BF16), b.astype(BF16)) for a, b in zip(a_list, b_list)]
    return [_dot(a, b, dims) for a, b in ops]


def _tri_inverse_pairs(n_pairs):
    c = CH
    ti = lax.broadcasted_iota(jnp.int32, (c, 2 * c), 0)
    li = lax.broadcasted_iota(jnp.int32, (c, 2 * c), 1)
    ji = li & (c - 1)
    left = li < c

    def block_diag(x):
        xb = x.astype(BF16)
        zero = jnp.zeros_like(xb)
        return jnp.concatenate([jnp.where(left, xb, zero), jnp.where(left, zero, xb)], axis=0)

    eye = (ti == ji).astype(F32)
    pair = (ti >> 1) == (ji >> 1)
    dinvs = [eye - jnp.where(pair, n, 0.0) for n in n_pairs]
    s = 1
    while (2 << s) <= c:
        off = jnp.logical_and((ti >> (s + 1)) == (ji >> (s + 1)), (ti >> s) != (ji >> s))
        coffs = [block_diag(jnp.where(off, n, 0.0)) for n in n_pairs]
        xs = _mm_many(dinvs, coffs)
        corr = _mm_many(xs, [block_diag(d) for d in dinvs])
        dinvs = [d - t for d, t in zip(dinvs, corr)]
        s += 1
    return dinvs


def _chunk_update_many(s_ref, a, kkt, rt, kh, bh, vh, dec_c, kdim):
    n = len(a)
    c = CH
    left = lax.broadcasted_iota(jnp.int32, (c, 2 * c), 1) < c
    tpairs = _tri_inverse_pairs([jnp.where(left, a[i][:c], a[i + 1][:c]) for i in range(0, n, 2)])
    tms = [tpairs[i // 2][:, :c] if i % 2 == 0 else tpairs[i // 2][:, c:] for i in range(n)]
    a_v = [a[i][:, c:] if i % 2 == 0 else a[i][:, :c] for i in range(n)]
    a_rb = [a[i][c:, :c] if i % 2 == 0 else a[i][c:, c:] for i in range(n)]
    av = _mm_many(a_v, vh)
    tw = _mm_many(tms, [jnp.concatenate([kkt[i], av[i][:c]], axis=1) for i in range(n)])
    states = [s_ref[i] for i in range(n)]
    p = _mm_many([jnp.concatenate([tw[i][:, :kdim], rt[i]], axis=0) for i in range(n)], states, NT)
    u = [p[i][:c] + tw[i][:, kdim:] for i in range(n)]
    au = _mm_many(a_rb, u)
    ds = _mm_many([jnp.concatenate([vh[i], -u[i]], axis=0) for i in range(n)],
                  [jnp.concatenate([kh[i], bh[i]], axis=0) for i in range(n)], TN)
    for i in range(n):
        s_ref[i] = states[i] * dec_c[i] + ds[i]
    return [p[i][c:] + av[i][c:] - au[i] for i in range(n)]


def _chunk_index(reverse):
    if reverse:
        return lambda i: NC - 1 - i
    return lambda i: (i + NC_LAT) % NC


def _rwkv_scan_kernel(rf_ref, vf_ref, kkf_ref, rr_ref, vr_ref, kkr_ref, lwf_ref, kdf_ref, bf_ref,
                      lwr_ref, kdr_ref, br_ref, yf_ref, yr_ref, s_ref):
    c, kdim = CH, RW_HEAD_DIM

    @pl.when(pl.program_id(1) == 0)
    def _():
        s_ref[...] = jnp.zeros_like(s_ref)

    sls = [slice(h * kdim, (h + 1) * kdim) for h in range(RW_HEADS)]
    heads = lambda x: [x[:, sl] for sl in sls]
    ri = lax.broadcasted_iota(jnp.int32, (2 * c, 2 * c), 0)
    ci = lax.broadcasted_iota(jnp.int32, (2 * c, 2 * c), 1)
    rt_, ct_ = ri & (c - 1), ci & (c - 1)
    lhs, rhs, amasks, kkt, rt, kh, bh, vh, dec = ([] for _ in range(9))
    dirs = ((rf_ref, vf_ref, kkf_ref, lwf_ref, kdf_ref, bf_ref), (rr_ref, vr_ref, kkr_ref, lwr_ref, kdr_ref, br_ref))
    for d, (r_ref, v_ref, kk_ref, lw_ref, kd_ref, b_ref) in enumerate(dirs):
        reverse = d == 1
        _, _, _, incl = _chunk_masks(reverse)
        before = (ct_ > rt_) if reverse else (ct_ < rt_)
        amask = jnp.logical_or(before, jnp.logical_and(rt_ == ct_, ri >= c))
        last = 0 if reverse else c - 1
        lws = [lw_ref[0, bi] for bi in range(SCAN_NB)]
        gs = [_mm(incl.astype(BF16), lw, NN, 1, 2) for lw in lws]
        for bi in range(SCAN_NB):
            lw, g = lws[bi], gs[bi]
            gc = g[last:last + 1, :]
            eng = jnp.exp(-g)
            egc = jnp.exp(gc - g)
            kd, bb = kd_ref[0, bi].astype(F32), b_ref[0, bi].astype(F32)
            kkt_all = kk_ref[bi].astype(F32) * jnp.exp(g - lw)
            rt_all = r_ref[bi].astype(F32) * jnp.exp(g)
            kn_all, bn_all = kd * eng, bb * eng
            lhs += [jnp.concatenate([x, y], axis=0) for x, y in zip(heads(kkt_all), heads(rt_all))]
            rhs += [jnp.concatenate([y, x] if h % 2 == 0 else [x, y], axis=0)
                    for h, (x, y) in enumerate(zip(heads(kn_all), heads(bn_all)))]
            amasks += [amask] * RW_HEADS
            kkt += heads(kkt_all)
            rt += heads(rt_all)
            kh += heads(kd * egc)
            bh += heads(bb * egc)
            vh += heads(v_ref[bi])
            dec += heads(jnp.exp(gc))
    a = [jnp.where(m, x, 0.0) for m, x in zip(amasks, _mm_many(lhs, rhs, NT))]
    ys = _chunk_update_many(s_ref, a, kkt, rt, kh, bh, vh, dec, kdim)
    for d, y_ref in enumerate((yf_ref, yr_ref)):
        for bi in range(SCAN_NB):
            for h, sl in enumerate(sls):
                y_ref[bi, :, sl] = ys[(d * SCAN_NB + bi) * RW_HEADS + h].astype(BF16)


def rwkv_scan(r, v, kk, lw, kd, bb):
    b, t, w = r.shape
    specs, specs_d = [], []
    for d in range(2):
        cidx = _chunk_index(d == 1)
        specs += [pl.BlockSpec((SCAN_NB, CH, w), lambda i, j, cidx=cidx: (i, cidx(j), 0))] * 3
        specs_d += [pl.BlockSpec((1, SCAN_NB, CH, w), lambda i, j, cidx=cidx, d=d: (d, i, cidx(j), 0))] * 3
    out = jax.ShapeDtypeStruct((b, t, w), BF16)
    return pl.pallas_call(
        _rwkv_scan_kernel,
        grid=(b // SCAN_NB, NC),
        in_specs=specs + specs_d,
        out_specs=[specs[0], specs[3]],
        out_shape=[out, out],
        scratch_shapes=[pltpu.VMEM((2 * SCAN_NB * RW_HEADS, RW_HEAD_DIM, RW_HEAD_DIM), F32)],
        compiler_params=_cparams("parallel", "arbitrary"),
        name="rwkv_scan",
    )(r, v, kk, r, v, kk, lw, kd, bb, lw, kd, bb)


def _hy_filter_kernel(z_ref, t_ref, w1_ref, b1_ref, w2_ref, b2_ref, w3_ref, b3_ref, w4_ref, fr_ref, dl_ref, o_ref):
    fr = fr_ref[...]
    h = jnp.sin(fr * (_mm(z_ref[...], w1_ref[...], NN, 2, 2) + b1_ref[...]))
    h = jnp.sin(fr * (_mm(h, w2_ref[...], NN, 2, 2) + b2_ref[...]))
    h = jnp.sin(fr * (_mm(h, w3_ref[...], NN, 2, 2) + b3_ref[...]))
    h = _mm(h, w4_ref[...], NN, 2, 2)
    h = h * jnp.exp(-t_ref[...] * dl_ref[...])
    hw = HY_WIDTH
    rows = lax.broadcasted_iota(jnp.int32, (h.shape[0], 1), 0)
    hf = h[:, :hw]
    hb = jnp.where(rows == 0, 0.0, h[:, hw:])
    norm = jnp.sum(jnp.abs(hf), axis=0, keepdims=True) + jnp.sum(jnp.abs(hb), axis=0, keepdims=True)
    o_ref[:, :hw] = (hf + hb) / norm
    o_ref[:, hw:] = (hf - hb) / norm


def hyena_filter_halves(length, w1, b1, w2, b2, w3, b3, w4, freq):
    t = jnp.linspace(0.0, 1.0, length, dtype=F32)[:, None]
    wv = 2 * math.pi * jnp.arange(length, dtype=F32)[:, None] / length
    f = jnp.linspace(1e-4, HY_BANDS - 1, HY_BANDS, dtype=F32)[None, :]
    z = jnp.concatenate([t, jnp.cos(f * wv), -jnp.sin(f * wv)], axis=-1)
    z = jnp.pad(z, ((0, 0), (0, HY_ORDER - HY_EMB)))
    w1p = jnp.pad(w1, ((0, HY_ORDER - HY_EMB), (0, 0)))
    deltas = jnp.abs(jnp.linspace(math.log(HY_TARGET) / HY_SLOW_DECAY, math.log(HY_TARGET) / HY_FAST_DECAY,
                                  HY_WIDTH, dtype=F32))
    dl = jnp.tile(deltas, 2)[None, :]
    row = lambda a: a.reshape(1, -1)
    return pl.pallas_call(
        _hy_filter_kernel,
        out_shape=jax.ShapeDtypeStruct((length, 2 * HY_WIDTH), F32),
        compiler_params=pltpu.CompilerParams(vmem_limit_bytes=VMEM_LIMIT),
        name="hyena_filter",
    )(z, t, w1p, row(b1), w2, row(b2), w3, row(b3), w4, row(freq), dl)


def dft_matrices(length):
    n = 2 * length
    nfb = length // FB
    t = jnp.arange(length, dtype=jnp.int32)
    ang = lambda f: ((f[:, None] * t[None, :]) % n).astype(F32) * (2 * math.pi / n)
    a_hi = ang(jnp.arange(nfb, dtype=jnp.int32) * FB)
    a_lo = ang(jnp.arange(FB, dtype=jnp.int32))
    nyq = jnp.where(t % 2 == 0, 1.0, -1.0).astype(F32)

    def build(ch, sh, cl, sl, first, nyq_b, axis):
        cosm = ch * cl - sh * sl
        msin = -(sh * cl + ch * sl)
        msin = jnp.where(first, nyq_b, msin)
        return jnp.concatenate([cosm, msin], axis=axis).astype(BF16)

    bi = lax.broadcasted_iota
    first = jnp.logical_and(bi(jnp.int32, (nfb, FB, 1), 0) == 0, bi(jnp.int32, (nfb, FB, 1), 1) == 0)
    wf = build(jnp.cos(a_hi)[:, None, :], jnp.sin(a_hi)[:, None, :], jnp.cos(a_lo)[None], jnp.sin(a_lo)[None],
               first, nyq[None, None, :], 1).reshape(n, length)
    first_t = jnp.logical_and(bi(jnp.int32, (1, nfb, FB), 1) == 0, bi(jnp.int32, (1, nfb, FB), 2) == 0)
    wft = build(jnp.cos(a_hi).T[:, :, None], jnp.sin(a_hi).T[:, :, None], jnp.cos(a_lo).T[:, None, :],
                jnp.sin(a_lo).T[:, None, :], first_t, nyq[:, None, None], 2).reshape(length, n)
    return wf, wft


def _spectrum_kernel(w_ref, h_ref, o_ref):
    o_ref[...] = _mm(w_ref[...], h_ref[...], NN, 1, 2)


def filter_spectrum(wf, hpm):
    n, length = wf.shape
    cols = hpm.shape[1]
    return pl.pallas_call(
        _spectrum_kernel,
        grid=(n // (2 * FB),),
        in_specs=[pl.BlockSpec((2 * FB, length), lambda i: (i, 0)),
                  pl.BlockSpec((length, cols), lambda i: (0, 0))],
        out_specs=pl.BlockSpec((2 * FB, cols), lambda i: (i, 0)),
        out_shape=jax.ShapeDtypeStruct((n, cols), F32),
        compiler_params=_cparams("parallel"),
        name="filter_spectrum",
    )(wf, hpm)


def _hy_conv_kernel(s_ref, wf_ref, wft_ref, h_ref, o_ref, *, n_fft):
    fb = pl.program_id(1)

    @pl.when(fb == 0)
    def _():
        o_ref[...] = jnp.zeros_like(o_ref)

    hw = HY_WIDTH
    x = _dot(wf_ref[...], s_ref[0])
    xr, xi = x[:FB], x[FB:]
    hraw = h_ref[...]
    hr, hi, hn = hraw[:FB, :hw], hraw[FB:, hw:], hraw[FB:, :hw]
    fidx = lax.broadcasted_iota(jnp.int32, (FB, 1), 0) + fb * FB
    m = (fidx != 0).astype(F32)
    him = hi * m
    hc = hr * m + hn * (1.0 - m)
    cf = jnp.where(fidx != 0, 2.0 / n_fft, 1.0 / n_fft).astype(F32)
    yr = (xr * hr - xi * him) * cf
    yi = (xr * him + xi * hc) * cf
    y = jnp.concatenate([yr, yi], axis=0).astype(BF16)
    o_ref[0] += _dot(wft_ref[...], y)


def hyena_long_conv(s, wf, wft, hspec, length, block):
    b = s.shape[0]
    hw = HY_WIDTH
    return pl.pallas_call(
        functools.partial(_hy_conv_kernel, n_fft=2 * length),
        grid=(b, length // FB),
        in_specs=[pl.BlockSpec((1, length, hw), lambda i, j: (i, block, 0)),
                  pl.BlockSpec((2 * FB, length), lambda i, j: (j, 0)),
                  pl.BlockSpec((length, 2 * FB), lambda i, j: (0, j)),
                  pl.BlockSpec((2 * FB, 2 * hw), lambda i, j: (j, 0))],
        out_specs=pl.BlockSpec((1, length, hw), lambda i, j: (i, 0, 0)),
        out_shape=jax.ShapeDtypeStruct((b, length, hw), F32),
        compiler_params=_cparams("parallel", "arbitrary"),
        name="hyena_long_conv",
    )(s, wf, wft, hspec)


MLP_HC = 1024


def _mlp_body(x, m, nw_ref, w1_ref, w2_ref, fw_ref, final):
    h = _rms_mod(x, nw_ref[...], m[3:4], m[4:5]).astype(BF16)
    acc = jnp.zeros_like(x)
    for c in range(w1_ref.shape[1] // MLP_HC):
        hid = _dot(h, w1_ref[:, c * MLP_HC:(c + 1) * MLP_HC])
        hid = jnp.square(jnp.maximum(hid, 0.0))
        acc = acc + _dot(hid.astype(BF16), w2_ref[c * MLP_HC:(c + 1) * MLP_HC, :])
    o = x + m[5:6] * acc
    if final:
        o = o * lax.rsqrt(jnp.mean(o * o, axis=-1, keepdims=True) + EPS) * fw_ref[...]
    return o


def _ab_tail_kernel(y0_ref, y1_ref, bv_ref, g_ref, x0_ref, cvl_ref, cvc_ref, s_ref, x_ref, ctx_ref, mod_ref,
                    lnw_ref, lnb_ref, skip_ref, bd_ref, wo_ref, nw_ref, w1_ref, w2_ref, fw_ref, o_ref):
    bd = bd_ref[...]
    inv = 1.0 / RW_HEAD_DIM
    y = y0_ref[0].astype(F32) + y1_ref[0].astype(F32)
    mean = _segsum(y, bd) * inv
    yc = y - mean
    var = _segsum(yc * yc, bd) * inv
    yn = yc * lax.rsqrt(var + RW_GN_EPS)
    a = (yn * lnw_ref[...] + lnb_ref[...] + bv_ref[0].astype(F32)) * g_ref[0].astype(F32)
    s = s_ref[0].astype(F32)
    bh = x0_ref[0].astype(F32) * (_token_block(cvl_ref, cvc_ref) + s * skip_ref[...])
    w = RW_WIDTH
    out = _dot(a.astype(BF16), wo_ref[:w, :]) + _dot(bh.astype(BF16), wo_ref[w:, :])
    m = mod_ref[0, 0]
    xo = _token_block(x_ref, ctx_ref) + m[2:3] * out
    o_ref[0] = _mlp_body(xo, m, nw_ref, w1_ref, w2_ref, fw_ref, False)


def ab_tail(y0, y1, bv, g, x0, cv_lat, cv_ctx, s, x, ctx, modv, ln_w, ln_b, skip, bd64, w_out, nw, w1, w2, fw):
    b, _, d = x.shape
    w = RW_WIDTH
    tok = pl.BlockSpec((1, TB, w), lambda i, j: (i, j, 0))
    full = lambda a: pl.BlockSpec(a.shape, lambda i, j: (0,) * a.ndim)
    params = (ln_w, ln_b, skip, bd64, w_out, nw.reshape(1, d), w1, w2, fw.reshape(1, d))
    return pl.pallas_call(
        _ab_tail_kernel,
        grid=(b, NB),
        in_specs=[tok] * 5 + _split_specs(w) + [tok] + _split_specs(d)
        + [pl.BlockSpec((1, 1, ADA_CHUNKS, d), lambda i, j: (i, j // NB_LAT, 0, 0))] + [full(a) for a in params],
        out_specs=pl.BlockSpec((1, TB, d), lambda i, j: (i, j, 0)),
        out_shape=jax.ShapeDtypeStruct((b, T_ALL, d), F32),
        compiler_params=_cparams("parallel", "parallel"),
        name="ab_tail",
    )(y0, y1, bv, g, x0, cv_lat, cv_ctx, s, x, ctx, modv, *params)


def _dn_front_kernel(x_ref, xp_ref, xn_ref, nw_ref, mod_ref, w_ref, cw_ref, alog_ref, dtb_ref, bd_ref,
                     q_ref, k_ref, v_ref, z_ref, gb_ref):
    p, pprev, pnext = _project_with_neighbours(x_ref[0], xp_ref, xn_ref, nw_ref, mod_ref, w_ref)
    wq = 3 * DN_DIM
    c = cw_ref[0:1, :] * pprev[:, :wq] + cw_ref[1:2, :] * p[:, :wq] + cw_ref[2:3, :] * pnext[:, :wq]
    c = c * _sigmoid(c)
    dd = DN_DIM
    bd = bd_ref[...]
    q, k = c[:, :dd], c[:, dd:2 * dd]
    q_ref[0] = (q * lax.rsqrt(_segsum(q * q, bd) + EPS) * (DN_HEAD_DIM ** -0.5)).astype(BF16)
    k_ref[0] = (k * lax.rsqrt(_segsum(k * k, bd) + EPS)).astype(BF16)
    v_ref[0] = c[:, 2 * dd:].astype(BF16)
    z_ref[0] = p[:, wq:wq + dd].astype(BF16)
    slab = p[:, wq + dd:]
    lane = lax.broadcasted_iota(jnp.int32, (1, 128), 1)
    gdec = -jnp.exp(alog_ref[...]) * _softplus(slab + dtb_ref[...])
    gb_ref[0] = jnp.where(lane < 2 * DN_HEADS, gdec, _sigmoid(slab))


def dn_front(xs, nw, modv, w_in, conv_w, alog_row, dtb_row, bd128):
    b, t, d = xs.shape
    full = lambda a: pl.BlockSpec(a.shape, lambda i, j: (0,) * a.ndim)
    params = (nw.reshape(1, d), modv, w_in, conv_w, alog_row, dtb_row, bd128)
    param_specs = [full(a) for a in params]
    param_specs[1] = pl.BlockSpec((1, 1, ADA_CHUNKS, d), lambda i, j: (i, j // NB_LAT, 0, 0))
    tok = pl.BlockSpec((1, TB, DN_DIM), lambda i, j: (i, j, 0))
    s1 = jax.ShapeDtypeStruct((b, t, DN_DIM), BF16)
    return pl.pallas_call(
        _dn_front_kernel,
        grid=(b, t // TB),
        in_specs=[pl.BlockSpec((1, TB, d), lambda i, j: (i, j, 0))] + _halo_specs(d, t // 8) + param_specs,
        out_specs=[tok, tok, tok, tok, pl.BlockSpec((1, TB, 128), lambda i, j: (i, j, 0))],
        out_shape=[s1, s1, s1, s1, jax.ShapeDtypeStruct((b, t, 128), F32)],
        compiler_params=_cparams("parallel", "parallel"),
        name="dn_front",
    )(xs, xs, xs, *params)


def _dn_scan_kernel(qf_ref, kf_ref, vf_ref, gbf_ref, qr_ref, kr_ref, vr_ref, gbr_ref, yf_ref, yr_ref, s_ref):
    c, kdim = CH, DN_HEAD_DIM

    @pl.when(pl.program_id(1) == 0)
    def _():
        s_ref[...] = jnp.zeros_like(s_ref)

    sls = [slice(h * kdim, (h + 1) * kdim) for h in range(DN_HEADS)]
    dirs = ((qf_ref, kf_ref, vf_ref, gbf_ref), (qr_ref, kr_ref, vr_ref, gbr_ref))
    lhs, rhs, ks, qs, kps, bbs, vhs = ([] for _ in range(7))
    for d, (q_ref, k_ref, v_ref, gb_ref) in enumerate(dirs):
        for bi in range(SCAN_NB):
            gb = gb_ref[bi]
            q, k, v = q_ref[bi].astype(F32), k_ref[bi].astype(F32), v_ref[bi]
            for h, sl in enumerate(sls):
                col = d * DN_HEADS + h
                kp = k[:, sl] * gb[:, 2 * DN_HEADS + col:2 * DN_HEADS + col + 1]
                bb = kp * jnp.exp(gb[:, col:col + 1])
                lhs.append(jnp.concatenate([k[:, sl], q[:, sl]], axis=0))
                rhs.append(jnp.concatenate([bb, kp] if h % 2 == 0 else [kp, bb], axis=0))
                ks.append(k[:, sl])
                qs.append(q[:, sl])
                kps.append(kp)
                bbs.append(bb)
                vhs.append(v[:, sl])
    a_raw = _mm_many(lhs, rhs, NT)
    dfull, kkt, rt, khs, bhs, decs = ([] for _ in range(6))
    ti2 = lax.broadcasted_iota(jnp.int32, (c, 2 * c), 0)
    ji2 = lax.broadcasted_iota(jnp.int32, (c, 2 * c), 1) & (c - 1)
    for d, (_, _, _, gb_ref) in enumerate(dirs):
        reverse = d == 1
        _, _, _, incl = _chunk_masks(reverse)
        strict2 = (ji2 > ti2) if reverse else (ji2 < ti2)
        incl2 = jnp.logical_or(strict2, ti2 == ji2)
        inclt2 = (jnp.logical_or(ji2 < ti2, ti2 == ji2) if reverse else jnp.logical_or(ji2 > ti2, ti2 == ji2))
        last = 0 if reverse else c - 1
        gbs = [gb_ref[bi] for bi in range(SCAN_NB)]
        gcols = [_mm(incl.astype(BF16), gb, NN, 1, 2) for gb in gbs]
        grows = [_mm(gb, inclt2.astype(BF16), TN, 2, 1) for gb in gbs]
        for bi in range(SCAN_NB):
            gb, gcol, grow = gbs[bi], gcols[bi], grows[bi]
            for h in range(DN_HEADS):
                i = (d * SCAN_NB + bi) * DN_HEADS + h
                col = d * DN_HEADS + h
                gt = gcol[:, col:col + 1]
                gxt = gt - gb[:, col:col + 1]
                gj = grow[col:col + 1, :]
                gc = gcol[last:last + 1, col:col + 1]
                d3 = jnp.where(incl2, jnp.exp(jnp.where(incl2, gt - gj, 0.0)), 0.0)
                d1 = jnp.where(strict2, jnp.exp(jnp.where(strict2, gxt - gj, 0.0)), 0.0)
                dfull.append(jnp.concatenate([d1, d3], axis=0))
                egc = jnp.exp(gc - gt)
                kkt.append(ks[i] * jnp.exp(gxt))
                rt.append(qs[i] * jnp.exp(gt))
                khs.append(kps[i] * egc)
                bhs.append(bbs[i] * egc)
                decs.append(jnp.exp(gc))
    a = [x * dm for x, dm in zip(a_raw, dfull)]
    ys = _chunk_update_many(s_ref, a, kkt, rt, khs, bhs, vhs, decs, kdim)
    for d, y_ref in enumerate((yf_ref, yr_ref)):
        for bi in range(SCAN_NB):
            for h, sl in enumerate(sls):
                y_ref[bi, :, sl] = ys[(d * SCAN_NB + bi) * DN_HEADS + h].astype(BF16)


def dn_scan(q, k, v, gb):
    b, t, w = q.shape
    specs = []
    for d in range(2):
        cidx = _chunk_index(d == 1)
        tok = pl.BlockSpec((SCAN_NB, CH, w), lambda i, j, cidx=cidx: (i, cidx(j), 0))
        specs += [tok, tok, tok, pl.BlockSpec((SCAN_NB, CH, 128), lambda i, j, cidx=cidx: (i, cidx(j), 0))]
    out = jax.ShapeDtypeStruct((b, t, w), BF16)
    return pl.pallas_call(
        _dn_scan_kernel,
        grid=(b // SCAN_NB, NC),
        in_specs=specs,
        out_specs=[specs[0], specs[4]],
        out_shape=[out, out],
        scratch_shapes=[pltpu.VMEM((2 * SCAN_NB * DN_HEADS, DN_HEAD_DIM, DN_HEAD_DIM), F32)],
        compiler_params=_cparams("parallel", "arbitrary"),
        name="dn_scan",
    )(q, k, v, gb, q, k, v, gb)


def _dn_tail_kernel(o0_ref, o1_ref, z_ref, x_ref, mod_ref, gnw_ref, bd_ref, wo_ref, nw_ref, w1_ref, w2_ref, fw_ref,
                    o_ref):
    o = o0_ref[0].astype(F32) + o1_ref[0].astype(F32)
    ms = _segsum(o * o, bd_ref[...]) * (1.0 / DN_HEAD_DIM)
    on = o * lax.rsqrt(ms + EPS) * gnw_ref[...]
    z = z_ref[0].astype(F32)
    gated = on * (z * _sigmoid(z))
    m = mod_ref[0, 0]
    xo = x_ref[0] + m[2:3] * _dot(gated.astype(BF16), wo_ref[...])
    o_ref[0] = _mlp_body(xo, m, nw_ref, w1_ref, w2_ref, fw_ref, True)


def dn_tail(o0, o1, z, xs, modv, gnw_tiled, bd128, w_out, nw, w1, w2, fw):
    b, _, d = xs.shape
    tok = pl.BlockSpec((1, TB, DN_DIM), lambda i, j: (i, j, 0))
    full = lambda a: pl.BlockSpec(a.shape, lambda i, j: (0,) * a.ndim)
    params = (gnw_tiled, bd128, w_out, nw.reshape(1, d), w1, w2, fw.reshape(1, d))
    return pl.pallas_call(
        _dn_tail_kernel,
        grid=(b, NB_LAT),
        in_specs=[tok, tok, tok, pl.BlockSpec((1, TB, d), lambda i, j: (i, j, 0)),
                  pl.BlockSpec((1, 1, ADA_CHUNKS, d), lambda i, j: (i, 0, 0, 0))] + [full(a) for a in params],
        out_specs=pl.BlockSpec((1, TB, d), lambda i, j: (i, j, 0)),
        out_shape=jax.ShapeDtypeStruct((b, SEQ, d), F32),
        compiler_params=_cparams("parallel", "parallel"),
        name="dn_tail",
    )(o0, o1, z, xs, modv, *params)


def _block_diag_ones(head):
    i = jnp.arange(128) // head
    return (i[:, None] == i[None, :]).astype(BF16)


def kernel(x, c, ctx, c_ctx, ada_w, ada_b, norm_mix, norm_mlp, mlp_w1, mlp_w2, final_norm, ab_w_in, ab_w_out, rw_mu, rw_w0, rw_w_up, rw_a0, rw_a_up, rw_g_up, rw_k_k, rw_k_a, rw_r_k, rw_ln_w, rw_ln_b, hy_conv_w, hy_conv_b, hy_f_w1, hy_f_b1, hy_f_w2, hy_f_b2, hy_f_w3, hy_f_b3, hy_f_w4, hy_freq, hy_skip, dn_w_in, dn_conv_w, dn_A_log, dn_dt_bias, dn_norm, dn_w_out):
    bsz = x.shape[0]
    d = D_MODEL
    w = RW_WIDTH
    row = lambda a: a.reshape(1, -1)

    cs = jnp.concatenate([c, c_ctx[None, :], jnp.zeros((16 - bsz - 1, d), F32)], axis=0)
    mod = ada_modulation(cs, ada_w, ada_b)

    def mod_vectors(layer):
        lat = mod[layer, :bsz].reshape(bsz, 1, ADA_CHUNKS, d)
        cx = jnp.broadcast_to(mod[layer, bsz].reshape(1, 1, ADA_CHUNKS, d), (bsz, 1, ADA_CHUNKS, d))
        return jnp.concatenate([lat, cx], axis=1)

    bd64 = _block_diag_ones(RW_HEAD_DIM)
    bd128 = _block_diag_ones(DN_HEAD_DIM)

    modv = mod_vectors(0)
    zeros = jnp.zeros((64, 2 * w), F32)
    lora = jnp.concatenate([jnp.concatenate([rw_w_up[0, 0], rw_w_up[0, 1], zeros], axis=1),
                            jnp.concatenate([zeros, rw_a_up[0, 0], rw_a_up[0, 1]], axis=1)], axis=0)
    r, v, kk, g, bv, lw, kd, bb, s, x0 = ab_front(
        x, ctx, norm_mix[0], modv, ab_w_in[0].astype(BF16), row(rw_mu[0]), lora.astype(BF16), row(rw_w0[0]),
        row(rw_a0[0]), rw_g_up[0].astype(BF16), row(rw_k_k[0]), row(rw_k_a[0]), row(rw_r_k[0]), bd64,
        hy_conv_w[0], row(hy_conv_b[0]))
    y0, y1 = rwkv_scan(r, v, kk, lw, kd, bb)
    filt = (hy_f_w1[0], hy_f_b1[0], hy_f_w2[0], hy_f_b2[0], hy_f_w3[0], hy_f_b3[0], hy_f_w4[0], hy_freq[0])
    convs = []
    for length, block in ((SEQ, 0), (CTX_LEN, SEQ // CTX_LEN)):
        wf, wft = dft_matrices(length)
        hspec = filter_spectrum(wf, hyena_filter_halves(length, *filt))
        convs.append(hyena_long_conv(s, wf, wft, hspec, length, block))
    xs = ab_tail(y0, y1, bv, g, x0, convs[0], convs[1], s, x, ctx, modv, row(rw_ln_w[0]), row(rw_ln_b[0]),
                 row(hy_skip[0]), bd64, ab_w_out[0].astype(BF16), norm_mlp[0], mlp_w1[0].astype(BF16),
                 mlp_w2[0].astype(BF16), final_norm)

    modv = mod_vectors(1)
    w_in = jnp.pad(dn_w_in[0], ((0, 0), (0, DN_COLS_PAD - DN_COLS))).astype(BF16)
    pad_row = lambda a: jnp.pad(a.reshape(1, -1), ((0, 0), (0, 128 - 2 * DN_HEADS)))
    q, k, vv, z, gb = dn_front(xs, norm_mix[1], modv, w_in, dn_conv_w[0], pad_row(dn_A_log[0]),
                               pad_row(dn_dt_bias[0]), bd128)
    o0, o1 = dn_scan(q, k, vv, gb)
    return dn_tail(o0, o1, z, xs, modv, row(jnp.tile(dn_norm[0], DN_HEADS)), bd128, dn_w_out[0].astype(BF16),
                   norm_mlp[1], mlp_w1[1].astype(BF16), mlp_w2[1].astype(BF16), final_norm)
```

```python
import functools
import math

import jax
import jax.numpy as jnp
from jax import lax
from jax.experimental import pallas as pl
from jax.experimental.pallas import tpu as pltpu

F32, BF16 = jnp.float32, jnp.bfloat16

D_MODEL = 1024
SEQ = 2048
CTX_LEN = 256
ADA_CHUNKS = 6
EPS = 1e-6
RW_WIDTH = 512
RW_HEAD_DIM = 64
RW_HEADS = 8
RW_COLS = 1792
RW_GN_EPS = 64e-5
HY_WIDTH = 512
HY_COLS = 1536
HY_BANDS = 16
HY_EMB = 33
HY_ORDER = 64
HY_FAST_DECAY = 0.3
HY_SLOW_DECAY = 1.5
HY_TARGET = 1e-2
AB_COLS = RW_COLS + HY_COLS
DN_HEADS = 8
DN_HEAD_DIM = 128
DN_DIM = 1024
DN_COLS = 4 * DN_DIM + 4 * DN_HEADS
DN_COLS_PAD = 4 * DN_DIM + 128

TB = 256
CH = 64
T_ALL = SEQ + CTX_LEN
NB = T_ALL // TB
NB_LAT = SEQ // TB
NC = T_ALL // CH
NC_LAT = SEQ // CH
FB = 256
CONV_NB = 2
SCAN_NB = 4
VMEM_LIMIT = 56 * 1024 * 1024

NN = ((1,), (0,))
NT = ((1,), (1,))
TN = ((0,), (0,))


def _dot(a, b, dims=NN):
    return lax.dot_general(a, b, (dims, ((), ())), preferred_element_type=F32)


def _split(x):
    hi = x.astype(BF16)
    lo = (x - hi.astype(F32)).astype(BF16)
    return hi, lo


def _mm(a, b, dims=NN, pa=1, pb=1):
    if a.dtype == BF16:
        a_hi, a_lo, pa = a, None, 1
    elif pa == 2:
        a_hi, a_lo = _split(a)
    else:
        a_hi, a_lo = a.astype(BF16), None
    if b.dtype == BF16:
        b_hi, b_lo, pb = b, None, 1
    elif pb == 2:
        b_hi, b_lo = _split(b)
    else:
        b_hi, b_lo = b.astype(BF16), None
    out = _dot(a_hi, b_hi, dims)
    if pa == 2:
        out = out + _dot(a_lo, b_hi, dims)
    if pb == 2:
        out = out + _dot(a_hi, b_lo, dims)
    return out


def _segsum(x, bd):
    parts = [_dot(x[:, g * 128:(g + 1) * 128].astype(BF16), bd) for g in range(x.shape[1] // 128)]
    return jnp.concatenate(parts, axis=1)


def _sigmoid(x):
    return 1.0 / (1.0 + jnp.exp(-x))


def _softplus(x):
    return jnp.maximum(x, 0.0) + jnp.log(1.0 + jnp.exp(-jnp.abs(x)))


def _rms_mod(x, nw, shift, scale):
    y = x * lax.rsqrt(jnp.mean(x * x, axis=-1, keepdims=True) + EPS)
    return (y * nw) * (1.0 + scale) + shift


def _cparams(*sem):
    return pltpu.CompilerParams(dimension_semantics=sem, vmem_limit_bytes=VMEM_LIMIT)


ADA_TN = 1536


def _ada_kernel(c_ref, w_ref, b_ref, o_ref):
    c = c_ref[...]
    a = c * _sigmoid(c)
    o_ref[0] = _mm(a, w_ref[0], NN, 2, 2) + b_ref[0]


def ada_modulation(cs, ada_w, ada_b):
    depth, d, n = ada_w.shape
    rows = cs.shape[0]
    return pl.pallas_call(
        _ada_kernel,
        grid=(depth, n // ADA_TN),
        in_specs=[pl.BlockSpec((rows, d), lambda l, j: (0, 0)),
                  pl.BlockSpec((1, d, ADA_TN), lambda l, j: (l, 0, j)),
                  pl.BlockSpec((1, 1, ADA_TN), lambda l, j: (l, 0, j))],
        out_specs=pl.BlockSpec((1, rows, ADA_TN), lambda l, j: (l, 0, j)),
        out_shape=jax.ShapeDtypeStruct((depth, rows, n), F32),
        compiler_params=_cparams("parallel", "parallel"),
        name="ada_modulation",
    )(cs, ada_w, ada_b.reshape(depth, 1, n))


def _token_block(x_ref, ctx_ref):
    return jnp.where(pl.program_id(1) == NB - 1, ctx_ref[0], x_ref[0])


def _split_specs(d):
    return [pl.BlockSpec((1, TB, d), lambda i, j: (i, jnp.minimum(j, NB_LAT - 1), 0)),
            pl.BlockSpec((1, TB, d), lambda i, j: (i, 0, 0))]


def _halo_specs(width, n_row_blocks):
    prev = pl.BlockSpec((1, 8, width), lambda i, j: (i, jnp.maximum(j * (TB // 8) - 1, 0), 0))
    nxt = pl.BlockSpec((1, 8, width), lambda i, j: (i, jnp.minimum((j + 1) * (TB // 8), n_row_blocks - 1), 0))
    return [prev, nxt]


def _project_with_neighbours(x, prev_ref, next_ref, nw_ref, mod_ref, w_ref):
    j = pl.program_id(1)
    m = mod_ref[0, 0]
    x_ext = jnp.concatenate([x, prev_ref[0], next_ref[0]], axis=0)
    h = _rms_mod(x_ext, nw_ref[...], m[0:1], m[1:2])
    p_ext = _dot(h.astype(BF16), w_ref[...])
    p = p_ext[:TB]
    pv = jnp.where(jnp.logical_and(j != 0, j != NB - 1), 1.0, 0.0).astype(F32)
    nv = jnp.where(j < NB - 2, 1.0, 0.0).astype(F32)
    prow = p_ext[TB + 7:TB + 8] * pv
    nrow = p_ext[TB + 8:TB + 9] * nv
    rows = lax.broadcasted_iota(jnp.int32, (8, 1), 0)
    pprev = pltpu.roll(p, 1, 0)
    pprev = jnp.concatenate([jnp.where(rows == 0, prow, pprev[:8]), pprev[8:]], axis=0)
    pnext = pltpu.roll(p, TB - 1, 0)
    pnext = jnp.concatenate([pnext[:TB - 8], jnp.where(rows == 7, nrow, pnext[TB - 8:])], axis=0)
    return p, pprev, pnext


def _ab_front_kernel(x_ref, ctx_ref, xp_ref, xn_ref, nw_ref, mod_ref, w_ref,
                     mu_ref, lora_ref, w0_ref, a0_ref, gup_ref, kk_ref, ka_ref, rk_ref, bd_ref, cw_ref, cb_ref,
                     r_ref, v_ref, kkn_ref, g_ref, bv_ref, lw_ref, kd_ref, b_ref, s_ref, x0_ref):
    x, xprev, xnext = _project_with_neighbours(_token_block(x_ref, ctx_ref), xp_ref, xn_ref, nw_ref, mod_ref, w_ref)
    w = RW_WIDTH
    xr = x[:, :RW_COLS]
    ps = xr + mu_ref[...] * (0.5 * (xprev[:, :RW_COLS] + xnext[:, :RW_COLS]) - xr)
    r, k, v = ps[:, :w], ps[:, w:2 * w], ps[:, 2 * w:3 * w]
    slab = ps[:, 3 * w:3 * w + 128]
    gl = ps[:, 3 * w + 128:3 * w + 256]
    lane = lax.broadcasted_iota(jnp.int32, (1, 128), 1)
    z = jnp.where(lane < 64, jnp.tanh(slab), slab)
    lora = _dot(z.astype(BF16), lora_ref[...])
    bd = bd_ref[...]
    kkraw = k * kk_ref[...]
    kk = kkraw * lax.rsqrt(_segsum(kkraw * kkraw, bd) + EPS)
    ksum = jnp.zeros_like(k)
    for d in range(2):
        lw_ref[d, 0] = (-math.exp(-0.5)) * _sigmoid(w0_ref[:, d * w:(d + 1) * w] + lora[:, d * w:(d + 1) * w])
        a = _sigmoid(a0_ref[:, d * w:(d + 1) * w] + lora[:, (2 + d) * w:(3 + d) * w])
        kd = k * (1.0 + (a - 1.0) * ka_ref[...])
        kd_ref[d, 0] = kd.astype(BF16)
        b_ref[d, 0] = (a * kk).astype(BF16)
        ksum = ksum + kd
    r_ref[0] = r.astype(BF16)
    v_ref[0] = v.astype(BF16)
    kkn_ref[0] = kk.astype(BF16)
    g_ref[0] = _dot(_sigmoid(gl).astype(BF16), gup_ref[...]).astype(BF16)
    bv_ref[0] = (_segsum(r * ksum * rk_ref[...], bd) * v).astype(BF16)
    u = x[:, RW_COLS:]
    c = (cw_ref[0:1, :] * xprev[:, RW_COLS:] + cw_ref[1:2, :] * u + cw_ref[2:3, :] * xnext[:, RW_COLS:]
         + cb_ref[...])
    hw = HY_WIDTH
    x0_ref[0] = c[:, :hw].astype(BF16)
    s_ref[0] = (c[:, hw:2 * hw] * c[:, 2 * hw:]).astype(BF16)


def ab_front(x, ctx, nw, modv, w_in, mu, lora, w0, a0, gup, k_k, k_a, r_k, bd64, conv_w, conv_b):
    b, _, d = x.shape
    w = RW_WIDTH
    full = lambda a: pl.BlockSpec(a.shape, lambda i, j: (0,) * a.ndim)
    params = (nw.reshape(1, d), modv, w_in, mu, lora, w0, a0, gup, k_k, k_a, r_k, bd64, conv_w, conv_b)
    param_specs = [full(a) for a in params]
    param_specs[1] = pl.BlockSpec((1, 1, ADA_CHUNKS, d), lambda i, j: (i, j // NB_LAT, 0, 0))
    tok = pl.BlockSpec((1, TB, w), lambda i, j: (i, j, 0))
    tok2 = pl.BlockSpec((2, 1, TB, w), lambda i, j: (0, i, j, 0))
    s1 = jax.ShapeDtypeStruct((b, T_ALL, w), BF16)
    s2 = jax.ShapeDtypeStruct((2, b, T_ALL, w), BF16)
    s2f = jax.ShapeDtypeStruct((2, b, T_ALL, w), F32)
    return pl.pallas_call(
        _ab_front_kernel,
        grid=(b, NB),
        in_specs=_split_specs(d) + _halo_specs(d, SEQ // 8) + param_specs,
        out_specs=[tok, tok, tok, tok, tok, tok2, tok2, tok2, tok, tok],
        out_shape=[s1, s1, s1, s1, s1, s2f, s2, s2, s1, s1],
        compiler_params=_cparams("parallel", "parallel"),
        name="ab_front",
    )(x, ctx, x, x, *params)


def _chunk_masks(reverse):
    c = CH
    ti = lax.broadcasted_iota(jnp.int32, (c, c), 0)
    ji = lax.broadcasted_iota(jnp.int32, (c, c), 1)
    strict = (ji > ti) if reverse else (ji < ti)
    incl = jnp.logical_or(strict, ti == ji)
    return ti, ji, strict, incl


def _mm_many(a_list, b_list, dims=NN):
    ops = [(a.astype(BF16), b.astype(BF16)) for a, b in zip(a_list, b_list)]
    return [_dot(a, b, dims) for a, b in ops]


def _tri_inverse_pairs(n_pairs):
    c = CH
    ti = lax.broadcasted_iota(jnp.int32, (c, 2 * c), 0)
    li = lax.broadcasted_iota(jnp.int32, (c, 2 * c), 1)
    ji = li & (c - 1)
    left = li < c

    def block_diag(x):
        xb = x.astype(BF16)
        zero = jnp.zeros_like(xb)
        return jnp.concatenate([jnp.where(left, xb, zero), jnp.where(left, zero, xb)], axis=0)

    eye = (ti == ji).astype(F32)
    pair = (ti >> 1) == (ji >> 1)
    dinvs = [eye - jnp.where(pair, n, 0.0) for n in n_pairs]
    s = 1
    while (2 << s) <= c:
        off = jnp.logical_and((ti >> (s + 1)) == (ji >> (s + 1)), (ti >> s) != (ji >> s))
        coffs = [block_diag(jnp.where(off, n, 0.0)) for n in n_pairs]
        xs = _mm_many(dinvs, coffs)
        corr = _mm_many(xs, [block_diag(d) for d in dinvs])
        dinvs = [d - t for d, t in zip(dinvs, corr)]
        s += 1
    return dinvs


def _chunk_update_many(s_ref, a, kkt, rt, kh, bh, vh, dec_c, kdim):
    n = len(a)
    c = CH
    left = lax.broadcasted_iota(jnp.int32, (c, 2 * c), 1) < c
    tpairs = _tri_inverse_pairs([jnp.where(left, a[i][:c], a[i + 1][:c]) for i in range(0, n, 2)])
    tms = [tpairs[i // 2][:, :c] if i % 2 == 0 else tpairs[i // 2][:, c:] for i in range(n)]
    a_v = [a[i][:, c:] if i % 2 == 0 else a[i][:, :c] for i in range(n)]
    a_rb = [a[i][c:, :c] if i % 2 == 0 else a[i][c:, c:] for i in range(n)]
    av = _mm_many(a_v, vh)
    tw = _mm_many(tms, [jnp.concatenate([kkt[i], av[i][:c]], axis=1) for i in range(n)])
    states = [s_ref[i] for i in range(n)]
    p = _mm_many([jnp.concatenate([tw[i][:, :kdim], rt[i]], axis=0) for i in range(n)], states, NT)
    u = [p[i][:c] + tw[i][:, kdim:] for i in range(n)]
    au = _mm_many(a_rb, u)
    ds = _mm_many([jnp.concatenate([vh[i], -u[i]], axis=0) for i in range(n)],
                  [jnp.concatenate([kh[i], bh[i]], axis=0) for i in range(n)], TN)
    for i in range(n):
        s_ref[i] = states[i] * dec_c[i] + ds[i]
    return [p[i][c:] + av[i][c:] - au[i] for i in range(n)]


def _pair_update_many(s_ref, a, kkt, rt, kh, bh, vh, dec_c):
    n = len(kkt)
    c = CH
    left = lax.broadcasted_iota(jnp.int32, (1, 2 * c), 1) < c

    def diag(x, anti=False):
        xb = x.astype(BF16)
        zero = jnp.zeros_like(xb)
        top, bottom = jnp.where(left, xb, zero), jnp.where(left, zero, xb)
        return jnp.concatenate([bottom, top] if anti else [top, bottom], axis=0)

    tps = _tri_inverse_pairs([jnp.where(left, a[2 * i + 1][:c], a[2 * i][:c]) for i in range(n)])
    av = _mm_many([jnp.where(left, a[2 * i], a[2 * i + 1]) for i in range(n)], [diag(x) for x in vh])
    tw = _mm_many(tps, [jnp.concatenate([diag(kkt[i], True), diag(av[i][:c], True)], axis=1) for i in range(n)])
    states = [s_ref[i] for i in range(n)]
    p = _mm_many([jnp.concatenate([tw[i][:, :2 * c], rt[i]], axis=0) for i in range(n)],
                 [diag(x) for x in states], NT)
    u = [p[i][:c] + tw[i][:, 2 * c:] for i in range(n)]
    au = _mm_many([jnp.where(left, a[2 * i + 1][c:], a[2 * i][c:]) for i in range(n)], [diag(x, True) for x in u])
    ds = _mm_many([jnp.concatenate([vh[i], -u[i]], axis=0) for i in range(n)],
                  [jnp.concatenate([kh[i], bh[i]], axis=0) for i in range(n)], TN)
    for i in range(n):
        s_ref[i] = states[i] * dec_c[i] + jnp.where(left, ds[i][:c], ds[i][c:])
    return [p[i][c:] + av[i][c:] - au[i] for i in range(n)]


def _chunk_index(reverse):
    if reverse:
        return lambda i: NC - 1 - i
    return lambda i: (i + NC_LAT) % NC


def _rwkv_scan_kernel(rf_ref, vf_ref, kkf_ref, rr_ref, vr_ref, kkr_ref, lwf_ref, kdf_ref, bf_ref,
                      lwr_ref, kdr_ref, br_ref, yf_ref, yr_ref, s_ref):
    c, kdim = CH, RW_HEAD_DIM

    @pl.when(pl.program_id(1) == 0)
    def _():
        s_ref[...] = jnp.zeros_like(s_ref)

    sls = [slice(m * 2 * kdim, (m + 1) * 2 * kdim) for m in range(RW_HEADS // 2)]
    slabs = lambda x: [x[:, sl] for sl in sls]
    left = lax.broadcasted_iota(jnp.int32, (1, 2 * kdim), 1) < kdim
    ri = lax.broadcasted_iota(jnp.int32, (2 * c, 2 * c), 0)
    ci = lax.broadcasted_iota(jnp.int32, (2 * c, 2 * c), 1)
    rt_, ct_ = ri & (c - 1), ci & (c - 1)
    lhs, rhs, amasks, kkt, rt, kh, bh, vh, dec = ([] for _ in range(9))
    dirs = ((rf_ref, vf_ref, kkf_ref, lwf_ref, kdf_ref, bf_ref), (rr_ref, vr_ref, kkr_ref, lwr_ref, kdr_ref, br_ref))
    for d, (r_ref, v_ref, kk_ref, lw_ref, kd_ref, b_ref) in enumerate(dirs):
        reverse = d == 1
        _, _, _, incl = _chunk_masks(reverse)
        before = (ct_ > rt_) if reverse else (ct_ < rt_)
        amask = jnp.logical_or(before, jnp.logical_and(rt_ == ct_, ri >= c))
        last = 0 if reverse else c - 1
        lws = [lw_ref[0, bi] for bi in range(SCAN_NB)]
        gs = [_mm(incl.astype(BF16), lw, NN, 1, 2) for lw in lws]
        for bi in range(SCAN_NB):
            lw, g = lws[bi], gs[bi]
            gc = g[last:last + 1, :]
            eng = jnp.exp(-g)
            egc = jnp.exp(gc - g)
            kd, bb = kd_ref[0, bi].astype(F32), b_ref[0, bi].astype(F32)
            kkt_all = kk_ref[bi].astype(F32) * jnp.exp(g - lw)
            rt_all = r_ref[bi].astype(F32) * jnp.exp(g)
            for x, y, kn, bn in zip(slabs(kkt_all), slabs(rt_all), slabs(kd * eng), slabs(bb * eng)):
                both = jnp.concatenate([x, y], axis=0)
                lhs += [jnp.where(left, both, 0.0), jnp.where(left, 0.0, both)]
                rhs += [jnp.concatenate([kn, bn], axis=0), jnp.concatenate([bn, kn], axis=0)]
            amasks += [amask] * RW_HEADS
            kkt += slabs(kkt_all)
            rt += slabs(rt_all)
            kh += slabs(kd * egc)
            bh += slabs(bb * egc)
            vh += slabs(v_ref[bi])
            dec += slabs(jnp.exp(gc))
    a = [jnp.where(m, x, 0.0) for m, x in zip(amasks, _mm_many(lhs, rhs, NT))]
    ys = _pair_update_many(s_ref, a, kkt, rt, kh, bh, vh, dec)
    for d, y_ref in enumerate((yf_ref, yr_ref)):
        for bi in range(SCAN_NB):
            for m, sl in enumerate(sls):
                y_ref[bi, :, sl] = ys[(d * SCAN_NB + bi) * len(sls) + m].astype(BF16)


def rwkv_scan(r, v, kk, lw, kd, bb):
    b, t, w = r.shape
    specs, specs_d = [], []
    for d in range(2):
        cidx = _chunk_index(d == 1)
        specs += [pl.BlockSpec((SCAN_NB, CH, w), lambda i, j, cidx=cidx: (i, cidx(j), 0))] * 3
        specs_d += [pl.BlockSpec((1, SCAN_NB, CH, w), lambda i, j, cidx=cidx, d=d: (d, i, cidx(j), 0))] * 3
    out = jax.ShapeDtypeStruct((b, t, w), BF16)
    return pl.pallas_call(
        _rwkv_scan_kernel,
        grid=(b // SCAN_NB, NC),
        in_specs=specs + specs_d,
        out_specs=[specs[0], specs[3]],
        out_shape=[out, out],
        scratch_shapes=[pltpu.VMEM((SCAN_NB * RW_HEADS, RW_HEAD_DIM, 2 * RW_HEAD_DIM), F32)],
        compiler_params=_cparams("parallel", "arbitrary"),
        name="rwkv_scan",
    )(r, v, kk, r, v, kk, lw, kd, bb, lw, kd, bb)


def _hy_filter_kernel(z_ref, t_ref, w1_ref, b1_ref, w2_ref, b2_ref, w3_ref, b3_ref, w4_ref, fr_ref, dl_ref, o_ref):
    fr = fr_ref[...]
    h = jnp.sin(fr * (_mm(z_ref[...], w1_ref[...], NN, 2, 2) + b1_ref[...]))
    h = jnp.sin(fr * (_mm(h, w2_ref[...], NN, 2, 2) + b2_ref[...]))
    h = jnp.sin(fr * (_mm(h, w3_ref[...], NN, 2, 2) + b3_ref[...]))
    h = _mm(h, w4_ref[...], NN, 2, 2)
    h = h * jnp.exp(-t_ref[...] * dl_ref[...])
    hw = HY_WIDTH
    rows = lax.broadcasted_iota(jnp.int32, (h.shape[0], 1), 0)
    hf = h[:, :hw]
    hb = jnp.where(rows == 0, 0.0, h[:, hw:])
    norm = jnp.sum(jnp.abs(hf), axis=0, keepdims=True) + jnp.sum(jnp.abs(hb), axis=0, keepdims=True)
    o_ref[:, :hw] = (hf + hb) / norm
    o_ref[:, hw:] = (hf - hb) / norm


def hyena_filter_halves(length, w1, b1, w2, b2, w3, b3, w4, freq):
    t = jnp.linspace(0.0, 1.0, length, dtype=F32)[:, None]
    wv = 2 * math.pi * jnp.arange(length, dtype=F32)[:, None] / length
    f = jnp.linspace(1e-4, HY_BANDS - 1, HY_BANDS, dtype=F32)[None, :]
    z = jnp.concatenate([t, jnp.cos(f * wv), -jnp.sin(f * wv)], axis=-1)
    z = jnp.pad(z, ((0, 0), (0, HY_ORDER - HY_EMB)))
    w1p = jnp.pad(w1, ((0, HY_ORDER - HY_EMB), (0, 0)))
    deltas = jnp.abs(jnp.linspace(math.log(HY_TARGET) / HY_SLOW_DECAY, math.log(HY_TARGET) / HY_FAST_DECAY,
                                  HY_WIDTH, dtype=F32))
    dl = jnp.tile(deltas, 2)[None, :]
    row = lambda a: a.reshape(1, -1)
    return pl.pallas_call(
        _hy_filter_kernel,
        out_shape=jax.ShapeDtypeStruct((length, 2 * HY_WIDTH), F32),
        compiler_params=pltpu.CompilerParams(vmem_limit_bytes=VMEM_LIMIT),
        name="hyena_filter",
    )(z, t, w1p, row(b1), w2, row(b2), w3, row(b3), w4, row(freq), dl)


def dft_matrices(length):
    n = 2 * length
    nfb = length // FB
    t = jnp.arange(length, dtype=jnp.int32)
    ang = lambda f: ((f[:, None] * t[None, :]) % n).astype(F32) * (2 * math.pi / n)
    a_hi = ang(jnp.arange(nfb, dtype=jnp.int32) * FB)
    a_lo = ang(jnp.arange(FB, dtype=jnp.int32))
    nyq = jnp.where(t % 2 == 0, 1.0, -1.0).astype(F32)

    def build(ch, sh, cl, sl, first, nyq_b, axis):
        cosm = ch * cl - sh * sl
        msin = -(sh * cl + ch * sl)
        msin = jnp.where(first, nyq_b, msin)
        return jnp.concatenate([cosm, msin], axis=axis).astype(BF16)

    bi = lax.broadcasted_iota
    first = jnp.logical_and(bi(jnp.int32, (nfb, FB, 1), 0) == 0, bi(jnp.int32, (nfb, FB, 1), 1) == 0)
    wf = build(jnp.cos(a_hi)[:, None, :], jnp.sin(a_hi)[:, None, :], jnp.cos(a_lo)[None], jnp.sin(a_lo)[None],
               first, nyq[None, None, :], 1).reshape(n, length)
    first_t = jnp.logical_and(bi(jnp.int32, (1, nfb, FB), 1) == 0, bi(jnp.int32, (1, nfb, FB), 2) == 0)
    wft = build(jnp.cos(a_hi).T[:, :, None], jnp.sin(a_hi).T[:, :, None], jnp.cos(a_lo).T[:, None, :],
                jnp.sin(a_lo).T[:, None, :], first_t, nyq[:, None, None], 2).reshape(length, n)
    return wf, wft


def _spectrum_kernel(w_ref, h_ref, o_ref):
    o_ref[...] = _dot(w_ref[...], h_ref[...].astype(BF16))


def filter_spectrum(wf, hpm):
    n, length = wf.shape
    cols = hpm.shape[1]
    return pl.pallas_call(
        _spectrum_kernel,
        grid=(n // (2 * FB),),
        in_specs=[pl.BlockSpec((2 * FB, length), lambda i: (i, 0)),
                  pl.BlockSpec((length, cols), lambda i: (0, 0))],
        out_specs=pl.BlockSpec((2 * FB, cols), lambda i: (i, 0)),
        out_shape=jax.ShapeDtypeStruct((n, cols), F32),
        compiler_params=_cparams("parallel"),
        name="filter_spectrum",
    )(wf, hpm)


def _hy_conv_kernel(s_ref, wf_ref, wft_ref, h_ref, o_ref, *, n_fft):
    fb = pl.program_id(1)

    @pl.when(fb == 0)
    def _():
        o_ref[...] = jnp.zeros_like(o_ref)

    hw = HY_WIDTH
    hraw = h_ref[...]
    hr, hi, hn = hraw[:FB, :hw], hraw[FB:, hw:], hraw[FB:, :hw]
    fidx = lax.broadcasted_iota(jnp.int32, (FB, 1), 0) + fb * FB
    m = (fidx != 0).astype(F32)
    cf = jnp.where(fidx != 0, 2.0 / n_fft, 1.0 / n_fft).astype(F32)
    hr_c = hr * cf
    him = hi * m * cf
    hc = (hr * m + hn * (1.0 - m)) * cf
    for bi in range(s_ref.shape[0]):
        x = _dot(wf_ref[...], s_ref[bi])
        xr, xi = x[:FB], x[FB:]
        y = jnp.concatenate([xr * hr_c - xi * him, xr * him + xi * hc], axis=0).astype(BF16)
        o_ref[bi] += _dot(wft_ref[...], y)


def hyena_long_conv(s, wf, wft, hspec, length, block):
    b = s.shape[0]
    hw = HY_WIDTH
    return pl.pallas_call(
        functools.partial(_hy_conv_kernel, n_fft=2 * length),
        grid=(b // CONV_NB, length // FB),
        in_specs=[pl.BlockSpec((CONV_NB, length, hw), lambda i, j: (i, block, 0)),
                  pl.BlockSpec((2 * FB, length), lambda i, j: (j, 0)),
                  pl.BlockSpec((length, 2 * FB), lambda i, j: (0, j)),
                  pl.BlockSpec((2 * FB, 2 * hw), lambda i, j: (j, 0))],
        out_specs=pl.BlockSpec((CONV_NB, length, hw), lambda i, j: (i, 0, 0)),
        out_shape=jax.ShapeDtypeStruct((b, length, hw), F32),
        compiler_params=_cparams("parallel", "arbitrary"),
        name="hyena_long_conv",
    )(s, wf, wft, hspec)


MLP_HC = 1024


def _mlp_body(x, m, nw_ref, w1_ref, w2_ref, fw_ref, final):
    h = _rms_mod(x, nw_ref[...], m[3:4], m[4:5]).astype(BF16)
    acc = jnp.zeros_like(x)
    for c in range(w1_ref.shape[1] // MLP_HC):
        hid = _dot(h, w1_ref[:, c * MLP_HC:(c + 1) * MLP_HC])
        hid = jnp.square(jnp.maximum(hid, 0.0))
        acc = acc + _dot(hid.astype(BF16), w2_ref[c * MLP_HC:(c + 1) * MLP_HC, :])
    o = x + m[5:6] * acc
    if final:
        o = o * lax.rsqrt(jnp.mean(o * o, axis=-1, keepdims=True) + EPS) * fw_ref[...]
    return o


def _ab_tail_kernel(y0_ref, y1_ref, bv_ref, g_ref, x0_ref, cvl_ref, cvc_ref, s_ref, x_ref, ctx_ref, mod_ref,
                    lnw_ref, lnb_ref, skip_ref, bd_ref, wo_ref, nw_ref, w1_ref, w2_ref, fw_ref, o_ref):
    bd = bd_ref[...]
    inv = 1.0 / RW_HEAD_DIM
    y = y0_ref[0].astype(F32) + y1_ref[0].astype(F32)
    mean = _segsum(y, bd) * inv
    yc = y - mean
    var = _segsum(yc * yc, bd) * inv
    yn = yc * lax.rsqrt(var + RW_GN_EPS)
    a = (yn * lnw_ref[...] + lnb_ref[...] + bv_ref[0].astype(F32)) * g_ref[0].astype(F32)
    s = s_ref[0].astype(F32)
    bh = x0_ref[0].astype(F32) * (_token_block(cvl_ref, cvc_ref) + s * skip_ref[...])
    w = RW_WIDTH
    out = _dot(a.astype(BF16), wo_ref[:w, :]) + _dot(bh.astype(BF16), wo_ref[w:, :])
    m = mod_ref[0, 0]
    xo = _token_block(x_ref, ctx_ref) + m[2:3] * out
    o_ref[0] = _mlp_body(xo, m, nw_ref, w1_ref, w2_ref, fw_ref, False)


def ab_tail(y0, y1, bv, g, x0, cv_lat, cv_ctx, s, x, ctx, modv, ln_w, ln_b, skip, bd64, w_out, nw, w1, w2, fw):
    b, _, d = x.shape
    w = RW_WIDTH
    tok = pl.BlockSpec((1, TB, w), lambda i, j: (i, j, 0))
    full = lambda a: pl.BlockSpec(a.shape, lambda i, j: (0,) * a.ndim)
    params = (ln_w, ln_b, skip, bd64, w_out, nw.reshape(1, d), w1, w2, fw.reshape(1, d))
    return pl.pallas_call(
        _ab_tail_kernel,
        grid=(b, NB),
        in_specs=[tok] * 5 + _split_specs(w) + [tok] + _split_specs(d)
        + [pl.BlockSpec((1, 1, ADA_CHUNKS, d), lambda i, j: (i, j // NB_LAT, 0, 0))] + [full(a) for a in params],
        out_specs=pl.BlockSpec((1, TB, d), lambda i, j: (i, j, 0)),
        out_shape=jax.ShapeDtypeStruct((b, T_ALL, d), F32),
        compiler_params=_cparams("parallel", "parallel"),
        name="ab_tail",
    )(y0, y1, bv, g, x0, cv_lat, cv_ctx, s, x, ctx, modv, *params)


def _dn_front_kernel(x_ref, xp_ref, xn_ref, nw_ref, mod_ref, w_ref, cw_ref, alog_ref, dtb_ref, bd_ref,
                     q_ref, k_ref, v_ref, z_ref, gb_ref):
    p, pprev, pnext = _project_with_neighbours(x_ref[0], xp_ref, xn_ref, nw_ref, mod_ref, w_ref)
    wq = 3 * DN_DIM
    c = cw_ref[0:1, :] * pprev[:, :wq] + cw_ref[1:2, :] * p[:, :wq] + cw_ref[2:3, :] * pnext[:, :wq]
    c = c * _sigmoid(c)
    dd = DN_DIM
    bd = bd_ref[...]
    q, k = c[:, :dd], c[:, dd:2 * dd]
    q_ref[0] = (q * lax.rsqrt(_segsum(q * q, bd) + EPS) * (DN_HEAD_DIM ** -0.5)).astype(BF16)
    k_ref[0] = (k * lax.rsqrt(_segsum(k * k, bd) + EPS)).astype(BF16)
    v_ref[0] = c[:, 2 * dd:].astype(BF16)
    z_ref[0] = p[:, wq:wq + dd].astype(BF16)
    slab = p[:, wq + dd:]
    lane = lax.broadcasted_iota(jnp.int32, (1, 128), 1)
    gdec = -jnp.exp(alog_ref[...]) * _softplus(slab + dtb_ref[...])
    gb_ref[0] = jnp.where(lane < 2 * DN_HEADS, gdec, _sigmoid(slab))


def dn_front(xs, nw, modv, w_in, conv_w, alog_row, dtb_row, bd128):
    b, t, d = xs.shape
    full = lambda a: pl.BlockSpec(a.shape, lambda i, j: (0,) * a.ndim)
    params = (nw.reshape(1, d), modv, w_in, conv_w, alog_row, dtb_row, bd128)
    param_specs = [full(a) for a in params]
    param_specs[1] = pl.BlockSpec((1, 1, ADA_CHUNKS, d), lambda i, j: (i, j // NB_LAT, 0, 0))
    tok = pl.BlockSpec((1, TB, DN_DIM), lambda i, j: (i, j, 0))
    s1 = jax.ShapeDtypeStruct((b, t, DN_DIM), BF16)
    return pl.pallas_call(
        _dn_front_kernel,
        grid=(b, t // TB),
        in_specs=[pl.BlockSpec((1, TB, d), lambda i, j: (i, j, 0))] + _halo_specs(d, t // 8) + param_specs,
        out_specs=[tok, tok, tok, tok, pl.BlockSpec((1, TB, 128), lambda i, j: (i, j, 0))],
        out_shape=[s1, s1, s1, s1, jax.ShapeDtypeStruct((b, t, 128), F32)],
        compiler_params=_cparams("parallel", "parallel"),
        name="dn_front",
    )(xs, xs, xs, *params)


def _dn_scan_kernel(qf_ref, kf_ref, vf_ref, gbf_ref, qr_ref, kr_ref, vr_ref, gbr_ref, yf_ref, yr_ref, s_ref):
    c, kdim = CH, DN_HEAD_DIM

    @pl.when(pl.program_id(1) == 0)
    def _():
        s_ref[...] = jnp.zeros_like(s_ref)

    sls = [slice(h * kdim, (h + 1) * kdim) for h in range(DN_HEADS)]
    dirs = ((qf_ref, kf_ref, vf_ref, gbf_ref), (qr_ref, kr_ref, vr_ref, gbr_ref))
    lhs, rhs, ks, qs, kps, bbs, vhs = ([] for _ in range(7))
    for d, (q_ref, k_ref, v_ref, gb_ref) in enumerate(dirs):
        for bi in range(SCAN_NB):
            gb = gb_ref[bi]
            q, k, v = q_ref[bi].astype(F32), k_ref[bi].astype(F32), v_ref[bi]
            for h, sl in enumerate(sls):
                col = d * DN_HEADS + h
                kp = k[:, sl] * gb[:, 2 * DN_HEADS + col:2 * DN_HEADS + col + 1]
                bb = kp * jnp.exp(gb[:, col:col + 1])
                lhs.append(jnp.concatenate([k[:, sl], q[:, sl]], axis=0))
                rhs.append(jnp.concatenate([bb, kp] if h % 2 == 0 else [kp, bb], axis=0))
                ks.append(k[:, sl])
                qs.append(q[:, sl])
                kps.append(kp)
                bbs.append(bb)
                vhs.append(v[:, sl])
    a_raw = _mm_many(lhs, rhs, NT)
    dfull, kkt, rt, khs, bhs, decs = ([] for _ in range(6))
    ti2 = lax.broadcasted_iota(jnp.int32, (c, 2 * c), 0)
    ji2 = lax.broadcasted_iota(jnp.int32, (c, 2 * c), 1) & (c - 1)
    for d, (_, _, _, gb_ref) in enumerate(dirs):
        reverse = d == 1
        _, _, _, incl = _chunk_masks(reverse)
        strict2 = (ji2 > ti2) if reverse else (ji2 < ti2)
        incl2 = jnp.logical_or(strict2, ti2 == ji2)
        inclt2 = (jnp.logical_or(ji2 < ti2, ti2 == ji2) if reverse else jnp.logical_or(ji2 > ti2, ti2 == ji2))
        last = 0 if reverse else c - 1
        gbs = [gb_ref[bi] for bi in range(SCAN_NB)]
        gcols = [_mm(incl.astype(BF16), gb, NN, 1, 2) for gb in gbs]
        grows = [_mm(gb, inclt2.astype(BF16), TN, 2, 1) for gb in gbs]
        for bi in range(SCAN_NB):
            gb, gcol, grow = gbs[bi], gcols[bi], grows[bi]
            for h in range(DN_HEADS):
                i = (d * SCAN_NB + bi) * DN_HEADS + h
                col = d * DN_HEADS + h
                gt = gcol[:, col:col + 1]
                gxt = gt - gb[:, col:col + 1]
                gj = grow[col:col + 1, :]
                gc = gcol[last:last + 1, col:col + 1]
                d3 = jnp.where(incl2, jnp.exp(jnp.where(incl2, gt - gj, 0.0)), 0.0)
                d1 = jnp.where(strict2, jnp.exp(jnp.where(strict2, gxt - gj, 0.0)), 0.0)
                dfull.append(jnp.concatenate([d1, d3], axis=0))
                egc = jnp.exp(gc - gt)
                kkt.append(ks[i] * jnp.exp(gxt))
                rt.append(qs[i] * jnp.exp(gt))
                khs.append(kps[i] * egc)
                bhs.append(bbs[i] * egc)
                decs.append(jnp.exp(gc))
    a = [x * dm for x, dm in zip(a_raw, dfull)]
    ys = _chunk_update_many(s_ref, a, kkt, rt, khs, bhs, vhs, decs, kdim)
    for d, y_ref in enumerate((yf_ref, yr_ref)):
        for bi in range(SCAN_NB):
            for h, sl in enumerate(sls):
                y_ref[bi, :, sl] = ys[(d * SCAN_NB + bi) * DN_HEADS + h].astype(BF16)


def dn_scan(q, k, v, gb):
    b, t, w = q.shape
    specs = []
    for d in range(2):
        cidx = _chunk_index(d == 1)
        tok = pl.BlockSpec((SCAN_NB, CH, w), lambda i, j, cidx=cidx: (i, cidx(j), 0))
        specs += [tok, tok, tok, pl.BlockSpec((SCAN_NB, CH, 128), lambda i, j, cidx=cidx: (i, cidx(j), 0))]
    out = jax.ShapeDtypeStruct((b, t, w), BF16)
    return pl.pallas_call(
        _dn_scan_kernel,
        grid=(b // SCAN_NB, NC),
        in_specs=specs,
        out_specs=[specs[0], specs[4]],
        out_shape=[out, out],
        scratch_shapes=[pltpu.VMEM((2 * SCAN_NB * DN_HEADS, DN_HEAD_DIM, DN_HEAD_DIM), F32)],
        compiler_params=_cparams("parallel", "arbitrary"),
        name="dn_scan",
    )(q, k, v, gb, q, k, v, gb)


def _dn_tail_kernel(o0_ref, o1_ref, z_ref, x_ref, mod_ref, gnw_ref, bd_ref, wo_ref, nw_ref, w1_ref, w2_ref, fw_ref,
                    o_ref):
    o = o0_ref[0].astype(F32) + o1_ref[0].astype(F32)
    ms = _segsum(o * o, bd_ref[...]) * (1.0 / DN_HEAD_DIM)
    on = o * lax.rsqrt(ms + EPS) * gnw_ref[...]
    z = z_ref[0].astype(F32)
    gated = on * (z * _sigmoid(z))
    m = mod_ref[0, 0]
    xo = x_ref[0] + m[2:3] * _dot(gated.astype(BF16), wo_ref[...])
    o_ref[0] = _mlp_body(xo, m, nw_ref, w1_ref, w2_ref, fw_ref, True)


def dn_tail(o0, o1, z, xs, modv, gnw_tiled, bd128, w_out, nw, w1, w2, fw):
    b, _, d = xs.shape
    tok = pl.BlockSpec((1, TB, DN_DIM), lambda i, j: (i, j, 0))
    full = lambda a: pl.BlockSpec(a.shape, lambda i, j: (0,) * a.ndim)
    params = (gnw_tiled, bd128, w_out, nw.reshape(1, d), w1, w2, fw.reshape(1, d))
    return pl.pallas_call(
        _dn_tail_kernel,
        grid=(b, NB_LAT),
        in_specs=[tok, tok, tok, pl.BlockSpec((1, TB, d), lambda i, j: (i, j, 0)),
                  pl.BlockSpec((1, 1, ADA_CHUNKS, d), lambda i, j: (i, 0, 0, 0))] + [full(a) for a in params],
        out_specs=pl.BlockSpec((1, TB, d), lambda i, j: (i, j, 0)),
        out_shape=jax.ShapeDtypeStruct((b, SEQ, d), F32),
        compiler_params=_cparams("parallel", "parallel"),
        name="dn_tail",
    )(o0, o1, z, xs, modv, *params)


def _block_diag_ones(head):
    i = jnp.arange(128) // head
    return (i[:, None] == i[None, :]).astype(BF16)


def kernel(x, c, ctx, c_ctx, ada_w, ada_b, norm_mix, norm_mlp, mlp_w1, mlp_w2, final_norm, ab_w_in, ab_w_out, rw_mu, rw_w0, rw_w_up, rw_a0, rw_a_up, rw_g_up, rw_k_k, rw_k_a, rw_r_k, rw_ln_w, rw_ln_b, hy_conv_w, hy_conv_b, hy_f_w1, hy_f_b1, hy_f_w2, hy_f_b2, hy_f_w3, hy_f_b3, hy_f_w4, hy_freq, hy_skip, dn_w_in, dn_conv_w, dn_A_log, dn_dt_bias, dn_norm, dn_w_out):
    bsz = x.shape[0]
    d = D_MODEL
    w = RW_WIDTH
    row = lambda a: a.reshape(1, -1)

    cs = jnp.concatenate([c, c_ctx[None, :], jnp.zeros((16 - bsz - 1, d), F32)], axis=0)
    mod = ada_modulation(cs, ada_w, ada_b)

    def mod_vectors(layer):
        lat = mod[layer, :bsz].reshape(bsz, 1, ADA_CHUNKS, d)
        cx = jnp.broadcast_to(mod[layer, bsz].reshape(1, 1, ADA_CHUNKS, d), (bsz, 1, ADA_CHUNKS, d))
        return jnp.concatenate([lat, cx], axis=1)

    bd64 = _block_diag_ones(RW_HEAD_DIM)
    bd128 = _block_diag_ones(DN_HEAD_DIM)

    modv = mod_vectors(0)
    zeros = jnp.zeros((64, 2 * w), F32)
    lora = jnp.concatenate([jnp.concatenate([rw_w_up[0, 0], rw_w_up[0, 1], zeros], axis=1),
                            jnp.concatenate([zeros, rw_a_up[0, 0], rw_a_up[0, 1]], axis=1)], axis=0)
    r, v, kk, g, bv, lw, kd, bb, s, x0 = ab_front(
        x, ctx, norm_mix[0], modv, ab_w_in[0].astype(BF16), row(rw_mu[0]), lora.astype(BF16), row(rw_w0[0]),
        row(rw_a0[0]), rw_g_up[0].astype(BF16), row(rw_k_k[0]), row(rw_k_a[0]), row(rw_r_k[0]), bd64,
        hy_conv_w[0], row(hy_conv_b[0]))
    y0, y1 = rwkv_scan(r, v, kk, lw, kd, bb)
    filt = (hy_f_w1[0], hy_f_b1[0], hy_f_w2[0], hy_f_b2[0], hy_f_w3[0], hy_f_b3[0], hy_f_w4[0], hy_freq[0])
    convs = []
    for length, block in ((SEQ, 0), (CTX_LEN, SEQ // CTX_LEN)):
        wf, wft = dft_matrices(length)
        hspec = filter_spectrum(wf, hyena_filter_halves(length, *filt))
        convs.append(hyena_long_conv(s, wf, wft, hspec, length, block))
    xs = ab_tail(y0, y1, bv, g, x0, convs[0], convs[1], s, x, ctx, modv, row(rw_ln_w[0]), row(rw_ln_b[0]),
                 row(hy_skip[0]), bd64, ab_w_out[0].astype(BF16), norm_mlp[0], mlp_w1[0].astype(BF16),
                 mlp_w2[0].astype(BF16), final_norm)

    modv = mod_vectors(1)
    w_in = jnp.pad(dn_w_in[0], ((0, 0), (0, DN_COLS_PAD - DN_COLS))).astype(BF16)
    pad_row = lambda a: jnp.pad(a.reshape(1, -1), ((0, 0), (0, 128 - 2 * DN_HEADS)))
    q, k, vv, z, gb = dn_front(xs, norm_mix[1], modv, w_in, dn_conv_w[0], pad_row(dn_A_log[0]),
                               pad_row(dn_dt_bias[0]), bd128)
    o0, o1 = dn_scan(q, k, vv, gb)
    return dn_tail(o0, o1, z, xs, modv, row(jnp.tile(dn_norm[0], DN_HEADS)), bd128, dn_w_out[0].astype(BF16),
                   norm_mlp[1], mlp_w1[1].astype(BF16), mlp_w2[1].astype(BF16), final_norm)
```

```python
import functools
import math

import jax
import jax.numpy as jnp
from jax import lax
from jax.experimental import pallas as pl
from jax.experimental.pallas import tpu as pltpu

F32, BF16 = jnp.float32, jnp.bfloat16

D_MODEL = 1024
SEQ = 2048
CTX_LEN = 256
ADA_CHUNKS = 6
EPS = 1e-6
RW_WIDTH = 512
RW_HEAD_DIM = 64
RW_HEADS = 8
RW_COLS = 1792
RW_GN_EPS = 64e-5
HY_WIDTH = 512
HY_COLS = 1536
HY_BANDS = 16
HY_EMB = 33
HY_ORDER = 64
HY_FAST_DECAY = 0.3
HY_SLOW_DECAY = 1.5
HY_TARGET = 1e-2
AB_COLS = RW_COLS + HY_COLS
DN_HEADS = 8
DN_HEAD_DIM = 128
DN_DIM = 1024
DN_COLS = 4 * DN_DIM + 4 * DN_HEADS
DN_COLS_PAD = 4 * DN_DIM + 128

TB = 256
CH = 64
T_ALL = SEQ + CTX_LEN
NB = T_ALL // TB
NB_LAT = SEQ // TB
NC = T_ALL // CH
NC_LAT = SEQ // CH
FB = 256
CONV_NB = 2
SCAN_NB = 4
VMEM_LIMIT = 56 * 1024 * 1024

NN = ((1,), (0,))
NT = ((1,), (1,))
TN = ((0,), (0,))


def _dot(a, b, dims=NN):
    return lax.dot_general(a, b, (dims, ((), ())), preferred_element_type=F32)


def _split(x):
    hi = x.astype(BF16)
    lo = (x - hi.astype(F32)).astype(BF16)
    return hi, lo


def _mm(a, b, dims=NN, pa=1, pb=1):
    if a.dtype == BF16:
        a_hi, a_lo, pa = a, None, 1
    elif pa == 2:
        a_hi, a_lo = _split(a)
    else:
        a_hi, a_lo = a.astype(BF16), None
    if b.dtype == BF16:
        b_hi, b_lo, pb = b, None, 1
    elif pb == 2:
        b_hi, b_lo = _split(b)
    else:
        b_hi, b_lo = b.astype(BF16), None
    out = _dot(a_hi, b_hi, dims)
    if pa == 2:
        out = out + _dot(a_lo, b_hi, dims)
    if pb == 2:
        out = out + _dot(a_hi, b_lo, dims)
    return out


def _segsum(x, bd):
    parts = [_dot(x[:, g * 128:(g + 1) * 128].astype(BF16), bd) for g in range(x.shape[1] // 128)]
    return jnp.concatenate(parts, axis=1)


def _sigmoid(x):
    return 1.0 / (1.0 + jnp.exp(-x))


def _softplus(x):
    return jnp.maximum(x, 0.0) + jnp.log(1.0 + jnp.exp(-jnp.abs(x)))


def _rms_mod(x, nw, shift, scale):
    y = x * lax.rsqrt(jnp.mean(x * x, axis=-1, keepdims=True) + EPS)
    return (y * nw) * (1.0 + scale) + shift


def _cparams(*sem):
    return pltpu.CompilerParams(dimension_semantics=sem, vmem_limit_bytes=VMEM_LIMIT)


ADA_TN = 1536


def _ada_kernel(c_ref, w_ref, b_ref, o_ref):
    c = c_ref[...]
    a = c * _sigmoid(c)
    o_ref[0] = _mm(a, w_ref[0], NN, 2, 2) + b_ref[0]


def ada_modulation(cs, ada_w, ada_b):
    depth, d, n = ada_w.shape
    rows = cs.shape[0]
    return pl.pallas_call(
        _ada_kernel,
        grid=(depth, n // ADA_TN),
        in_specs=[pl.BlockSpec((rows, d), lambda l, j: (0, 0)),
                  pl.BlockSpec((1, d, ADA_TN), lambda l, j: (l, 0, j)),
                  pl.BlockSpec((1, 1, ADA_TN), lambda l, j: (l, 0, j))],
        out_specs=pl.BlockSpec((1, rows, ADA_TN), lambda l, j: (l, 0, j)),
        out_shape=jax.ShapeDtypeStruct((depth, rows, n), F32),
        compiler_params=_cparams("parallel", "parallel"),
        name="ada_modulation",
    )(cs, ada_w, ada_b.reshape(depth, 1, n))


def _token_block(x_ref, ctx_ref):
    return jnp.where(pl.program_id(1) == NB - 1, ctx_ref[0], x_ref[0])


def _split_specs(d):
    return [pl.BlockSpec((1, TB, d), lambda i, j: (i, jnp.minimum(j, NB_LAT - 1), 0)),
            pl.BlockSpec((1, TB, d), lambda i, j: (i, 0, 0))]


def _halo_specs(width, n_row_blocks):
    prev = pl.BlockSpec((1, 8, width), lambda i, j: (i, jnp.maximum(j * (TB // 8) - 1, 0), 0))
    nxt = pl.BlockSpec((1, 8, width), lambda i, j: (i, jnp.minimum((j + 1) * (TB // 8), n_row_blocks - 1), 0))
    return [prev, nxt]


def _project_with_neighbours(x, prev_ref, next_ref, nw_ref, mod_ref, w_ref):
    j = pl.program_id(1)
    m = mod_ref[0, 0]
    x_ext = jnp.concatenate([x, prev_ref[0], next_ref[0]], axis=0)
    h = _rms_mod(x_ext, nw_ref[...], m[0:1], m[1:2])
    p_ext = _dot(h.astype(BF16), w_ref[...])
    p = p_ext[:TB]
    pv = jnp.where(jnp.logical_and(j != 0, j != NB - 1), 1.0, 0.0).astype(F32)
    nv = jnp.where(j < NB - 2, 1.0, 0.0).astype(F32)
    prow = p_ext[TB + 7:TB + 8] * pv
    nrow = p_ext[TB + 8:TB + 9] * nv
    rows = lax.broadcasted_iota(jnp.int32, (8, 1), 0)
    pprev = pltpu.roll(p, 1, 0)
    pprev = jnp.concatenate([jnp.where(rows == 0, prow, pprev[:8]), pprev[8:]], axis=0)
    pnext = pltpu.roll(p, TB - 1, 0)
    pnext = jnp.concatenate([pnext[:TB - 8], jnp.where(rows == 7, nrow, pnext[TB - 8:])], axis=0)
    return p, pprev, pnext


def _ab_front_kernel(x_ref, ctx_ref, xp_ref, xn_ref, nw_ref, mod_ref, w_ref,
                     mu_ref, lora_ref, w0_ref, a0_ref, gup_ref, kk_ref, ka_ref, rk_ref, bd_ref, cw_ref, cb_ref,
                     r_ref, v_ref, kkn_ref, g_ref, bv_ref, lw_ref, kd_ref, b_ref, s_ref, x0_ref):
    x, xprev, xnext = _project_with_neighbours(_token_block(x_ref, ctx_ref), xp_ref, xn_ref, nw_ref, mod_ref, w_ref)
    w = RW_WIDTH
    xr = x[:, :RW_COLS]
    ps = xr + mu_ref[...] * (0.5 * (xprev[:, :RW_COLS] + xnext[:, :RW_COLS]) - xr)
    r, k, v = ps[:, :w], ps[:, w:2 * w], ps[:, 2 * w:3 * w]
    slab = ps[:, 3 * w:3 * w + 128]
    gl = ps[:, 3 * w + 128:3 * w + 256]
    lane = lax.broadcasted_iota(jnp.int32, (1, 128), 1)
    z = jnp.where(lane < 64, jnp.tanh(slab), slab)
    lora = _dot(z.astype(BF16), lora_ref[...])
    bd = bd_ref[...]
    kkraw = k * kk_ref[...]
    kk = kkraw * lax.rsqrt(_segsum(kkraw * kkraw, bd) + EPS)
    ksum = jnp.zeros_like(k)
    for d in range(2):
        lw_ref[d, 0] = (-math.exp(-0.5)) * _sigmoid(w0_ref[:, d * w:(d + 1) * w] + lora[:, d * w:(d + 1) * w])
        a = _sigmoid(a0_ref[:, d * w:(d + 1) * w] + lora[:, (2 + d) * w:(3 + d) * w])
        kd = k * (1.0 + (a - 1.0) * ka_ref[...])
        kd_ref[d, 0] = kd.astype(BF16)
        b_ref[d, 0] = (a * kk).astype(BF16)
        ksum = ksum + kd
    r_ref[0] = r.astype(BF16)
    v_ref[0] = v.astype(BF16)
    kkn_ref[0] = kk.astype(BF16)
    g_ref[0] = _dot(_sigmoid(gl).astype(BF16), gup_ref[...]).astype(BF16)
    bv_ref[0] = (_segsum(r * ksum * rk_ref[...], bd) * v).astype(BF16)
    u = x[:, RW_COLS:]
    c = (cw_ref[0:1, :] * xprev[:, RW_COLS:] + cw_ref[1:2, :] * u + cw_ref[2:3, :] * xnext[:, RW_COLS:]
         + cb_ref[...])
    hw = HY_WIDTH
    x0_ref[0] = c[:, :hw].astype(BF16)
    s_ref[0] = (c[:, hw:2 * hw] * c[:, 2 * hw:]).astype(BF16)


def ab_front(x, ctx, nw, modv, w_in, mu, lora, w0, a0, gup, k_k, k_a, r_k, bd64, conv_w, conv_b):
    b, _, d = x.shape
    w = RW_WIDTH
    full = lambda a: pl.BlockSpec(a.shape, lambda i, j: (0,) * a.ndim)
    params = (nw.reshape(1, d), modv, w_in, mu, lora, w0, a0, gup, k_k, k_a, r_k, bd64, conv_w, conv_b)
    param_specs = [full(a) for a in params]
    param_specs[1] = pl.BlockSpec((1, 1, ADA_CHUNKS, d), lambda i, j: (i, j // NB_LAT, 0, 0))
    tok = pl.BlockSpec((1, TB, w), lambda i, j: (i, j, 0))
    tok2 = pl.BlockSpec((2, 1, TB, w), lambda i, j: (0, i, j, 0))
    s1 = jax.ShapeDtypeStruct((b, T_ALL, w), BF16)
    s2 = jax.ShapeDtypeStruct((2, b, T_ALL, w), BF16)
    s2f = jax.ShapeDtypeStruct((2, b, T_ALL, w), F32)
    return pl.pallas_call(
        _ab_front_kernel,
        grid=(b, NB),
        in_specs=_split_specs(d) + _halo_specs(d, SEQ // 8) + param_specs,
        out_specs=[tok, tok, tok, tok, tok, tok2, tok2, tok2, tok, tok],
        out_shape=[s1, s1, s1, s1, s1, s2f, s2, s2, s1, s1],
        compiler_params=_cparams("parallel", "parallel"),
        name="ab_front",
    )(x, ctx, x, x, *params)


def _chunk_masks(reverse):
    c = CH
    ti = lax.broadcasted_iota(jnp.int32, (c, c), 0)
    ji = lax.broadcasted_iota(jnp.int32, (c, c), 1)
    strict = (ji > ti) if reverse else (ji < ti)
    incl = jnp.logical_or(strict, ti == ji)
    return ti, ji, strict, incl


def _mm_many(a_list, b_list, dims=NN):
    ops = [(a.astype(BF16), b.astype(BF16)) for a, b in zip(a_list, b_list)]
    return [_dot(a, b, dims) for a, b in ops]


def _tri_inverse_pairs(n_pairs):
    c = CH
    ti = lax.broadcasted_iota(jnp.int32, (c, 2 * c), 0)
    li = lax.broadcasted_iota(jnp.int32, (c, 2 * c), 1)
    ji = li & (c - 1)
    left = li < c

    def block_diag(x):
        xb = x.astype(BF16)
        zero = jnp.zeros_like(xb)
        return jnp.concatenate([jnp.where(left, xb, zero), jnp.where(left, zero, xb)], axis=0)

    eye = (ti == ji).astype(F32)
    pair = (ti >> 1) == (ji >> 1)
    dinvs = [eye - jnp.where(pair, n, 0.0) for n in n_pairs]
    s = 1
    while (2 << s) <= c:
        off = jnp.logical_and((ti >> (s + 1)) == (ji >> (s + 1)), (ti >> s) != (ji >> s))
        coffs = [block_diag(jnp.where(off, n, 0.0)) for n in n_pairs]
        xs = _mm_many(dinvs, coffs)
        corr = _mm_many(xs, [block_diag(d) for d in dinvs])
        dinvs = [d - t for d, t in zip(dinvs, corr)]
        s += 1
    return dinvs


def _chunk_update_many(s_ref, a, kkt, rt, kh, bh, vh, dec_c, kdim):
    n = len(a)
    c = CH
    left = lax.broadcasted_iota(jnp.int32, (c, 2 * c), 1) < c
    tpairs = _tri_inverse_pairs([jnp.where(left, a[i][:c], a[i + 1][:c]) for i in range(0, n, 2)])
    tms = [tpairs[i // 2][:, :c] if i % 2 == 0 else tpairs[i // 2][:, c:] for i in range(n)]
    a_v = [a[i][:, c:] if i % 2 == 0 else a[i][:, :c] for i in range(n)]
    a_rb = [a[i][c:, :c] if i % 2 == 0 else a[i][c:, c:] for i in range(n)]
    av = _mm_many(a_v, vh)
    tw = _mm_many(tms, [jnp.concatenate([kkt[i], av[i][:c]], axis=1) for i in range(n)])
    states = [s_ref[i] for i in range(n)]
    p = _mm_many([jnp.concatenate([tw[i][:, :kdim], rt[i]], axis=0) for i in range(n)], states, NT)
    u = [p[i][:c] + tw[i][:, kdim:] for i in range(n)]
    au = _mm_many(a_rb, u)
    ds = _mm_many([jnp.concatenate([vh[i], -u[i]], axis=0) for i in range(n)],
                  [jnp.concatenate([kh[i], bh[i]], axis=0) for i in range(n)], TN)
    for i in range(n):
        s_ref[i] = states[i] * dec_c[i] + ds[i]
    return [p[i][c:] + av[i][c:] - au[i] for i in range(n)]


def _pair_update_many(s_ref, a, kkt, rt, kh, bh, vh, dec_c):
    n = len(kkt)
    c = CH
    left = lax.broadcasted_iota(jnp.int32, (1, 2 * c), 1) < c

    def diag(x, anti=False):
        xb = x.astype(BF16)
        zero = jnp.zeros_like(xb)
        top, bottom = jnp.where(left, xb, zero), jnp.where(left, zero, xb)
        return jnp.concatenate([bottom, top] if anti else [top, bottom], axis=0)

    tps = _tri_inverse_pairs([jnp.where(left, a[2 * i + 1][:c], a[2 * i][:c]) for i in range(n)])
    av = _mm_many([jnp.where(left, a[2 * i], a[2 * i + 1]) for i in range(n)], [diag(x) for x in vh])
    tw = _mm_many(tps, [jnp.concatenate([diag(kkt[i], True), diag(av[i][:c], True)], axis=1) for i in range(n)])
    states = [s_ref[i] for i in range(n)]
    p = _mm_many([jnp.concatenate([tw[i][:, :2 * c], rt[i]], axis=0) for i in range(n)],
                 [diag(x) for x in states], NT)
    u = [p[i][:c] + tw[i][:, 2 * c:] for i in range(n)]
    au = _mm_many([jnp.where(left, a[2 * i + 1][c:], a[2 * i][c:]) for i in range(n)], [diag(x, True) for x in u])
    ds = _mm_many([jnp.concatenate([vh[i], -u[i]], axis=0) for i in range(n)],
                  [jnp.concatenate([kh[i], bh[i]], axis=0) for i in range(n)], TN)
    for i in range(n):
        s_ref[i] = states[i] * dec_c[i] + jnp.where(left, ds[i][:c], ds[i][c:])
    return [p[i][c:] + av[i][c:] - au[i] for i in range(n)]


def _chunk_index(reverse):
    if reverse:
        return lambda i: NC - 1 - i
    return lambda i: (i + NC_LAT) % NC


def _rwkv_scan_kernel(rf_ref, vf_ref, kkf_ref, rr_ref, vr_ref, kkr_ref, lwf_ref, kdf_ref, bf_ref,
                      lwr_ref, kdr_ref, br_ref, yf_ref, yr_ref, s_ref):
    c, kdim = CH, RW_HEAD_DIM

    @pl.when(pl.program_id(1) == 0)
    def _():
        s_ref[...] = jnp.zeros_like(s_ref)

    sls = [slice(m * 2 * kdim, (m + 1) * 2 * kdim) for m in range(RW_HEADS // 2)]
    slabs = lambda x: [x[:, sl] for sl in sls]
    left = lax.broadcasted_iota(jnp.int32, (1, 2 * kdim), 1) < kdim
    ri = lax.broadcasted_iota(jnp.int32, (2 * c, 2 * c), 0)
    ci = lax.broadcasted_iota(jnp.int32, (2 * c, 2 * c), 1)
    rt_, ct_ = ri & (c - 1), ci & (c - 1)
    lhs, rhs, amasks, kkt, rt, kh, bh, vh, dec = ([] for _ in range(9))
    dirs = ((rf_ref, vf_ref, kkf_ref, lwf_ref, kdf_ref, bf_ref), (rr_ref, vr_ref, kkr_ref, lwr_ref, kdr_ref, br_ref))
    for d, (r_ref, v_ref, kk_ref, lw_ref, kd_ref, b_ref) in enumerate(dirs):
        reverse = d == 1
        _, _, _, incl = _chunk_masks(reverse)
        before = (ct_ > rt_) if reverse else (ct_ < rt_)
        amask = jnp.logical_or(before, jnp.logical_and(rt_ == ct_, ri >= c))
        last = 0 if reverse else c - 1
        lws = [lw_ref[0, bi] for bi in range(SCAN_NB)]
        gs = [_mm(incl.astype(BF16), lw, NN, 1, 2) for lw in lws]
        for bi in range(SCAN_NB):
            lw, g = lws[bi], gs[bi]
            gc = g[last:last + 1, :]
            eng = jnp.exp(-g)
            egc = jnp.exp(gc - g)
            kd, bb = kd_ref[0, bi].astype(F32), b_ref[0, bi].astype(F32)
            kkt_all = kk_ref[bi].astype(F32) * jnp.exp(g - lw)
            rt_all = r_ref[bi].astype(F32) * jnp.exp(g)
            for x, y, kn, bn in zip(slabs(kkt_all), slabs(rt_all), slabs(kd * eng), slabs(bb * eng)):
                both = jnp.concatenate([x, y], axis=0)
                lhs += [jnp.where(left, both, 0.0), jnp.where(left, 0.0, both)]
                rhs += [jnp.concatenate([kn, bn], axis=0), jnp.concatenate([bn, kn], axis=0)]
            amasks += [amask] * RW_HEADS
            kkt += slabs(kkt_all)
            rt += slabs(rt_all)
            kh += slabs(kd * egc)
            bh += slabs(bb * egc)
            vh += slabs(v_ref[bi])
            dec += slabs(jnp.exp(gc))
    a = [jnp.where(m, x, 0.0) for m, x in zip(amasks, _mm_many(lhs, rhs, NT))]
    ys = _pair_update_many(s_ref, a, kkt, rt, kh, bh, vh, dec)
    for d, y_ref in enumerate((yf_ref, yr_ref)):
        for bi in range(SCAN_NB):
            for m, sl in enumerate(sls):
                y_ref[bi, :, sl] = ys[(d * SCAN_NB + bi) * len(sls) + m].astype(BF16)


def rwkv_scan(r, v, kk, lw, kd, bb):
    b, t, w = r.shape
    assert b % SCAN_NB == 0 and t == T_ALL and w == RW_WIDTH
    specs, specs_d = [], []
    for d in range(2):
        cidx = _chunk_index(d == 1)
        specs += [pl.BlockSpec((SCAN_NB, CH, w), lambda i, j, cidx=cidx: (i, cidx(j), 0))] * 3
        specs_d += [pl.BlockSpec((1, SCAN_NB, CH, w), lambda i, j, cidx=cidx, d=d: (d, i, cidx(j), 0))] * 3
    out = jax.ShapeDtypeStruct((b, t, w), BF16)
    return pl.pallas_call(
        _rwkv_scan_kernel,
        grid=(b // SCAN_NB, NC),
        in_specs=specs + specs_d,
        out_specs=[specs[0], specs[3]],
        out_shape=[out, out],
        scratch_shapes=[pltpu.VMEM((SCAN_NB * RW_HEADS, RW_HEAD_DIM, 2 * RW_HEAD_DIM), F32)],
        compiler_params=_cparams("parallel", "arbitrary"),
        name="rwkv_scan",
    )(r, v, kk, r, v, kk, lw, kd, bb, lw, kd, bb)


def _hy_filter_kernel(z_ref, t_ref, w1_ref, b1_ref, w2_ref, b2_ref, w3_ref, b3_ref, w4_ref, fr_ref, dl_ref, o_ref):
    fr = fr_ref[...]
    h = jnp.sin(fr * (_mm(z_ref[...], w1_ref[...], NN, 2, 2) + b1_ref[...]))
    h = jnp.sin(fr * (_mm(h, w2_ref[...], NN, 2, 2) + b2_ref[...]))
    h = jnp.sin(fr * (_mm(h, w3_ref[...], NN, 2, 2) + b3_ref[...]))
    h = _mm(h, w4_ref[...], NN, 2, 2)
    h = h * jnp.exp(-t_ref[...] * dl_ref[...])
    hw = HY_WIDTH
    rows = lax.broadcasted_iota(jnp.int32, (h.shape[0], 1), 0)
    hf = h[:, :hw]
    hb = jnp.where(rows == 0, 0.0, h[:, hw:])
    norm = jnp.sum(jnp.abs(hf), axis=0, keepdims=True) + jnp.sum(jnp.abs(hb), axis=0, keepdims=True)
    inv_norm = 1.0 / norm
    o_ref[:, :hw] = (hf + hb) * inv_norm
    o_ref[:, hw:] = (hf - hb) * inv_norm


def hyena_filter_halves(length, w1, b1, w2, b2, w3, b3, w4, freq):
    t = jnp.linspace(0.0, 1.0, length, dtype=F32)[:, None]
    wv = 2 * math.pi * jnp.arange(length, dtype=F32)[:, None] / length
    f = jnp.linspace(1e-4, HY_BANDS - 1, HY_BANDS, dtype=F32)[None, :]
    z = jnp.concatenate([t, jnp.cos(f * wv), -jnp.sin(f * wv)], axis=-1)
    z = jnp.pad(z, ((0, 0), (0, HY_ORDER - HY_EMB)))
    w1p = jnp.pad(w1, ((0, HY_ORDER - HY_EMB), (0, 0)))
    deltas = jnp.abs(jnp.linspace(math.log(HY_TARGET) / HY_SLOW_DECAY, math.log(HY_TARGET) / HY_FAST_DECAY,
                                  HY_WIDTH, dtype=F32))
    dl = jnp.tile(deltas, 2)[None, :]
    row = lambda a: a.reshape(1, -1)
    return pl.pallas_call(
        _hy_filter_kernel,
        out_shape=jax.ShapeDtypeStruct((length, 2 * HY_WIDTH), F32),
        compiler_params=pltpu.CompilerParams(vmem_limit_bytes=VMEM_LIMIT),
        name="hyena_filter",
    )(z, t, w1p, row(b1), w2, row(b2), w3, row(b3), w4, row(freq), dl)


def dft_matrices(length):
    n = 2 * length
    nfb = length // FB
    t = jnp.arange(length, dtype=jnp.int32)
    ang = lambda f: ((f[:, None] * t[None, :]) % n).astype(F32) * (2 * math.pi / n)
    a_hi = ang(jnp.arange(nfb, dtype=jnp.int32) * FB)
    a_lo = ang(jnp.arange(FB, dtype=jnp.int32))
    nyq = jnp.where(t % 2 == 0, 1.0, -1.0).astype(F32)

    def build(ch, sh, cl, sl, first, nyq_b, axis):
        cosm = ch * cl - sh * sl
        msin = -(sh * cl + ch * sl)
        msin = jnp.where(first, nyq_b, msin)
        return jnp.concatenate([cosm, msin], axis=axis).astype(BF16)

    bi = lax.broadcasted_iota
    first = jnp.logical_and(bi(jnp.int32, (nfb, FB, 1), 0) == 0, bi(jnp.int32, (nfb, FB, 1), 1) == 0)
    wf = build(jnp.cos(a_hi)[:, None, :], jnp.sin(a_hi)[:, None, :], jnp.cos(a_lo)[None], jnp.sin(a_lo)[None],
               first, nyq[None, None, :], 1).reshape(n, length)
    first_t = jnp.logical_and(bi(jnp.int32, (1, nfb, FB), 1) == 0, bi(jnp.int32, (1, nfb, FB), 2) == 0)
    wft = build(jnp.cos(a_hi).T[:, :, None], jnp.sin(a_hi).T[:, :, None], jnp.cos(a_lo).T[:, None, :],
                jnp.sin(a_lo).T[:, None, :], first_t, nyq[:, None, None], 2).reshape(length, n)
    return wf, wft


def _spectrum_kernel(w_ref, h_ref, o_ref):
    o_ref[...] = _dot(w_ref[...], h_ref[...].astype(BF16))


def filter_spectrum(wf, hpm):
    n, length = wf.shape
    cols = hpm.shape[1]
    return pl.pallas_call(
        _spectrum_kernel,
        grid=(n // (2 * FB),),
        in_specs=[pl.BlockSpec((2 * FB, length), lambda i: (i, 0)),
                  pl.BlockSpec((length, cols), lambda i: (0, 0))],
        out_specs=pl.BlockSpec((2 * FB, cols), lambda i: (i, 0)),
        out_shape=jax.ShapeDtypeStruct((n, cols), F32),
        compiler_params=_cparams("parallel"),
        name="filter_spectrum",
    )(wf, hpm)


def _hy_conv_kernel(s_ref, wf_ref, wft_ref, h_ref, o_ref, *, n_fft):
    fb = pl.program_id(1)

    @pl.when(fb == 0)
    def _():
        o_ref[...] = jnp.zeros_like(o_ref)

    hw = HY_WIDTH
    hraw = h_ref[...]
    hr, hi, hn = hraw[:FB, :hw], hraw[FB:, hw:], hraw[FB:, :hw]
    fidx = lax.broadcasted_iota(jnp.int32, (FB, 1), 0) + fb * FB
    m = (fidx != 0).astype(F32)
    cf = jnp.where(fidx != 0, 2.0 / n_fft, 1.0 / n_fft).astype(F32)
    hr_c = hr * cf
    him = hi * m * cf
    hc = (hr * m + hn * (1.0 - m)) * cf
    for bi in range(s_ref.shape[0]):
        x = _dot(wf_ref[...], s_ref[bi])
        xr, xi = x[:FB], x[FB:]
        y = jnp.concatenate([xr * hr_c - xi * him, xr * him + xi * hc], axis=0).astype(BF16)
        o_ref[bi] += _dot(wft_ref[...], y)


def hyena_long_conv(s, wf, wft, hspec, length, block):
    b = s.shape[0]
    hw = HY_WIDTH
    assert b % CONV_NB == 0 and length % FB == 0
    return pl.pallas_call(
        functools.partial(_hy_conv_kernel, n_fft=2 * length),
        grid=(b // CONV_NB, length // FB),
        in_specs=[pl.BlockSpec((CONV_NB, length, hw), lambda i, j: (i, block, 0)),
                  pl.BlockSpec((2 * FB, length), lambda i, j: (j, 0)),
                  pl.BlockSpec((length, 2 * FB), lambda i, j: (0, j)),
                  pl.BlockSpec((2 * FB, 2 * hw), lambda i, j: (j, 0))],
        out_specs=pl.BlockSpec((CONV_NB, length, hw), lambda i, j: (i, 0, 0)),
        out_shape=jax.ShapeDtypeStruct((b, length, hw), F32),
        compiler_params=_cparams("parallel", "arbitrary"),
        name="hyena_long_conv",
    )(s, wf, wft, hspec)


MLP_HC = 1024


def _mlp_body(x, m, nw_ref, w1_ref, w2_ref, fw_ref, final):
    h = _rms_mod(x, nw_ref[...], m[3:4], m[4:5]).astype(BF16)
    acc = jnp.zeros_like(x)
    for c in range(w1_ref.shape[1] // MLP_HC):
        hid = _dot(h, w1_ref[:, c * MLP_HC:(c + 1) * MLP_HC])
        hid = jnp.square(jnp.maximum(hid, 0.0))
        acc = acc + _dot(hid.astype(BF16), w2_ref[c * MLP_HC:(c + 1) * MLP_HC, :])
    o = x + m[5:6] * acc
    if final:
        o = o * lax.rsqrt(jnp.mean(o * o, axis=-1, keepdims=True) + EPS) * fw_ref[...]
    return o


def _ab_tail_kernel(y0_ref, y1_ref, bv_ref, g_ref, x0_ref, cvl_ref, cvc_ref, s_ref, x_ref, ctx_ref, mod_ref,
                    lnw_ref, lnb_ref, skip_ref, bd_ref, wo_ref, nw_ref, w1_ref, w2_ref, fw_ref, o_ref):
    bd = bd_ref[...]
    inv = 1.0 / RW_HEAD_DIM
    y = y0_ref[0].astype(F32) + y1_ref[0].astype(F32)
    mean = _segsum(y, bd) * inv
    yc = y - mean
    var = _segsum(yc * yc, bd) * inv
    yn = yc * lax.rsqrt(var + RW_GN_EPS)
    a = (yn * lnw_ref[...] + lnb_ref[...] + bv_ref[0].astype(F32)) * g_ref[0].astype(F32)
    s = s_ref[0].astype(F32)
    bh = x0_ref[0].astype(F32) * (_token_block(cvl_ref, cvc_ref) + s * skip_ref[...])
    w = RW_WIDTH
    out = _dot(a.astype(BF16), wo_ref[:w, :]) + _dot(bh.astype(BF16), wo_ref[w:, :])
    m = mod_ref[0, 0]
    xo = _token_block(x_ref, ctx_ref) + m[2:3] * out
    o_ref[0] = _mlp_body(xo, m, nw_ref, w1_ref, w2_ref, fw_ref, False)


def ab_tail(y0, y1, bv, g, x0, cv_lat, cv_ctx, s, x, ctx, modv, ln_w, ln_b, skip, bd64, w_out, nw, w1, w2, fw,
            layer):
    b, _, d = x.shape
    w = RW_WIDTH
    tok = pl.BlockSpec((1, TB, w), lambda i, j: (i, j, 0))
    full = lambda a: pl.BlockSpec(a.shape, lambda i, j: (0,) * a.ndim)
    params = (ln_w, ln_b, skip, bd64, w_out, nw.reshape(1, d), w1, w2, fw.reshape(1, d))
    param_specs = [full(a) for a in params]
    param_specs[6] = pl.BlockSpec((None,) + w1.shape[1:], lambda i, j: (layer, 0, 0))
    param_specs[7] = pl.BlockSpec((None,) + w2.shape[1:], lambda i, j: (layer, 0, 0))
    return pl.pallas_call(
        _ab_tail_kernel,
        grid=(b, NB),
        in_specs=[tok] * 5 + _split_specs(w) + [tok] + _split_specs(d)
        + [pl.BlockSpec((1, 1, ADA_CHUNKS, d), lambda i, j: (i, j // NB_LAT, 0, 0))] + param_specs,
        out_specs=pl.BlockSpec((1, TB, d), lambda i, j: (i, j, 0)),
        out_shape=jax.ShapeDtypeStruct((b, T_ALL, d), F32),
        compiler_params=_cparams("parallel", "parallel"),
        name="ab_tail",
    )(y0, y1, bv, g, x0, cv_lat, cv_ctx, s, x, ctx, modv, *params)


def _dn_front_kernel(x_ref, xp_ref, xn_ref, nw_ref, mod_ref, w_ref, cw_ref, alog_ref, dtb_ref, bd_ref,
                     q_ref, k_ref, v_ref, z_ref, gb_ref):
    p, pprev, pnext = _project_with_neighbours(x_ref[0], xp_ref, xn_ref, nw_ref, mod_ref, w_ref)
    wq = 3 * DN_DIM
    c = cw_ref[0:1, :] * pprev[:, :wq] + cw_ref[1:2, :] * p[:, :wq] + cw_ref[2:3, :] * pnext[:, :wq]
    c = c * _sigmoid(c)
    dd = DN_DIM
    bd = bd_ref[...]
    q, k = c[:, :dd], c[:, dd:2 * dd]
    q_ref[0] = (q * lax.rsqrt(_segsum(q * q, bd) + EPS) * (DN_HEAD_DIM ** -0.5)).astype(BF16)
    k_ref[0] = (k * lax.rsqrt(_segsum(k * k, bd) + EPS)).astype(BF16)
    v_ref[0] = c[:, 2 * dd:].astype(BF16)
    z_ref[0] = p[:, wq:wq + dd].astype(BF16)
    slab = p[:, wq + dd:]
    lane = lax.broadcasted_iota(jnp.int32, (1, 128), 1)
    gdec = -jnp.exp(alog_ref[...]) * _softplus(slab + dtb_ref[...])
    gb_ref[0] = jnp.where(lane < 2 * DN_HEADS, gdec, _sigmoid(slab))


def dn_front(xs, nw, modv, w_in, conv_w, alog_row, dtb_row, bd128):
    b, t, d = xs.shape
    full = lambda a: pl.BlockSpec(a.shape, lambda i, j: (0,) * a.ndim)
    params = (nw.reshape(1, d), modv, w_in, conv_w, alog_row, dtb_row, bd128)
    param_specs = [full(a) for a in params]
    param_specs[1] = pl.BlockSpec((1, 1, ADA_CHUNKS, d), lambda i, j: (i, j // NB_LAT, 0, 0))
    tok = pl.BlockSpec((1, TB, DN_DIM), lambda i, j: (i, j, 0))
    s1 = jax.ShapeDtypeStruct((b, t, DN_DIM), BF16)
    return pl.pallas_call(
        _dn_front_kernel,
        grid=(b, t // TB),
        in_specs=[pl.BlockSpec((1, TB, d), lambda i, j: (i, j, 0))] + _halo_specs(d, t // 8) + param_specs,
        out_specs=[tok, tok, tok, tok, pl.BlockSpec((1, TB, 128), lambda i, j: (i, j, 0))],
        out_shape=[s1, s1, s1, s1, jax.ShapeDtypeStruct((b, t, 128), F32)],
        compiler_params=_cparams("parallel", "parallel"),
        name="dn_front",
    )(xs, xs, xs, *params)


def _dn_scan_kernel(qf_ref, kf_ref, vf_ref, gbf_ref, qr_ref, kr_ref, vr_ref, gbr_ref, yf_ref, yr_ref, s_ref):
    c, kdim = CH, DN_HEAD_DIM

    @pl.when(pl.program_id(1) == 0)
    def _():
        s_ref[...] = jnp.zeros_like(s_ref)

    sls = [slice(h * kdim, (h + 1) * kdim) for h in range(DN_HEADS)]
    dirs = ((qf_ref, kf_ref, vf_ref, gbf_ref), (qr_ref, kr_ref, vr_ref, gbr_ref))
    lhs, rhs, ks, qs, kps, bbs, vhs = ([] for _ in range(7))
    for d, (q_ref, k_ref, v_ref, gb_ref) in enumerate(dirs):
        for bi in range(SCAN_NB):
            gb = gb_ref[bi]
            q, k, v = q_ref[bi].astype(F32), k_ref[bi].astype(F32), v_ref[bi]
            for h, sl in enumerate(sls):
                col = d * DN_HEADS + h
                kp = k[:, sl] * gb[:, 2 * DN_HEADS + col:2 * DN_HEADS + col + 1]
                bb = kp * jnp.exp(gb[:, col:col + 1])
                lhs.append(jnp.concatenate([k[:, sl], q[:, sl]], axis=0))
                rhs.append(jnp.concatenate([bb, kp] if h % 2 == 0 else [kp, bb], axis=0))
                ks.append(k[:, sl])
                qs.append(q[:, sl])
                kps.append(kp)
                bbs.append(bb)
                vhs.append(v[:, sl])
    a_raw = _mm_many(lhs, rhs, NT)
    dfull, kkt, rt, khs, bhs, decs = ([] for _ in range(6))
    ti2 = lax.broadcasted_iota(jnp.int32, (c, 2 * c), 0)
    ji2 = lax.broadcasted_iota(jnp.int32, (c, 2 * c), 1) & (c - 1)
    for d, (_, _, _, gb_ref) in enumerate(dirs):
        reverse = d == 1
        _, _, _, incl = _chunk_masks(reverse)
        strict2 = (ji2 > ti2) if reverse else (ji2 < ti2)
        incl2 = jnp.logical_or(strict2, ti2 == ji2)
        inclt2 = (jnp.logical_or(ji2 < ti2, ti2 == ji2) if reverse else jnp.logical_or(ji2 > ti2, ti2 == ji2))
        last = 0 if reverse else c - 1
        gbs = [gb_ref[bi] for bi in range(SCAN_NB)]
        gcols = [_mm(incl.astype(BF16), gb, NN, 1, 2) for gb in gbs]
        grows = [_mm(gb, inclt2.astype(BF16), TN, 2, 1) for gb in gbs]
        for bi in range(SCAN_NB):
            gb, gcol, grow = gbs[bi], gcols[bi], grows[bi]
            for h in range(DN_HEADS):
                i = (d * SCAN_NB + bi) * DN_HEADS + h
                col = d * DN_HEADS + h
                gt = gcol[:, col:col + 1]
                gxt = gt - gb[:, col:col + 1]
                gj = grow[col:col + 1, :]
                gc = gcol[last:last + 1, col:col + 1]
                d3 = jnp.where(incl2, jnp.exp(jnp.where(incl2, gt - gj, 0.0)), 0.0)
                d1 = jnp.where(strict2, jnp.exp(jnp.where(strict2, gxt - gj, 0.0)), 0.0)
                dfull.append(jnp.concatenate([d1, d3], axis=0))
                egc = jnp.exp(gc - gt)
                kkt.append(ks[i] * jnp.exp(gxt))
                rt.append(qs[i] * jnp.exp(gt))
                khs.append(kps[i] * egc)
                bhs.append(bbs[i] * egc)
                decs.append(jnp.exp(gc))
    a = [x * dm for x, dm in zip(a_raw, dfull)]
    ys = _chunk_update_many(s_ref, a, kkt, rt, khs, bhs, vhs, decs, kdim)
    for d, y_ref in enumerate((yf_ref, yr_ref)):
        for bi in range(SCAN_NB):
            for h, sl in enumerate(sls):
                y_ref[bi, :, sl] = ys[(d * SCAN_NB + bi) * DN_HEADS + h].astype(BF16)


def dn_scan(q, k, v, gb):
    b, t, w = q.shape
    assert b % SCAN_NB == 0 and t == T_ALL and w == DN_DIM
    specs = []
    for d in range(2):
        cidx = _chunk_index(d == 1)
        tok = pl.BlockSpec((SCAN_NB, CH, w), lambda i, j, cidx=cidx: (i, cidx(j), 0))
        specs += [tok, tok, tok, pl.BlockSpec((SCAN_NB, CH, 128), lambda i, j, cidx=cidx: (i, cidx(j), 0))]
    out = jax.ShapeDtypeStruct((b, t, w), BF16)
    return pl.pallas_call(
        _dn_scan_kernel,
        grid=(b // SCAN_NB, NC),
        in_specs=specs,
        out_specs=[specs[0], specs[4]],
        out_shape=[out, out],
        scratch_shapes=[pltpu.VMEM((2 * SCAN_NB * DN_HEADS, DN_HEAD_DIM, DN_HEAD_DIM), F32)],
        compiler_params=_cparams("parallel", "arbitrary"),
        name="dn_scan",
    )(q, k, v, gb, q, k, v, gb)


def _dn_tail_kernel(o0_ref, o1_ref, z_ref, x_ref, mod_ref, gnw_ref, bd_ref, wo_ref, nw_ref, w1_ref, w2_ref, fw_ref,
                    o_ref):
    o = o0_ref[0].astype(F32) + o1_ref[0].astype(F32)
    ms = _segsum(o * o, bd_ref[...]) * (1.0 / DN_HEAD_DIM)
    on = o * lax.rsqrt(ms + EPS) * gnw_ref[...]
    z = z_ref[0].astype(F32)
    gated = on * (z * _sigmoid(z))
    m = mod_ref[0, 0]
    xo = x_ref[0] + m[2:3] * _dot(gated.astype(BF16), wo_ref[...])
    o_ref[0] = _mlp_body(xo, m, nw_ref, w1_ref, w2_ref, fw_ref, True)


def dn_tail(o0, o1, z, xs, modv, gnw_tiled, bd128, w_out, nw, w1, w2, fw, layer):
    b, _, d = xs.shape
    tok = pl.BlockSpec((1, TB, DN_DIM), lambda i, j: (i, j, 0))
    full = lambda a: pl.BlockSpec(a.shape, lambda i, j: (0,) * a.ndim)
    params = (gnw_tiled, bd128, w_out, nw.reshape(1, d), w1, w2, fw.reshape(1, d))
    param_specs = [full(a) for a in params]
    param_specs[4] = pl.BlockSpec((None,) + w1.shape[1:], lambda i, j: (layer, 0, 0))
    param_specs[5] = pl.BlockSpec((None,) + w2.shape[1:], lambda i, j: (layer, 0, 0))
    return pl.pallas_call(
        _dn_tail_kernel,
        grid=(b, NB_LAT),
        in_specs=[tok, tok, tok, pl.BlockSpec((1, TB, d), lambda i, j: (i, j, 0)),
                  pl.BlockSpec((1, 1, ADA_CHUNKS, d), lambda i, j: (i, 0, 0, 0))] + param_specs,
        out_specs=pl.BlockSpec((1, TB, d), lambda i, j: (i, j, 0)),
        out_shape=jax.ShapeDtypeStruct((b, SEQ, d), F32),
        compiler_params=_cparams("parallel", "parallel"),
        name="dn_tail",
    )(o0, o1, z, xs, modv, *params)


def _block_diag_ones(head):
    i = jnp.arange(128) // head
    return (i[:, None] == i[None, :]).astype(BF16)


def kernel(x, c, ctx, c_ctx, ada_w, ada_b, norm_mix, norm_mlp, mlp_w1, mlp_w2, final_norm, ab_w_in, ab_w_out, rw_mu, rw_w0, rw_w_up, rw_a0, rw_a_up, rw_g_up, rw_k_k, rw_k_a, rw_r_k, rw_ln_w, rw_ln_b, hy_conv_w, hy_conv_b, hy_f_w1, hy_f_b1, hy_f_w2, hy_f_b2, hy_f_w3, hy_f_b3, hy_f_w4, hy_freq, hy_skip, dn_w_in, dn_conv_w, dn_A_log, dn_dt_bias, dn_norm, dn_w_out):
    bsz = x.shape[0]
    d = D_MODEL
    w = RW_WIDTH
    row = lambda a: a.reshape(1, -1)

    cs = jnp.concatenate([c, c_ctx[None, :], jnp.zeros((16 - bsz - 1, d), F32)], axis=0)
    mod = ada_modulation(cs, ada_w, ada_b)

    def mod_vectors(layer):
        lat = mod[layer, :bsz].reshape(bsz, 1, ADA_CHUNKS, d)
        cx = jnp.broadcast_to(mod[layer, bsz].reshape(1, 1, ADA_CHUNKS, d), (bsz, 1, ADA_CHUNKS, d))
        return jnp.concatenate([lat, cx], axis=1)

    w1_all, w2_all = mlp_w1.astype(BF16), mlp_w2.astype(BF16)
    bd64 = _block_diag_ones(RW_HEAD_DIM)
    bd128 = _block_diag_ones(DN_HEAD_DIM)

    modv = mod_vectors(0)
    zeros = jnp.zeros((64, 2 * w), F32)
    lora = jnp.concatenate([jnp.concatenate([rw_w_up[0, 0], rw_w_up[0, 1], zeros], axis=1),
                            jnp.concatenate([zeros, rw_a_up[0, 0], rw_a_up[0, 1]], axis=1)], axis=0)
    r, v, kk, g, bv, lw, kd, bb, s, x0 = ab_front(
        x, ctx, norm_mix[0], modv, ab_w_in[0].astype(BF16), row(rw_mu[0]), lora.astype(BF16), row(rw_w0[0]),
        row(rw_a0[0]), rw_g_up[0].astype(BF16), row(rw_k_k[0]), row(rw_k_a[0]), row(rw_r_k[0]), bd64,
        hy_conv_w[0], row(hy_conv_b[0]))
    y0, y1 = rwkv_scan(r, v, kk, lw, kd, bb)
    filt = (hy_f_w1[0], hy_f_b1[0], hy_f_w2[0], hy_f_b2[0], hy_f_w3[0], hy_f_b3[0], hy_f_w4[0], hy_freq[0])
    convs = []
    for length, block in ((SEQ, 0), (CTX_LEN, SEQ // CTX_LEN)):
        wf, wft = dft_matrices(length)
        hspec = filter_spectrum(wf, hyena_filter_halves(length, *filt))
        convs.append(hyena_long_conv(s, wf, wft, hspec, length, block))
    xs = ab_tail(y0, y1, bv, g, x0, convs[0], convs[1], s, x, ctx, modv, row(rw_ln_w[0]), row(rw_ln_b[0]),
                 row(hy_skip[0]), bd64, ab_w_out[0].astype(BF16), norm_mlp[0], w1_all, w2_all, final_norm, 0)

    modv = mod_vectors(1)
    w_in = jnp.pad(dn_w_in[0], ((0, 0), (0, DN_COLS_PAD - DN_COLS))).astype(BF16)
    pad_row = lambda a: jnp.pad(a.reshape(1, -1), ((0, 0), (0, 128 - 2 * DN_HEADS)))
    q, k, vv, z, gb = dn_front(xs, norm_mix[1], modv, w_in, dn_conv_w[0], pad_row(dn_A_log[0]),
                               pad_row(dn_dt_bias[0]), bd128)
    o0, o1 = dn_scan(q, k, vv, gb)
    return dn_tail(o0, o1, z, xs, modv, row(jnp.tile(dn_norm[0], DN_HEADS)), bd128, dn_w_out[0].astype(BF16),
                   norm_mlp[1], w1_all, w2_all, final_norm, 1)
```

```python
import functools
import math

import jax
import jax.numpy as jnp
from jax import lax
from jax.experimental import pallas as pl
from jax.experimental.pallas import tpu as pltpu

F32, BF16 = jnp.float32, jnp.bfloat16

D_MODEL = 1024
SEQ = 2048
CTX_LEN = 256
ADA_CHUNKS = 6
EPS = 1e-6
RW_WIDTH = 512
RW_HEAD_DIM = 64
RW_HEADS = 8
RW_COLS = 1792
RW_GN_EPS = 64e-5
HY_WIDTH = 512
HY_COLS = 1536
HY_BANDS = 16
HY_EMB = 33
HY_ORDER = 64
HY_FAST_DECAY = 0.3
HY_SLOW_DECAY = 1.5
HY_TARGET = 1e-2
AB_COLS = RW_COLS + HY_COLS
DN_HEADS = 8
DN_HEAD_DIM = 128
DN_DIM = 1024
DN_COLS = 4 * DN_DIM + 4 * DN_HEADS
DN_COLS_PAD = 4 * DN_DIM + 128

TB = 256
CH = 64
T_ALL = SEQ + CTX_LEN
NB = T_ALL // TB
NB_LAT = SEQ // TB
NC = T_ALL // CH
NC_LAT = SEQ // CH
FB_MAX = 512
CONV_NB = 2
SCAN_NB = 4
RW_SCAN_NB = 8
VMEM_LIMIT = 56 * 1024 * 1024

NN = ((1,), (0,))
NT = ((1,), (1,))
TN = ((0,), (0,))


def _dot(a, b, dims=NN):
    return lax.dot_general(a, b, (dims, ((), ())), preferred_element_type=F32)


def _split(x):
    hi = x.astype(BF16)
    lo = (x - hi.astype(F32)).astype(BF16)
    return hi, lo


def _mm(a, b, dims=NN, pa=1, pb=1):
    if a.dtype == BF16:
        a_hi, a_lo, pa = a, None, 1
    elif pa == 2:
        a_hi, a_lo = _split(a)
    else:
        a_hi, a_lo = a.astype(BF16), None
    if b.dtype == BF16:
        b_hi, b_lo, pb = b, None, 1
    elif pb == 2:
        b_hi, b_lo = _split(b)
    else:
        b_hi, b_lo = b.astype(BF16), None
    out = _dot(a_hi, b_hi, dims)
    if pa == 2:
        out = out + _dot(a_lo, b_hi, dims)
    if pb == 2:
        out = out + _dot(a_hi, b_lo, dims)
    return out


def _segsum(x, bd):
    parts = [_dot(x[:, g * 128:(g + 1) * 128].astype(BF16), bd) for g in range(x.shape[1] // 128)]
    return jnp.concatenate(parts, axis=1)


def _sigmoid(x):
    return 1.0 / (1.0 + jnp.exp(-x))


def _softplus(x):
    return jnp.maximum(x, 0.0) + jnp.log(1.0 + jnp.exp(-jnp.abs(x)))


def _rms_mod(x, nw, shift, scale):
    y = x * lax.rsqrt(jnp.mean(x * x, axis=-1, keepdims=True) + EPS)
    return (y * nw) * (1.0 + scale) + shift


def _cparams(*sem):
    return pltpu.CompilerParams(dimension_semantics=sem, vmem_limit_bytes=VMEM_LIMIT)


ADA_TN = 1536


def _ada_kernel(c_ref, w_ref, b_ref, o_ref):
    c = c_ref[...]
    a = c * _sigmoid(c)
    a_hi, a_lo = _split(a)
    w_hi, w_lo = _split(w_ref[0])
    rows = a.shape[0]
    both = _dot(jnp.concatenate([a_hi, a_lo], axis=0), w_hi)
    o_ref[0] = both[:rows] + both[rows:] + _dot(a_hi, w_lo) + b_ref[0]


def ada_modulation(cs, ada_w, ada_b):
    depth, d, n = ada_w.shape
    rows = cs.shape[0]
    return pl.pallas_call(
        _ada_kernel,
        grid=(depth, n // ADA_TN),
        in_specs=[pl.BlockSpec((rows, d), lambda l, j: (0, 0)),
                  pl.BlockSpec((1, d, ADA_TN), lambda l, j: (l, 0, j)),
                  pl.BlockSpec((1, 1, ADA_TN), lambda l, j: (l, 0, j))],
        out_specs=pl.BlockSpec((1, rows, ADA_TN), lambda l, j: (l, 0, j)),
        out_shape=jax.ShapeDtypeStruct((depth, rows, n), F32),
        compiler_params=_cparams("parallel", "parallel"),
        name="ada_modulation",
    )(cs, ada_w, ada_b.reshape(depth, 1, n))


def _token_block(x_ref, ctx_ref):
    return jnp.where(pl.program_id(1) == NB - 1, ctx_ref[0], x_ref[0])


def _split_specs(d):
    return [pl.BlockSpec((1, TB, d), lambda i, j: (i, jnp.minimum(j, NB_LAT - 1), 0)),
            pl.BlockSpec((1, TB, d), lambda i, j: (i, 0, 0))]


def _halo_specs(width, n_row_blocks):
    prev = pl.BlockSpec((1, 8, width), lambda i, j: (i, jnp.maximum(j * (TB // 8) - 1, 0), 0))
    nxt = pl.BlockSpec((1, 8, width), lambda i, j: (i, jnp.minimum((j + 1) * (TB // 8), n_row_blocks - 1), 0))
    return [prev, nxt]


def _project_with_neighbours(x, prev_ref, next_ref, nw_ref, mod_ref, w_ref):
    j = pl.program_id(1)
    m = mod_ref[0, 0]
    x_ext = jnp.concatenate([x, prev_ref[0], next_ref[0]], axis=0)
    h = _rms_mod(x_ext, nw_ref[...], m[0:1], m[1:2])
    p_ext = _dot(h.astype(BF16), w_ref[...])
    p = p_ext[:TB]
    pv = jnp.where(jnp.logical_and(j != 0, j != NB - 1), 1.0, 0.0).astype(F32)
    nv = jnp.where(j < NB - 2, 1.0, 0.0).astype(F32)
    prow = p_ext[TB + 7:TB + 8] * pv
    nrow = p_ext[TB + 8:TB + 9] * nv
    rows = lax.broadcasted_iota(jnp.int32, (8, 1), 0)
    pprev = pltpu.roll(p, 1, 0)
    pprev = jnp.concatenate([jnp.where(rows == 0, prow, pprev[:8]), pprev[8:]], axis=0)
    pnext = pltpu.roll(p, TB - 1, 0)
    pnext = jnp.concatenate([pnext[:TB - 8], jnp.where(rows == 7, nrow, pnext[TB - 8:])], axis=0)
    return p, pprev, pnext


def _ab_front_kernel(x_ref, ctx_ref, xp_ref, xn_ref, nw_ref, mod_ref, w_ref,
                     mu_ref, lora_ref, w0_ref, a0_ref, gup_ref, kk_ref, ka_ref, rk_ref, bd_ref, cw_ref, cb_ref,
                     r_ref, v_ref, kkn_ref, g_ref, bv_ref, lw_ref, kd_ref, b_ref, s_ref, x0_ref):
    x, xprev, xnext = _project_with_neighbours(_token_block(x_ref, ctx_ref), xp_ref, xn_ref, nw_ref, mod_ref, w_ref)
    w = RW_WIDTH
    xr = x[:, :RW_COLS]
    ps = xr + mu_ref[...] * (0.5 * (xprev[:, :RW_COLS] + xnext[:, :RW_COLS]) - xr)
    r, k, v = ps[:, :w], ps[:, w:2 * w], ps[:, 2 * w:3 * w]
    slab = ps[:, 3 * w:3 * w + 128]
    gl = ps[:, 3 * w + 128:3 * w + 256]
    lane = lax.broadcasted_iota(jnp.int32, (1, 128), 1)
    z = jnp.where(lane < 64, jnp.tanh(slab), slab)
    lora = _dot(z.astype(BF16), lora_ref[...])
    bd = bd_ref[...]
    kkraw = k * kk_ref[...]
    kk = kkraw * lax.rsqrt(_segsum(kkraw * kkraw, bd) + EPS)
    ksum = jnp.zeros_like(k)
    for d in range(2):
        lw_ref[d, 0] = (-math.exp(-0.5)) * _sigmoid(w0_ref[:, d * w:(d + 1) * w] + lora[:, d * w:(d + 1) * w])
        a = _sigmoid(a0_ref[:, d * w:(d + 1) * w] + lora[:, (2 + d) * w:(3 + d) * w])
        kd = k * (1.0 + (a - 1.0) * ka_ref[...])
        kd_ref[d, 0] = kd.astype(BF16)
        b_ref[d, 0] = (a * kk).astype(BF16)
        ksum = ksum + kd
    r_ref[0] = r.astype(BF16)
    v_ref[0] = v.astype(BF16)
    kkn_ref[0] = kk.astype(BF16)
    g_ref[0] = _dot(_sigmoid(gl).astype(BF16), gup_ref[...]).astype(BF16)
    bv_ref[0] = (_segsum(r * ksum * rk_ref[...], bd) * v).astype(BF16)
    u = x[:, RW_COLS:]
    c = (cw_ref[0:1, :] * xprev[:, RW_COLS:] + cw_ref[1:2, :] * u + cw_ref[2:3, :] * xnext[:, RW_COLS:]
         + cb_ref[...])
    hw = HY_WIDTH
    x0_ref[0] = c[:, :hw].astype(BF16)
    s_ref[0] = (c[:, hw:2 * hw] * c[:, 2 * hw:]).astype(BF16)


def ab_front(x, ctx, nw, modv, w_in, mu, lora, w0, a0, gup, k_k, k_a, r_k, bd64, conv_w, conv_b):
    b, _, d = x.shape
    w = RW_WIDTH
    full = lambda a: pl.BlockSpec(a.shape, lambda i, j: (0,) * a.ndim)
    params = (nw.reshape(1, d), modv, w_in, mu, lora, w0, a0, gup, k_k, k_a, r_k, bd64, conv_w, conv_b)
    param_specs = [full(a) for a in params]
    param_specs[1] = pl.BlockSpec((1, 1, ADA_CHUNKS, d), lambda i, j: (i, j // NB_LAT, 0, 0))
    tok = pl.BlockSpec((1, TB, w), lambda i, j: (i, j, 0))
    tok2 = pl.BlockSpec((2, 1, TB, w), lambda i, j: (0, i, j, 0))
    s1 = jax.ShapeDtypeStruct((b, T_ALL, w), BF16)
    s2 = jax.ShapeDtypeStruct((2, b, T_ALL, w), BF16)
    s2f = jax.ShapeDtypeStruct((2, b, T_ALL, w), F32)
    return pl.pallas_call(
        _ab_front_kernel,
        grid=(b, NB),
        in_specs=_split_specs(d) + _halo_specs(d, SEQ // 8) + param_specs,
        out_specs=[tok, tok, tok, tok, tok, tok2, tok2, tok2, tok, tok],
        out_shape=[s1, s1, s1, s1, s1, s2f, s2, s2, s1, s1],
        compiler_params=_cparams("parallel", "parallel"),
        name="ab_front",
    )(x, ctx, x, x, *params)


def _chunk_masks(reverse):
    c = CH
    ti = lax.broadcasted_iota(jnp.int32, (c, c), 0)
    ji = lax.broadcasted_iota(jnp.int32, (c, c), 1)
    strict = (ji > ti) if reverse else (ji < ti)
    incl = jnp.logical_or(strict, ti == ji)
    return ti, ji, strict, incl


def _mm_many(a_list, b_list, dims=NN):
    ops = [(a.astype(BF16), b.astype(BF16)) for a, b in zip(a_list, b_list)]
    return [_dot(a, b, dims) for a, b in ops]


def _tri_inverse_pairs(n_pairs):
    c = CH
    ti = lax.broadcasted_iota(jnp.int32, (c, 2 * c), 0)
    li = lax.broadcasted_iota(jnp.int32, (c, 2 * c), 1)
    ji = li & (c - 1)
    left = li < c

    def block_diag(x):
        xb = x.astype(BF16)
        zero = jnp.zeros_like(xb)
        return jnp.concatenate([jnp.where(left, xb, zero), jnp.where(left, zero, xb)], axis=0)

    eye = (ti == ji).astype(F32)
    pair = (ti >> 1) == (ji >> 1)
    dinvs = [eye - jnp.where(pair, n, 0.0) for n in n_pairs]
    s = 1
    while (2 << s) <= c:
        off = jnp.logical_and((ti >> (s + 1)) == (ji >> (s + 1)), (ti >> s) != (ji >> s))
        coffs = [block_diag(jnp.where(off, n, 0.0)) for n in n_pairs]
        xs = _mm_many(dinvs, coffs)
        corr = _mm_many(xs, [block_diag(d) for d in dinvs])
        dinvs = [d - t for d, t in zip(dinvs, corr)]
        s += 1
    return dinvs


def _chunk_update_many(s_ref, a, kkt, rt, kh, bh, vh, dec_c, kdim):
    n = len(a)
    c = CH
    left = lax.broadcasted_iota(jnp.int32, (c, 2 * c), 1) < c
    tpairs = _tri_inverse_pairs([jnp.where(left, a[i][:c], a[i + 1][:c]) for i in range(0, n, 2)])
    tms = [tpairs[i // 2][:, :c] if i % 2 == 0 else tpairs[i // 2][:, c:] for i in range(n)]
    a_v = [a[i][:, c:] if i % 2 == 0 else a[i][:, :c] for i in range(n)]
    a_rb = [a[i][c:, :c] if i % 2 == 0 else a[i][c:, c:] for i in range(n)]
    av = _mm_many(a_v, vh)
    tw = _mm_many(tms, [jnp.concatenate([kkt[i], av[i][:c]], axis=1) for i in range(n)])
    states = [s_ref[i] for i in range(n)]
    p = _mm_many([jnp.concatenate([tw[i][:, :kdim], rt[i]], axis=0) for i in range(n)], states, NT)
    u = [p[i][:c] + tw[i][:, kdim:] for i in range(n)]
    au = _mm_many(a_rb, u)
    ds = _mm_many([jnp.concatenate([vh[i], -u[i]], axis=0) for i in range(n)],
                  [jnp.concatenate([kh[i], bh[i]], axis=0) for i in range(n)], TN)
    for i in range(n):
        s_ref[i] = states[i] * dec_c[i] + ds[i]
    return [p[i][c:] + av[i][c:] - au[i] for i in range(n)]


def _pair_update_many(s_ref, a, kkt, rt, kh, bh, vh, dec_c):
    n = len(kkt)
    c = CH
    left = lax.broadcasted_iota(jnp.int32, (1, 2 * c), 1) < c

    def diag(x, anti=False):
        xb = x.astype(BF16)
        zero = jnp.zeros_like(xb)
        top, bottom = jnp.where(left, xb, zero), jnp.where(left, zero, xb)
        return jnp.concatenate([bottom, top] if anti else [top, bottom], axis=0)

    tps = _tri_inverse_pairs([jnp.where(left, a[2 * i + 1][:c], a[2 * i][:c]) for i in range(n)])
    av = _mm_many([jnp.where(left, a[2 * i], a[2 * i + 1]) for i in range(n)], [diag(x) for x in vh])
    tw = _mm_many(tps, [jnp.concatenate([diag(kkt[i], True), diag(av[i][:c], True)], axis=1) for i in range(n)])
    states = [s_ref[i] for i in range(n)]
    p = _mm_many([jnp.concatenate([tw[i][:, :2 * c], rt[i]], axis=0) for i in range(n)],
                 [diag(x) for x in states], NT)
    u = [p[i][:c] + tw[i][:, 2 * c:] for i in range(n)]
    au = _mm_many([jnp.where(left, a[2 * i + 1][c:], a[2 * i][c:]) for i in range(n)], [diag(x, True) for x in u])
    ds = _mm_many([jnp.concatenate([vh[i], -u[i]], axis=0) for i in range(n)],
                  [jnp.concatenate([kh[i], bh[i]], axis=0) for i in range(n)], TN)
    for i in range(n):
        s_ref[i] = states[i] * dec_c[i] + jnp.where(left, ds[i][:c], ds[i][c:])
    return [p[i][c:] + av[i][c:] - au[i] for i in range(n)]


def _chunk_index(reverse):
    if reverse:
        return lambda i: NC - 1 - i
    return lambda i: (i + NC_LAT) % NC


def _rwkv_scan_kernel(rf_ref, vf_ref, kkf_ref, rr_ref, vr_ref, kkr_ref, lwf_ref, kdf_ref, bf_ref,
                      lwr_ref, kdr_ref, br_ref, yf_ref, yr_ref, s_ref):
    c, kdim = CH, RW_HEAD_DIM

    @pl.when(pl.program_id(1) == 0)
    def _():
        s_ref[...] = jnp.zeros_like(s_ref)

    sls = [slice(m * 2 * kdim, (m + 1) * 2 * kdim) for m in range(RW_HEADS // 2)]
    slabs = lambda x: [x[:, sl] for sl in sls]
    left = lax.broadcasted_iota(jnp.int32, (1, 2 * kdim), 1) < kdim
    ri = lax.broadcasted_iota(jnp.int32, (2 * c, 2 * c), 0)
    ci = lax.broadcasted_iota(jnp.int32, (2 * c, 2 * c), 1)
    rt_, ct_ = ri & (c - 1), ci & (c - 1)
    lhs, rhs, amasks, kkt, rt, kh, bh, vh, dec = ([] for _ in range(9))
    dirs = ((rf_ref, vf_ref, kkf_ref, lwf_ref, kdf_ref, bf_ref), (rr_ref, vr_ref, kkr_ref, lwr_ref, kdr_ref, br_ref))
    for d, (r_ref, v_ref, kk_ref, lw_ref, kd_ref, b_ref) in enumerate(dirs):
        reverse = d == 1
        _, _, _, incl = _chunk_masks(reverse)
        before = (ct_ > rt_) if reverse else (ct_ < rt_)
        amask = jnp.logical_or(before, jnp.logical_and(rt_ == ct_, ri >= c))
        last = 0 if reverse else c - 1
        lws = [lw_ref[0, bi] for bi in range(RW_SCAN_NB)]
        gs = [_mm(incl.astype(BF16), lw, NN, 1, 2) for lw in lws]
        for bi in range(RW_SCAN_NB):
            lw, g = lws[bi], gs[bi]
            gc = g[last:last + 1, :]
            eng = jnp.exp(-g)
            egc = jnp.exp(gc - g)
            kd, bb = kd_ref[0, bi].astype(F32), b_ref[0, bi].astype(F32)
            kkt_all = kk_ref[bi].astype(F32) * jnp.exp(g - lw)
            rt_all = r_ref[bi].astype(F32) * jnp.exp(g)
            for x, y, kn, bn in zip(slabs(kkt_all), slabs(rt_all), slabs(kd * eng), slabs(bb * eng)):
                both = jnp.concatenate([x, y], axis=0)
                lhs += [jnp.where(left, both, 0.0), jnp.where(left, 0.0, both)]
                rhs += [jnp.concatenate([kn, bn], axis=0), jnp.concatenate([bn, kn], axis=0)]
            amasks += [amask] * RW_HEADS
            kkt += slabs(kkt_all)
            rt += slabs(rt_all)
            kh += slabs(kd * egc)
            bh += slabs(bb * egc)
            vh += slabs(v_ref[bi])
            dec += slabs(jnp.exp(gc))
    a = [jnp.where(m, x, 0.0) for m, x in zip(amasks, _mm_many(lhs, rhs, NT))]
    ys = _pair_update_many(s_ref, a, kkt, rt, kh, bh, vh, dec)
    for d, y_ref in enumerate((yf_ref, yr_ref)):
        for bi in range(RW_SCAN_NB):
            for m, sl in enumerate(sls):
                y_ref[bi, :, sl] = ys[(d * RW_SCAN_NB + bi) * len(sls) + m].astype(BF16)


def rwkv_scan(r, v, kk, lw, kd, bb):
    b, t, w = r.shape
    assert b % RW_SCAN_NB == 0 and t == T_ALL and w == RW_WIDTH
    specs, specs_d = [], []
    for d in range(2):
        cidx = _chunk_index(d == 1)
        specs += [pl.BlockSpec((RW_SCAN_NB, CH, w), lambda i, j, cidx=cidx: (i, cidx(j), 0))] * 3
        specs_d += [pl.BlockSpec((1, RW_SCAN_NB, CH, w), lambda i, j, cidx=cidx, d=d: (d, i, cidx(j), 0))] * 3
    out = jax.ShapeDtypeStruct((b, t, w), BF16)
    return pl.pallas_call(
        _rwkv_scan_kernel,
        grid=(b // RW_SCAN_NB, NC),
        in_specs=specs + specs_d,
        out_specs=[specs[0], specs[3]],
        out_shape=[out, out],
        scratch_shapes=[pltpu.VMEM((RW_SCAN_NB * RW_HEADS, RW_HEAD_DIM, 2 * RW_HEAD_DIM), F32)],
        compiler_params=_cparams("parallel", "arbitrary"),
        name="rwkv_scan",
    )(r, v, kk, r, v, kk, lw, kd, bb, lw, kd, bb)


def _hy_filter_kernel(z_ref, t_ref, w1_ref, b1_ref, w2_ref, b2_ref, w3_ref, b3_ref, w4_ref, fr_ref, dl_ref, o_ref):
    fr = fr_ref[...]
    h = jnp.sin(fr * (_mm(z_ref[...], w1_ref[...], NN, 2, 2) + b1_ref[...]))
    h = jnp.sin(fr * (_mm(h, w2_ref[...], NN, 2, 2) + b2_ref[...]))
    h = jnp.sin(fr * (_mm(h, w3_ref[...], NN, 2, 2) + b3_ref[...]))
    h = _mm(h, w4_ref[...], NN, 2, 2)
    h = h * jnp.exp(-t_ref[...] * dl_ref[...])
    hw = HY_WIDTH
    rows = lax.broadcasted_iota(jnp.int32, (h.shape[0], 1), 0)
    hf = h[:, :hw]
    hb = jnp.where(rows == 0, 0.0, h[:, hw:])
    norm = jnp.sum(jnp.abs(hf), axis=0, keepdims=True) + jnp.sum(jnp.abs(hb), axis=0, keepdims=True)
    inv_norm = 1.0 / norm
    o_ref[:, :hw] = (hf + hb) * inv_norm
    o_ref[:, hw:] = (hf - hb) * inv_norm


def hyena_filter_halves(length, w1, b1, w2, b2, w3, b3, w4, freq):
    t = jnp.linspace(0.0, 1.0, length, dtype=F32)[:, None]
    wv = 2 * math.pi * jnp.arange(length, dtype=F32)[:, None] / length
    f = jnp.linspace(1e-4, HY_BANDS - 1, HY_BANDS, dtype=F32)[None, :]
    z = jnp.concatenate([t, jnp.cos(f * wv), -jnp.sin(f * wv)], axis=-1)
    z = jnp.pad(z, ((0, 0), (0, HY_ORDER - HY_EMB)))
    w1p = jnp.pad(w1, ((0, HY_ORDER - HY_EMB), (0, 0)))
    deltas = jnp.abs(jnp.linspace(math.log(HY_TARGET) / HY_SLOW_DECAY, math.log(HY_TARGET) / HY_FAST_DECAY,
                                  HY_WIDTH, dtype=F32))
    dl = jnp.tile(deltas, 2)[None, :]
    row = lambda a: a.reshape(1, -1)
    return pl.pallas_call(
        _hy_filter_kernel,
        out_shape=jax.ShapeDtypeStruct((length, 2 * HY_WIDTH), F32),
        compiler_params=pltpu.CompilerParams(vmem_limit_bytes=VMEM_LIMIT),
        name="hyena_filter",
    )(z, t, w1p, row(b1), w2, row(b2), w3, row(b3), w4, row(freq), dl)


def _freq_block(length):
    return min(FB_MAX, length)


def dft_matrices(length):
    n = 2 * length
    FB = _freq_block(length)
    nfb = length // FB
    t = jnp.arange(length, dtype=jnp.int32)
    ang = lambda f: ((f[:, None] * t[None, :]) % n).astype(F32) * (2 * math.pi / n)
    a_hi = ang(jnp.arange(nfb, dtype=jnp.int32) * FB)
    a_lo = ang(jnp.arange(FB, dtype=jnp.int32))
    nyq = jnp.where(t % 2 == 0, 1.0, -1.0).astype(F32)

    def build(ch, sh, cl, sl, first, nyq_b, axis):
        cosm = ch * cl - sh * sl
        msin = -(sh * cl + ch * sl)
        msin = jnp.where(first, nyq_b, msin)
        return jnp.concatenate([cosm, msin], axis=axis).astype(BF16)

    bi = lax.broadcasted_iota
    first = jnp.logical_and(bi(jnp.int32, (nfb, FB, 1), 0) == 0, bi(jnp.int32, (nfb, FB, 1), 1) == 0)
    wf = build(jnp.cos(a_hi)[:, None, :], jnp.sin(a_hi)[:, None, :], jnp.cos(a_lo)[None], jnp.sin(a_lo)[None],
               first, nyq[None, None, :], 1).reshape(n, length)
    first_t = jnp.logical_and(bi(jnp.int32, (1, nfb, FB), 1) == 0, bi(jnp.int32, (1, nfb, FB), 2) == 0)
    wft = build(jnp.cos(a_hi).T[:, :, None], jnp.sin(a_hi).T[:, :, None], jnp.cos(a_lo).T[:, None, :],
                jnp.sin(a_lo).T[:, None, :], first_t, nyq[:, None, None], 2).reshape(length, n)
    return wf, wft


def _spectrum_kernel(w_ref, h_ref, o_ref):
    o_ref[...] = _dot(w_ref[...], h_ref[...].astype(BF16))


def filter_spectrum(wf, hpm):
    n, length = wf.shape
    cols = hpm.shape[1]
    FB = _freq_block(length)
    return pl.pallas_call(
        _spectrum_kernel,
        grid=(n // (2 * FB),),
        in_specs=[pl.BlockSpec((2 * FB, length), lambda i: (i, 0)),
                  pl.BlockSpec((length, cols), lambda i: (0, 0))],
        out_specs=pl.BlockSpec((2 * FB, cols), lambda i: (i, 0)),
        out_shape=jax.ShapeDtypeStruct((n, cols), F32),
        compiler_params=_cparams("parallel"),
        name="filter_spectrum",
    )(wf, hpm)


def _hy_conv_kernel(s_ref, wf_ref, wft_ref, h_ref, o_ref, *, n_fft):
    fb = pl.program_id(1)
    FB = wf_ref.shape[0] // 2

    @pl.when(fb == 0)
    def _():
        o_ref[...] = jnp.zeros_like(o_ref)

    hw = HY_WIDTH
    hraw = h_ref[...]
    hr, hi, hn = hraw[:FB, :hw], hraw[FB:, hw:], hraw[FB:, :hw]
    fidx = lax.broadcasted_iota(jnp.int32, (FB, 1), 0) + fb * FB
    m = (fidx != 0).astype(F32)
    cf = jnp.where(fidx != 0, 2.0 / n_fft, 1.0 / n_fft).astype(F32)
    hr_c = hr * cf
    him = hi * m * cf
    hc = (hr * m + hn * (1.0 - m)) * cf
    for bi in range(s_ref.shape[0]):
        x = _dot(wf_ref[...], s_ref[bi])
        xr, xi = x[:FB], x[FB:]
        y = jnp.concatenate([xr * hr_c - xi * him, xr * him + xi * hc], axis=0).astype(BF16)
        o_ref[bi] += _dot(wft_ref[...], y)


def hyena_long_conv(s, wf, wft, hspec, length, block):
    b = s.shape[0]
    hw = HY_WIDTH
    FB = _freq_block(length)
    assert b % CONV_NB == 0 and length % FB == 0
    return pl.pallas_call(
        functools.partial(_hy_conv_kernel, n_fft=2 * length),
        grid=(b // CONV_NB, length // FB),
        in_specs=[pl.BlockSpec((CONV_NB, length, hw), lambda i, j: (i, block, 0)),
                  pl.BlockSpec((2 * FB, length), lambda i, j: (j, 0)),
                  pl.BlockSpec((length, 2 * FB), lambda i, j: (0, j)),
                  pl.BlockSpec((2 * FB, 2 * hw), lambda i, j: (j, 0))],
        out_specs=pl.BlockSpec((CONV_NB, length, hw), lambda i, j: (i, 0, 0)),
        out_shape=jax.ShapeDtypeStruct((b, length, hw), F32),
        compiler_params=_cparams("parallel", "arbitrary"),
        name="hyena_long_conv",
    )(s, wf, wft, hspec)


MLP_HC = 1024


def _mlp_body(x, m, nw_ref, w1_ref, w2_ref, fw_ref, final):
    h = _rms_mod(x, nw_ref[...], m[3:4], m[4:5]).astype(BF16)
    acc = jnp.zeros_like(x)
    for c in range(w1_ref.shape[1] // MLP_HC):
        hid = _dot(h, w1_ref[:, c * MLP_HC:(c + 1) * MLP_HC])
        hid = jnp.square(jnp.maximum(hid, 0.0))
        acc = acc + _dot(hid.astype(BF16), w2_ref[c * MLP_HC:(c + 1) * MLP_HC, :])
    o = x + m[5:6] * acc
    if final:
        o = o * lax.rsqrt(jnp.mean(o * o, axis=-1, keepdims=True) + EPS) * fw_ref[...]
    return o


def _ab_tail_kernel(y0_ref, y1_ref, bv_ref, g_ref, x0_ref, cvl_ref, cvc_ref, s_ref, x_ref, ctx_ref, mod_ref,
                    lnw_ref, lnb_ref, skip_ref, bd_ref, wo_ref, nw_ref, w1_ref, w2_ref, fw_ref, o_ref):
    bd = bd_ref[...]
    inv = 1.0 / RW_HEAD_DIM
    y = y0_ref[0].astype(F32) + y1_ref[0].astype(F32)
    mean = _segsum(y, bd) * inv
    yc = y - mean
    var = _segsum(yc * yc, bd) * inv
    yn = yc * lax.rsqrt(var + RW_GN_EPS)
    a = (yn * lnw_ref[...] + lnb_ref[...] + bv_ref[0].astype(F32)) * g_ref[0].astype(F32)
    s = s_ref[0].astype(F32)
    bh = x0_ref[0].astype(F32) * (_token_block(cvl_ref, cvc_ref) + s * skip_ref[...])
    w = RW_WIDTH
    out = _dot(a.astype(BF16), wo_ref[:w, :]) + _dot(bh.astype(BF16), wo_ref[w:, :])
    m = mod_ref[0, 0]
    xo = _token_block(x_ref, ctx_ref) + m[2:3] * out
    o_ref[0] = _mlp_body(xo, m, nw_ref, w1_ref, w2_ref, fw_ref, False)


def ab_tail(y0, y1, bv, g, x0, cv_lat, cv_ctx, s, x, ctx, modv, ln_w, ln_b, skip, bd64, w_out, nw, w1, w2, fw,
            layer):
    b, _, d = x.shape
    w = RW_WIDTH
    tok = pl.BlockSpec((1, TB, w), lambda i, j: (i, j, 0))
    full = lambda a: pl.BlockSpec(a.shape, lambda i, j: (0,) * a.ndim)
    params = (ln_w, ln_b, skip, bd64, w_out, nw.reshape(1, d), w1, w2, fw.reshape(1, d))
    param_specs = [full(a) for a in params]
    param_specs[6] = pl.BlockSpec((None,) + w1.shape[1:], lambda i, j: (layer, 0, 0))
    param_specs[7] = pl.BlockSpec((None,) + w2.shape[1:], lambda i, j: (layer, 0, 0))
    return pl.pallas_call(
        _ab_tail_kernel,
        grid=(b, NB),
        in_specs=[tok] * 5 + _split_specs(w) + [tok] + _split_specs(d)
        + [pl.BlockSpec((1, 1, ADA_CHUNKS, d), lambda i, j: (i, j // NB_LAT, 0, 0))] + param_specs,
        out_specs=pl.BlockSpec((1, TB, d), lambda i, j: (i, j, 0)),
        out_shape=jax.ShapeDtypeStruct((b, T_ALL, d), F32),
        compiler_params=_cparams("parallel", "parallel"),
        name="ab_tail",
    )(y0, y1, bv, g, x0, cv_lat, cv_ctx, s, x, ctx, modv, *params)


def _dn_front_kernel(x_ref, xp_ref, xn_ref, nw_ref, mod_ref, w_ref, cw_ref, alog_ref, dtb_ref, bd_ref,
                     q_ref, k_ref, v_ref, z_ref, gb_ref):
    p, pprev, pnext = _project_with_neighbours(x_ref[0], xp_ref, xn_ref, nw_ref, mod_ref, w_ref)
    wq = 3 * DN_DIM
    c = cw_ref[0:1, :] * pprev[:, :wq] + cw_ref[1:2, :] * p[:, :wq] + cw_ref[2:3, :] * pnext[:, :wq]
    c = c * _sigmoid(c)
    dd = DN_DIM
    bd = bd_ref[...]
    q, k = c[:, :dd], c[:, dd:2 * dd]
    q_ref[0] = (q * lax.rsqrt(_segsum(q * q, bd) + EPS) * (DN_HEAD_DIM ** -0.5)).astype(BF16)
    k_ref[0] = (k * lax.rsqrt(_segsum(k * k, bd) + EPS)).astype(BF16)
    v_ref[0] = c[:, 2 * dd:].astype(BF16)
    z_ref[0] = p[:, wq:wq + dd].astype(BF16)
    slab = p[:, wq + dd:]
    lane = lax.broadcasted_iota(jnp.int32, (1, 128), 1)
    gdec = -jnp.exp(alog_ref[...]) * _softplus(slab + dtb_ref[...])
    gb_ref[0] = jnp.where(lane < 2 * DN_HEADS, gdec, _sigmoid(slab))


def dn_front(xs, nw, modv, w_in, conv_w, alog_row, dtb_row, bd128):
    b, t, d = xs.shape
    full = lambda a: pl.BlockSpec(a.shape, lambda i, j: (0,) * a.ndim)
    params = (nw.reshape(1, d), modv, w_in, conv_w, alog_row, dtb_row, bd128)
    param_specs = [full(a) for a in params]
    param_specs[1] = pl.BlockSpec((1, 1, ADA_CHUNKS, d), lambda i, j: (i, j // NB_LAT, 0, 0))
    tok = pl.BlockSpec((1, TB, DN_DIM), lambda i, j: (i, j, 0))
    s1 = jax.ShapeDtypeStruct((b, t, DN_DIM), BF16)
    return pl.pallas_call(
        _dn_front_kernel,
        grid=(b, t // TB),
        in_specs=[pl.BlockSpec((1, TB, d), lambda i, j: (i, j, 0))] + _halo_specs(d, t // 8) + param_specs,
        out_specs=[tok, tok, tok, tok, pl.BlockSpec((1, TB, 128), lambda i, j: (i, j, 0))],
        out_shape=[s1, s1, s1, s1, jax.ShapeDtypeStruct((b, t, 128), F32)],
        compiler_params=_cparams("parallel", "parallel"),
        name="dn_front",
    )(xs, xs, xs, *params)


def _dn_scan_kernel(qf_ref, kf_ref, vf_ref, gbf_ref, qr_ref, kr_ref, vr_ref, gbr_ref, yf_ref, yr_ref, s_ref):
    c, kdim = CH, DN_HEAD_DIM

    @pl.when(pl.program_id(1) == 0)
    def _():
        s_ref[...] = jnp.zeros_like(s_ref)

    sls = [slice(h * kdim, (h + 1) * kdim) for h in range(DN_HEADS)]
    dirs = ((qf_ref, kf_ref, vf_ref, gbf_ref), (qr_ref, kr_ref, vr_ref, gbr_ref))
    lhs, rhs, ks, qs, kps, bbs, vhs = ([] for _ in range(7))
    for d, (q_ref, k_ref, v_ref, gb_ref) in enumerate(dirs):
        for bi in range(SCAN_NB):
            gb = gb_ref[bi]
            q, k, v = q_ref[bi].astype(F32), k_ref[bi].astype(F32), v_ref[bi]
            for h, sl in enumerate(sls):
                col = d * DN_HEADS + h
                kp = k[:, sl] * gb[:, 2 * DN_HEADS + col:2 * DN_HEADS + col + 1]
                bb = kp * jnp.exp(gb[:, col:col + 1])
                lhs.append(jnp.concatenate([k[:, sl], q[:, sl]], axis=0))
                rhs.append(jnp.concatenate([bb, kp] if h % 2 == 0 else [kp, bb], axis=0))
                ks.append(k[:, sl])
                qs.append(q[:, sl])
                kps.append(kp)
                bbs.append(bb)
                vhs.append(v[:, sl])
    a_raw = _mm_many(lhs, rhs, NT)
    dfull, kkt, rt, khs, bhs, decs = ([] for _ in range(6))
    ti2 = lax.broadcasted_iota(jnp.int32, (c, 2 * c), 0)
    ji2 = lax.broadcasted_iota(jnp.int32, (c, 2 * c), 1) & (c - 1)
    for d, (_, _, _, gb_ref) in enumerate(dirs):
        reverse = d == 1
        _, _, _, incl = _chunk_masks(reverse)
        strict2 = (ji2 > ti2) if reverse else (ji2 < ti2)
        incl2 = jnp.logical_or(strict2, ti2 == ji2)
        inclt2 = (jnp.logical_or(ji2 < ti2, ti2 == ji2) if reverse else jnp.logical_or(ji2 > ti2, ti2 == ji2))
        last = 0 if reverse else c - 1
        gbs = [gb_ref[bi] for bi in range(SCAN_NB)]
        gcols = [_mm(incl.astype(BF16), gb, NN, 1, 2) for gb in gbs]
        grows = [_mm(gb, inclt2.astype(BF16), TN, 2, 1) for gb in gbs]
        for bi in range(SCAN_NB):
            gb, gcol, grow = gbs[bi], gcols[bi], grows[bi]
            for h in range(DN_HEADS):
                i = (d * SCAN_NB + bi) * DN_HEADS + h
                col = d * DN_HEADS + h
                gt = gcol[:, col:col + 1]
                gxt = gt - gb[:, col:col + 1]
                gj = grow[col:col + 1, :]
                gc = gcol[last:last + 1, col:col + 1]
                d3 = jnp.where(incl2, jnp.exp(jnp.where(incl2, gt - gj, 0.0)), 0.0)
                d1 = jnp.where(strict2, jnp.exp(jnp.where(strict2, gxt - gj, 0.0)), 0.0)
                dfull.append(jnp.concatenate([d1, d3], axis=0))
                egc = jnp.exp(gc - gt)
                kkt.append(ks[i] * jnp.exp(gxt))
                rt.append(qs[i] * jnp.exp(gt))
                khs.append(kps[i] * egc)
                bhs.append(bbs[i] * egc)
                decs.append(jnp.exp(gc))
    a = [x * dm for x, dm in zip(a_raw, dfull)]
    ys = _chunk_update_many(s_ref, a, kkt, rt, khs, bhs, vhs, decs, kdim)
    for d, y_ref in enumerate((yf_ref, yr_ref)):
        for bi in range(SCAN_NB):
            for h, sl in enumerate(sls):
                y_ref[bi, :, sl] = ys[(d * SCAN_NB + bi) * DN_HEADS + h].astype(BF16)


def dn_scan(q, k, v, gb):
    b, t, w = q.shape
    assert b % SCAN_NB == 0 and t == T_ALL and w == DN_DIM
    specs = []
    for d in range(2):
        cidx = _chunk_index(d == 1)
        tok = pl.BlockSpec((SCAN_NB, CH, w), lambda i, j, cidx=cidx: (i, cidx(j), 0))
        specs += [tok, tok, tok, pl.BlockSpec((SCAN_NB, CH, 128), lambda i, j, cidx=cidx: (i, cidx(j), 0))]
    out = jax.ShapeDtypeStruct((b, t, w), BF16)
    return pl.pallas_call(
        _dn_scan_kernel,
        grid=(b // SCAN_NB, NC),
        in_specs=specs,
        out_specs=[specs[0], specs[4]],
        out_shape=[out, out],
        scratch_shapes=[pltpu.VMEM((2 * SCAN_NB * DN_HEADS, DN_HEAD_DIM, DN_HEAD_DIM), F32)],
        compiler_params=_cparams("parallel", "arbitrary"),
        name="dn_scan",
    )(q, k, v, gb, q, k, v, gb)


def _dn_tail_kernel(o0_ref, o1_ref, z_ref, x_ref, mod_ref, gnw_ref, bd_ref, wo_ref, nw_ref, w1_ref, w2_ref, fw_ref,
                    o_ref):
    o = o0_ref[0].astype(F32) + o1_ref[0].astype(F32)
    ms = _segsum(o * o, bd_ref[...]) * (1.0 / DN_HEAD_DIM)
    on = o * lax.rsqrt(ms + EPS) * gnw_ref[...]
    z = z_ref[0].astype(F32)
    gated = on * (z * _sigmoid(z))
    m = mod_ref[0, 0]
    xo = x_ref[0] + m[2:3] * _dot(gated.astype(BF16), wo_ref[...])
    o_ref[0] = _mlp_body(xo, m, nw_ref, w1_ref, w2_ref, fw_ref, True)


def dn_tail(o0, o1, z, xs, modv, gnw_tiled, bd128, w_out, nw, w1, w2, fw, layer):
    b, _, d = xs.shape
    tok = pl.BlockSpec((1, TB, DN_DIM), lambda i, j: (i, j, 0))
    full = lambda a: pl.BlockSpec(a.shape, lambda i, j: (0,) * a.ndim)
    params = (gnw_tiled, bd128, w_out, nw.reshape(1, d), w1, w2, fw.reshape(1, d))
    param_specs = [full(a) for a in params]
    param_specs[4] = pl.BlockSpec((None,) + w1.shape[1:], lambda i, j: (layer, 0, 0))
    param_specs[5] = pl.BlockSpec((None,) + w2.shape[1:], lambda i, j: (layer, 0, 0))
    return pl.pallas_call(
        _dn_tail_kernel,
        grid=(b, NB_LAT),
        in_specs=[tok, tok, tok, pl.BlockSpec((1, TB, d), lambda i, j: (i, j, 0)),
                  pl.BlockSpec((1, 1, ADA_CHUNKS, d), lambda i, j: (i, 0, 0, 0))] + param_specs,
        out_specs=pl.BlockSpec((1, TB, d), lambda i, j: (i, j, 0)),
        out_shape=jax.ShapeDtypeStruct((b, SEQ, d), F32),
        compiler_params=_cparams("parallel", "parallel"),
        name="dn_tail",
    )(o0, o1, z, xs, modv, *params)


def _block_diag_ones(head):
    i = jnp.arange(128) // head
    return (i[:, None] == i[None, :]).astype(BF16)


def kernel(x, c, ctx, c_ctx, ada_w, ada_b, norm_mix, norm_mlp, mlp_w1, mlp_w2, final_norm, ab_w_in, ab_w_out, rw_mu, rw_w0, rw_w_up, rw_a0, rw_a_up, rw_g_up, rw_k_k, rw_k_a, rw_r_k, rw_ln_w, rw_ln_b, hy_conv_w, hy_conv_b, hy_f_w1, hy_f_b1, hy_f_w2, hy_f_b2, hy_f_w3, hy_f_b3, hy_f_w4, hy_freq, hy_skip, dn_w_in, dn_conv_w, dn_A_log, dn_dt_bias, dn_norm, dn_w_out):
    bsz = x.shape[0]
    d = D_MODEL
    w = RW_WIDTH
    row = lambda a: a.reshape(1, -1)

    cs = jnp.concatenate([c, c_ctx[None, :], jnp.zeros((16 - bsz - 1, d), F32)], axis=0)
    mod = ada_modulation(cs, ada_w, ada_b)

    def mod_vectors(layer):
        lat = mod[layer, :bsz].reshape(bsz, 1, ADA_CHUNKS, d)
        cx = jnp.broadcast_to(mod[layer, bsz].reshape(1, 1, ADA_CHUNKS, d), (bsz, 1, ADA_CHUNKS, d))
        return jnp.concatenate([lat, cx], axis=1)

    w1_all, w2_all = mlp_w1.astype(BF16), mlp_w2.astype(BF16)
    bd64 = _block_diag_ones(RW_HEAD_DIM)
    bd128 = _block_diag_ones(DN_HEAD_DIM)

    modv = mod_vectors(0)
    zeros = jnp.zeros((64, 2 * w), F32)
    lora = jnp.concatenate([jnp.concatenate([rw_w_up[0, 0], rw_w_up[0, 1], zeros], axis=1),
                            jnp.concatenate([zeros, rw_a_up[0, 0], rw_a_up[0, 1]], axis=1)], axis=0)
    r, v, kk, g, bv, lw, kd, bb, s, x0 = ab_front(
        x, ctx, norm_mix[0], modv, ab_w_in[0].astype(BF16), row(rw_mu[0]), lora.astype(BF16), row(rw_w0[0]),
        row(rw_a0[0]), rw_g_up[0].astype(BF16), row(rw_k_k[0]), row(rw_k_a[0]), row(rw_r_k[0]), bd64,
        hy_conv_w[0], row(hy_conv_b[0]))
    y0, y1 = rwkv_scan(r, v, kk, lw, kd, bb)
    filt = (hy_f_w1[0], hy_f_b1[0], hy_f_w2[0], hy_f_b2[0], hy_f_w3[0], hy_f_b3[0], hy_f_w4[0], hy_freq[0])
    convs = []
    for length, block in ((SEQ, 0), (CTX_LEN, SEQ // CTX_LEN)):
        wf, wft = dft_matrices(length)
        hspec = filter_spectrum(wf, hyena_filter_halves(length, *filt))
        convs.append(hyena_long_conv(s, wf, wft, hspec, length, block))
    xs = ab_tail(y0, y1, bv, g, x0, convs[0], convs[1], s, x, ctx, modv, row(rw_ln_w[0]), row(rw_ln_b[0]),
                 row(hy_skip[0]), bd64, ab_w_out[0].astype(BF16), norm_mlp[0], w1_all, w2_all, final_norm, 0)

    modv = mod_vectors(1)
    w_in = jnp.pad(dn_w_in[0], ((0, 0), (0, DN_COLS_PAD - DN_COLS))).astype(BF16)
    pad_row = lambda a: jnp.pad(a.reshape(1, -1), ((0, 0), (0, 128 - 2 * DN_HEADS)))
    q, k, vv, z, gb = dn_front(xs, norm_mix[1], modv, w_in, dn_conv_w[0], pad_row(dn_A_log[0]),
                               pad_row(dn_dt_bias[0]), bd128)
    o0, o1 = dn_scan(q, k, vv, gb)
    return dn_tail(o0, o1, z, xs, modv, row(jnp.tile(dn_norm[0], DN_HEADS)), bd128, dn_w_out[0].astype(BF16),
                   norm_mlp[1], w1_all, w2_all, final_norm, 1)
```

```python
import functools
import math

import jax
import jax.numpy as jnp
from jax import lax
from jax.experimental import pallas as pl
from jax.experimental.pallas import tpu as pltpu

F32, BF16 = jnp.float32, jnp.bfloat16

D_MODEL = 1024
SEQ = 2048
CTX_LEN = 256
ADA_CHUNKS = 6
EPS = 1e-6
RW_WIDTH = 512
RW_HEAD_DIM = 64
RW_HEADS = 8
RW_COLS = 1792
RW_GN_EPS = 64e-5
HY_WIDTH = 512
HY_COLS = 1536
HY_BANDS = 16
HY_EMB = 33
HY_ORDER = 64
HY_FAST_DECAY = 0.3
HY_SLOW_DECAY = 1.5
HY_TARGET = 1e-2
AB_COLS = RW_COLS + HY_COLS
DN_HEADS = 8
DN_HEAD_DIM = 128
DN_DIM = 1024
DN_COLS = 4 * DN_DIM + 4 * DN_HEADS
DN_COLS_PAD = 4 * DN_DIM + 128

TB = 256
CH = 64
T_ALL = SEQ + CTX_LEN
NB = T_ALL // TB
NB_LAT = SEQ // TB
NC = T_ALL // CH
NC_LAT = SEQ // CH
FB_MAX = 256
CONV_NB = 2
SCAN_NB = 4
RW_SCAN_NB = 8
VMEM_LIMIT = 56 * 1024 * 1024

NN = ((1,), (0,))
NT = ((1,), (1,))
TN = ((0,), (0,))


def _dot(a, b, dims=NN):
    return lax.dot_general(a, b, (dims, ((), ())), preferred_element_type=F32)


def _split(x):
    hi = x.astype(BF16)
    lo = (x - hi.astype(F32)).astype(BF16)
    return hi, lo


def _mm(a, b, dims=NN, pa=1, pb=1):
    if a.dtype == BF16:
        a_hi, a_lo, pa = a, None, 1
    elif pa == 2:
        a_hi, a_lo = _split(a)
    else:
        a_hi, a_lo = a.astype(BF16), None
    if b.dtype == BF16:
        b_hi, b_lo, pb = b, None, 1
    elif pb == 2:
        b_hi, b_lo = _split(b)
    else:
        b_hi, b_lo = b.astype(BF16), None
    out = _dot(a_hi, b_hi, dims)
    if pa == 2:
        out = out + _dot(a_lo, b_hi, dims)
    if pb == 2:
        out = out + _dot(a_hi, b_lo, dims)
    return out


def _segsum(x, bd):
    parts = [_dot(x[:, g * 128:(g + 1) * 128].astype(BF16), bd) for g in range(x.shape[1] // 128)]
    return jnp.concatenate(parts, axis=1)


def _sigmoid(x):
    return 1.0 / (1.0 + jnp.exp(-x))


def _softplus(x):
    return jnp.maximum(x, 0.0) + jnp.log(1.0 + jnp.exp(-jnp.abs(x)))


def _rms_mod(x, nw, shift, scale):
    y = x * lax.rsqrt(jnp.mean(x * x, axis=-1, keepdims=True) + EPS)
    return (y * nw) * (1.0 + scale) + shift


def _cparams(*sem):
    return pltpu.CompilerParams(dimension_semantics=sem, vmem_limit_bytes=VMEM_LIMIT)


ADA_TN = 1536


def _ada_kernel(c_ref, w_ref, b_ref, o_ref):
    c = c_ref[...]
    a = c * _sigmoid(c)
    a_hi, a_lo = _split(a)
    w_hi, w_lo = _split(w_ref[0])
    rows = a.shape[0]
    both = _dot(jnp.concatenate([a_hi, a_lo], axis=0), w_hi)
    o_ref[0] = both[:rows] + both[rows:] + _dot(a_hi, w_lo) + b_ref[0]


def ada_modulation(cs, ada_w, ada_b):
    depth, d, n = ada_w.shape
    rows = cs.shape[0]
    return pl.pallas_call(
        _ada_kernel,
        grid=(depth, n // ADA_TN),
        in_specs=[pl.BlockSpec((rows, d), lambda l, j: (0, 0)),
                  pl.BlockSpec((1, d, ADA_TN), lambda l, j: (l, 0, j)),
                  pl.BlockSpec((1, 1, ADA_TN), lambda l, j: (l, 0, j))],
        out_specs=pl.BlockSpec((1, rows, ADA_TN), lambda l, j: (l, 0, j)),
        out_shape=jax.ShapeDtypeStruct((depth, rows, n), F32),
        compiler_params=_cparams("parallel", "parallel"),
        name="ada_modulation",
    )(cs, ada_w, ada_b.reshape(depth, 1, n))


def _token_block(x_ref, ctx_ref):
    return jnp.where(pl.program_id(1) == NB - 1, ctx_ref[0], x_ref[0])


def _split_specs(d):
    return [pl.BlockSpec((1, TB, d), lambda i, j: (i, jnp.minimum(j, NB_LAT - 1), 0)),
            pl.BlockSpec((1, TB, d), lambda i, j: (i, 0, 0))]


def _halo_specs(width, n_row_blocks):
    prev = pl.BlockSpec((1, 8, width), lambda i, j: (i, jnp.maximum(j * (TB // 8) - 1, 0), 0))
    nxt = pl.BlockSpec((1, 8, width), lambda i, j: (i, jnp.minimum((j + 1) * (TB // 8), n_row_blocks - 1), 0))
    return [prev, nxt]


def _project_with_neighbours(x, prev_ref, next_ref, nw_ref, mod_ref, w_ref):
    j = pl.program_id(1)
    m = mod_ref[0, 0]
    x_ext = jnp.concatenate([x, prev_ref[0], next_ref[0]], axis=0)
    h = _rms_mod(x_ext, nw_ref[...], m[0:1], m[1:2])
    p_ext = _dot(h.astype(BF16), w_ref[...])
    p = p_ext[:TB]
    pv = jnp.where(jnp.logical_and(j != 0, j != NB - 1), 1.0, 0.0).astype(F32)
    nv = jnp.where(j < NB - 2, 1.0, 0.0).astype(F32)
    prow = p_ext[TB + 7:TB + 8] * pv
    nrow = p_ext[TB + 8:TB + 9] * nv
    rows = lax.broadcasted_iota(jnp.int32, (8, 1), 0)
    pprev = pltpu.roll(p, 1, 0)
    pprev = jnp.concatenate([jnp.where(rows == 0, prow, pprev[:8]), pprev[8:]], axis=0)
    pnext = pltpu.roll(p, TB - 1, 0)
    pnext = jnp.concatenate([pnext[:TB - 8], jnp.where(rows == 7, nrow, pnext[TB - 8:])], axis=0)
    return p, pprev, pnext


def _ab_front_kernel(x_ref, ctx_ref, xp_ref, xn_ref, nw_ref, mod_ref, w_ref,
                     mu_ref, lora_ref, w0_ref, a0_ref, gup_ref, kk_ref, ka_ref, rk_ref, bd_ref, cw_ref, cb_ref,
                     r_ref, v_ref, kkn_ref, g_ref, bv_ref, lw_ref, kd_ref, b_ref, s_ref, x0_ref):
    x, xprev, xnext = _project_with_neighbours(_token_block(x_ref, ctx_ref), xp_ref, xn_ref, nw_ref, mod_ref, w_ref)
    w = RW_WIDTH
    xr = x[:, :RW_COLS]
    ps = xr + mu_ref[...] * (0.5 * (xprev[:, :RW_COLS] + xnext[:, :RW_COLS]) - xr)
    r, k, v = ps[:, :w], ps[:, w:2 * w], ps[:, 2 * w:3 * w]
    slab = ps[:, 3 * w:3 * w + 128]
    gl = ps[:, 3 * w + 128:3 * w + 256]
    lane = lax.broadcasted_iota(jnp.int32, (1, 128), 1)
    z = jnp.where(lane < 64, jnp.tanh(slab), slab)
    lora = _dot(z.astype(BF16), lora_ref[...])
    bd = bd_ref[...]
    kkraw = k * kk_ref[...]
    kk = kkraw * lax.rsqrt(_segsum(kkraw * kkraw, bd) + EPS)
    ksum = jnp.zeros_like(k)
    for d in range(2):
        lw_ref[d, 0] = (-math.exp(-0.5)) * _sigmoid(w0_ref[:, d * w:(d + 1) * w] + lora[:, d * w:(d + 1) * w])
        a = _sigmoid(a0_ref[:, d * w:(d + 1) * w] + lora[:, (2 + d) * w:(3 + d) * w])
        kd = k * (1.0 + (a - 1.0) * ka_ref[...])
        kd_ref[d, 0] = kd.astype(BF16)
        b_ref[d, 0] = (a * kk).astype(BF16)
        ksum = ksum + kd
    r_ref[0] = r.astype(BF16)
    v_ref[0] = v.astype(BF16)
    kkn_ref[0] = kk.astype(BF16)
    g_ref[0] = _dot(_sigmoid(gl).astype(BF16), gup_ref[...]).astype(BF16)
    bv_ref[0] = (_segsum(r * ksum * rk_ref[...], bd) * v).astype(BF16)
    u = x[:, RW_COLS:]
    c = (cw_ref[0:1, :] * xprev[:, RW_COLS:] + cw_ref[1:2, :] * u + cw_ref[2:3, :] * xnext[:, RW_COLS:]
         + cb_ref[...])
    hw = HY_WIDTH
    x0_ref[0] = c[:, :hw].astype(BF16)
    s_ref[0] = (c[:, hw:2 * hw] * c[:, 2 * hw:]).astype(BF16)


def ab_front(x, ctx, nw, modv, w_in, mu, lora, w0, a0, gup, k_k, k_a, r_k, bd64, conv_w, conv_b):
    b, _, d = x.shape
    w = RW_WIDTH
    full = lambda a: pl.BlockSpec(a.shape, lambda i, j: (0,) * a.ndim)
    params = (nw.reshape(1, d), modv, w_in, mu, lora, w0, a0, gup, k_k, k_a, r_k, bd64, conv_w, conv_b)
    param_specs = [full(a) for a in params]
    param_specs[1] = pl.BlockSpec((1, 1, ADA_CHUNKS, d), lambda i, j: (i, j // NB_LAT, 0, 0))
    tok = pl.BlockSpec((1, TB, w), lambda i, j: (i, j, 0))
    tok2 = pl.BlockSpec((2, 1, TB, w), lambda i, j: (0, i, j, 0))
    s1 = jax.ShapeDtypeStruct((b, T_ALL, w), BF16)
    s2 = jax.ShapeDtypeStruct((2, b, T_ALL, w), BF16)
    s2f = jax.ShapeDtypeStruct((2, b, T_ALL, w), F32)
    return pl.pallas_call(
        _ab_front_kernel,
        grid=(b, NB),
        in_specs=_split_specs(d) + _halo_specs(d, SEQ // 8) + param_specs,
        out_specs=[tok, tok, tok, tok, tok, tok2, tok2, tok2, tok, tok],
        out_shape=[s1, s1, s1, s1, s1, s2f, s2, s2, s1, s1],
        compiler_params=_cparams("parallel", "parallel"),
        name="ab_front",
    )(x, ctx, x, x, *params)


def _chunk_masks(reverse):
    c = CH
    ti = lax.broadcasted_iota(jnp.int32, (c, c), 0)
    ji = lax.broadcasted_iota(jnp.int32, (c, c), 1)
    strict = (ji > ti) if reverse else (ji < ti)
    incl = jnp.logical_or(strict, ti == ji)
    return ti, ji, strict, incl


def _mm_many(a_list, b_list, dims=NN):
    ops = [(a.astype(BF16), b.astype(BF16)) for a, b in zip(a_list, b_list)]
    return [_dot(a, b, dims) for a, b in ops]


def _tri_inverse_pairs(n_pairs):
    c = CH
    ti = lax.broadcasted_iota(jnp.int32, (c, 2 * c), 0)
    li = lax.broadcasted_iota(jnp.int32, (c, 2 * c), 1)
    ji = li & (c - 1)
    left = li < c

    def block_diag(x):
        xb = x.astype(BF16)
        zero = jnp.zeros_like(xb)
        return jnp.concatenate([jnp.where(left, xb, zero), jnp.where(left, zero, xb)], axis=0)

    eye = (ti == ji).astype(F32)
    pair = (ti >> 1) == (ji >> 1)
    dinvs = [eye - jnp.where(pair, n, 0.0) for n in n_pairs]
    s = 1
    while (2 << s) <= c:
        off = jnp.logical_and((ti >> (s + 1)) == (ji >> (s + 1)), (ti >> s) != (ji >> s))
        coffs = [block_diag(jnp.where(off, n, 0.0)) for n in n_pairs]
        xs = _mm_many(dinvs, coffs)
        corr = _mm_many(xs, [block_diag(d) for d in dinvs])
        dinvs = [d - t for d, t in zip(dinvs, corr)]
        s += 1
    return dinvs


def _chunk_update_many(s_ref, a, kkt, rt, kh, bh, vh, dec_c, kdim):
    n = len(a)
    c = CH
    left = lax.broadcasted_iota(jnp.int32, (c, 2 * c), 1) < c
    tpairs = _tri_inverse_pairs([jnp.where(left, a[i][:c], a[i + 1][:c]) for i in range(0, n, 2)])
    tms = [tpairs[i // 2][:, :c] if i % 2 == 0 else tpairs[i // 2][:, c:] for i in range(n)]
    a_v = [a[i][:, c:] if i % 2 == 0 else a[i][:, :c] for i in range(n)]
    a_rb = [a[i][c:, :c] if i % 2 == 0 else a[i][c:, c:] for i in range(n)]
    av = _mm_many(a_v, vh)
    tw = _mm_many(tms, [jnp.concatenate([kkt[i], av[i][:c]], axis=1) for i in range(n)])
    states = [s_ref[i] for i in range(n)]
    p = _mm_many([jnp.concatenate([tw[i][:, :kdim], rt[i]], axis=0) for i in range(n)], states, NT)
    u = [p[i][:c] + tw[i][:, kdim:] for i in range(n)]
    au = _mm_many(a_rb, u)
    ds = _mm_many([jnp.concatenate([vh[i], -u[i]], axis=0) for i in range(n)],
                  [jnp.concatenate([kh[i], bh[i]], axis=0) for i in range(n)], TN)
    for i in range(n):
        s_ref[i] = states[i] * dec_c[i] + ds[i]
    return [p[i][c:] + av[i][c:] - au[i] for i in range(n)]


def _pair_update_many(s_ref, a, kkt, rt, kh, bh, vh, dec_c):
    n = len(kkt)
    c = CH
    left = lax.broadcasted_iota(jnp.int32, (1, 2 * c), 1) < c

    def diag(x, anti=False):
        xb = x.astype(BF16)
        zero = jnp.zeros_like(xb)
        top, bottom = jnp.where(left, xb, zero), jnp.where(left, zero, xb)
        return jnp.concatenate([bottom, top] if anti else [top, bottom], axis=0)

    tps = _tri_inverse_pairs([jnp.where(left, a[2 * i + 1][:c], a[2 * i][:c]) for i in range(n)])
    av = _mm_many([jnp.where(left, a[2 * i], a[2 * i + 1]) for i in range(n)], [diag(x) for x in vh])
    tw = _mm_many(tps, [jnp.concatenate([diag(kkt[i], True), diag(av[i][:c], True)], axis=1) for i in range(n)])
    states = [s_ref[i] for i in range(n)]
    p = _mm_many([jnp.concatenate([tw[i][:, :2 * c], rt[i]], axis=0) for i in range(n)],
                 [diag(x) for x in states], NT)
    u = [p[i][:c] + tw[i][:, 2 * c:] for i in range(n)]
    au = _mm_many([jnp.where(left, a[2 * i + 1][c:], a[2 * i][c:]) for i in range(n)], [diag(x, True) for x in u])
    ds = _mm_many([jnp.concatenate([vh[i], -u[i]], axis=0) for i in range(n)],
                  [jnp.concatenate([kh[i], bh[i]], axis=0) for i in range(n)], TN)
    for i in range(n):
        s_ref[i] = states[i] * dec_c[i] + jnp.where(left, ds[i][:c], ds[i][c:])
    return [p[i][c:] + av[i][c:] - au[i] for i in range(n)]


def _chunk_index(reverse):
    if reverse:
        return lambda i: NC - 1 - i
    return lambda i: (i + NC_LAT) % NC


def _rwkv_scan_kernel(rf_ref, vf_ref, kkf_ref, rr_ref, vr_ref, kkr_ref, lwf_ref, kdf_ref, bf_ref,
                      lwr_ref, kdr_ref, br_ref, yf_ref, yr_ref, s_ref):
    c, kdim = CH, RW_HEAD_DIM

    @pl.when(pl.program_id(1) == 0)
    def _():
        s_ref[...] = jnp.zeros_like(s_ref)

    sls = [slice(m * 2 * kdim, (m + 1) * 2 * kdim) for m in range(RW_HEADS // 2)]
    slabs = lambda x: [x[:, sl] for sl in sls]
    left = lax.broadcasted_iota(jnp.int32, (1, 2 * kdim), 1) < kdim
    ri = lax.broadcasted_iota(jnp.int32, (2 * c, 2 * c), 0)
    ci = lax.broadcasted_iota(jnp.int32, (2 * c, 2 * c), 1)
    rt_, ct_ = ri & (c - 1), ci & (c - 1)
    lhs, rhs, amasks, kkt, rt, kh, bh, vh, dec = ([] for _ in range(9))
    dirs = ((rf_ref, vf_ref, kkf_ref, lwf_ref, kdf_ref, bf_ref), (rr_ref, vr_ref, kkr_ref, lwr_ref, kdr_ref, br_ref))
    for d, (r_ref, v_ref, kk_ref, lw_ref, kd_ref, b_ref) in enumerate(dirs):
        reverse = d == 1
        _, _, _, incl = _chunk_masks(reverse)
        before = (ct_ > rt_) if reverse else (ct_ < rt_)
        amask = jnp.logical_or(before, jnp.logical_and(rt_ == ct_, ri >= c))
        last = 0 if reverse else c - 1
        lws = [lw_ref[0, bi] for bi in range(RW_SCAN_NB)]
        gs = [_mm(incl.astype(BF16), lw, NN, 1, 2) for lw in lws]
        for bi in range(RW_SCAN_NB):
            lw, g = lws[bi], gs[bi]
            gc = g[last:last + 1, :]
            eng = jnp.exp(-g)
            egc = jnp.exp(gc - g)
            kd, bb = kd_ref[0, bi].astype(F32), b_ref[0, bi].astype(F32)
            kkt_all = kk_ref[bi].astype(F32) * jnp.exp(g - lw)
            rt_all = r_ref[bi].astype(F32) * jnp.exp(g)
            for x, y, kn, bn in zip(slabs(kkt_all), slabs(rt_all), slabs(kd * eng), slabs(bb * eng)):
                both = jnp.concatenate([x, y], axis=0)
                lhs += [jnp.where(left, both, 0.0), jnp.where(left, 0.0, both)]
                rhs += [jnp.concatenate([kn, bn], axis=0), jnp.concatenate([bn, kn], axis=0)]
            amasks += [amask] * RW_HEADS
            kkt += slabs(kkt_all)
            rt += slabs(rt_all)
            kh += slabs(kd * egc)
            bh += slabs(bb * egc)
            vh += slabs(v_ref[bi])
            dec += slabs(jnp.exp(gc))
    a = [jnp.where(m, x, 0.0) for m, x in zip(amasks, _mm_many(lhs, rhs, NT))]
    ys = _pair_update_many(s_ref, a, kkt, rt, kh, bh, vh, dec)
    for d, y_ref in enumerate((yf_ref, yr_ref)):
        for bi in range(RW_SCAN_NB):
            for m, sl in enumerate(sls):
                y_ref[bi, :, sl] = ys[(d * RW_SCAN_NB + bi) * len(sls) + m].astype(BF16)


def rwkv_scan(r, v, kk, lw, kd, bb):
    b, t, w = r.shape
    assert b % RW_SCAN_NB == 0 and t == T_ALL and w == RW_WIDTH
    specs, specs_d = [], []
    for d in range(2):
        cidx = _chunk_index(d == 1)
        specs += [pl.BlockSpec((RW_SCAN_NB, CH, w), lambda i, j, cidx=cidx: (i, cidx(j), 0))] * 3
        specs_d += [pl.BlockSpec((1, RW_SCAN_NB, CH, w), lambda i, j, cidx=cidx, d=d: (d, i, cidx(j), 0))] * 3
    out = jax.ShapeDtypeStruct((b, t, w), BF16)
    return pl.pallas_call(
        _rwkv_scan_kernel,
        grid=(b // RW_SCAN_NB, NC),
        in_specs=specs + specs_d,
        out_specs=[specs[0], specs[3]],
        out_shape=[out, out],
        scratch_shapes=[pltpu.VMEM((RW_SCAN_NB * RW_HEADS, RW_HEAD_DIM, 2 * RW_HEAD_DIM), F32)],
        compiler_params=_cparams("parallel", "arbitrary"),
        name="rwkv_scan",
    )(r, v, kk, r, v, kk, lw, kd, bb, lw, kd, bb)


def _hy_filter_kernel(z_ref, t_ref, w1_ref, b1_ref, w2_ref, b2_ref, w3_ref, b3_ref, w4_ref, fr_ref, dl_ref, o_ref):
    fr = fr_ref[...]
    h = jnp.sin(fr * (_mm(z_ref[...], w1_ref[...], NN, 2, 2) + b1_ref[...]))
    h = jnp.sin(fr * (_mm(h, w2_ref[...], NN, 2, 2) + b2_ref[...]))
    h = jnp.sin(fr * (_mm(h, w3_ref[...], NN, 2, 2) + b3_ref[...]))
    h = _mm(h, w4_ref[...], NN, 2, 2)
    h = h * jnp.exp(-t_ref[...] * dl_ref[...])
    hw = HY_WIDTH
    rows = lax.broadcasted_iota(jnp.int32, (h.shape[0], 1), 0)
    hf = h[:, :hw]
    hb = jnp.where(rows == 0, 0.0, h[:, hw:])
    norm = jnp.sum(jnp.abs(hf), axis=0, keepdims=True) + jnp.sum(jnp.abs(hb), axis=0, keepdims=True)
    inv_norm = 1.0 / norm
    o_ref[:, :hw] = (hf + hb) * inv_norm
    o_ref[:, hw:] = (hf - hb) * inv_norm


def hyena_filter_halves(length, w1, b1, w2, b2, w3, b3, w4, freq):
    t = jnp.linspace(0.0, 1.0, length, dtype=F32)[:, None]
    wv = 2 * math.pi * jnp.arange(length, dtype=F32)[:, None] / length
    f = jnp.linspace(1e-4, HY_BANDS - 1, HY_BANDS, dtype=F32)[None, :]
    z = jnp.concatenate([t, jnp.cos(f * wv), -jnp.sin(f * wv)], axis=-1)
    z = jnp.pad(z, ((0, 0), (0, HY_ORDER - HY_EMB)))
    w1p = jnp.pad(w1, ((0, HY_ORDER - HY_EMB), (0, 0)))
    deltas = jnp.abs(jnp.linspace(math.log(HY_TARGET) / HY_SLOW_DECAY, math.log(HY_TARGET) / HY_FAST_DECAY,
                                  HY_WIDTH, dtype=F32))
    dl = jnp.tile(deltas, 2)[None, :]
    row = lambda a: a.reshape(1, -1)
    return pl.pallas_call(
        _hy_filter_kernel,
        out_shape=jax.ShapeDtypeStruct((length, 2 * HY_WIDTH), F32),
        compiler_params=pltpu.CompilerParams(vmem_limit_bytes=VMEM_LIMIT),
        name="hyena_filter",
    )(z, t, w1p, row(b1), w2, row(b2), w3, row(b3), w4, row(freq), dl)


def _freq_block(length):
    return min(FB_MAX, length)


def dft_matrices(length):
    n = 2 * length
    FB = _freq_block(length)
    nfb = length // FB
    t = jnp.arange(length, dtype=jnp.int32)
    ang = lambda f: ((f[:, None] * t[None, :]) % n).astype(F32) * (2 * math.pi / n)
    a_hi = ang(jnp.arange(nfb, dtype=jnp.int32) * FB)
    a_lo = ang(jnp.arange(FB, dtype=jnp.int32))
    nyq = jnp.where(t % 2 == 0, 1.0, -1.0).astype(F32)

    def build(ch, sh, cl, sl, first, nyq_b, axis):
        cosm = ch * cl - sh * sl
        msin = -(sh * cl + ch * sl)
        msin = jnp.where(first, nyq_b, msin)
        return jnp.concatenate([cosm, msin], axis=axis).astype(BF16)

    bi = lax.broadcasted_iota
    first = jnp.logical_and(bi(jnp.int32, (nfb, FB, 1), 0) == 0, bi(jnp.int32, (nfb, FB, 1), 1) == 0)
    wf = build(jnp.cos(a_hi)[:, None, :], jnp.sin(a_hi)[:, None, :], jnp.cos(a_lo)[None], jnp.sin(a_lo)[None],
               first, nyq[None, None, :], 1).reshape(n, length)
    first_t = jnp.logical_and(bi(jnp.int32, (1, nfb, FB), 1) == 0, bi(jnp.int32, (1, nfb, FB), 2) == 0)
    wft = build(jnp.cos(a_hi).T[:, :, None], jnp.sin(a_hi).T[:, :, None], jnp.cos(a_lo).T[:, None, :],
                jnp.sin(a_lo).T[:, None, :], first_t, nyq[:, None, None], 2).reshape(length, n)
    return wf, wft


def _spectrum_kernel(w_ref, h_ref, o_ref):
    o_ref[...] = _dot(w_ref[...], h_ref[...].astype(BF16))


def filter_spectrum(wf, hpm):
    n, length = wf.shape
    cols = hpm.shape[1]
    FB = _freq_block(length)
    return pl.pallas_call(
        _spectrum_kernel,
        grid=(n // (2 * FB),),
        in_specs=[pl.BlockSpec((2 * FB, length), lambda i: (i, 0)),
                  pl.BlockSpec((length, cols), lambda i: (0, 0))],
        out_specs=pl.BlockSpec((2 * FB, cols), lambda i: (i, 0)),
        out_shape=jax.ShapeDtypeStruct((n, cols), F32),
        compiler_params=_cparams("parallel"),
        name="filter_spectrum",
    )(wf, hpm)


def _hy_conv_kernel(s_ref, wf_ref, wft_ref, h_ref, o_ref, *, n_fft):
    fb = pl.program_id(1)
    FB = wf_ref.shape[0] // 2

    @pl.when(fb == 0)
    def _():
        o_ref[...] = jnp.zeros_like(o_ref)

    hw = HY_WIDTH
    hraw = h_ref[...]
    hr, hi, hn = hraw[:FB, :hw], hraw[FB:, hw:], hraw[FB:, :hw]
    fidx = lax.broadcasted_iota(jnp.int32, (FB, 1), 0) + fb * FB
    m = (fidx != 0).astype(F32)
    cf = jnp.where(fidx != 0, 2.0 / n_fft, 1.0 / n_fft).astype(F32)
    hr_c = hr * cf
    him = hi * m * cf
    hc = (hr * m + hn * (1.0 - m)) * cf
    for bi in range(s_ref.shape[0]):
        x = _dot(wf_ref[...], s_ref[bi])
        xr, xi = x[:FB], x[FB:]
        y = jnp.concatenate([xr * hr_c - xi * him, xr * him + xi * hc], axis=0).astype(BF16)
        o_ref[bi] += _dot(wft_ref[...], y)


def hyena_long_conv(s, wf, wft, hspec, length, block):
    b = s.shape[0]
    hw = HY_WIDTH
    FB = _freq_block(length)
    assert b % CONV_NB == 0 and length % FB == 0
    return pl.pallas_call(
        functools.partial(_hy_conv_kernel, n_fft=2 * length),
        grid=(b // CONV_NB, length // FB),
        in_specs=[pl.BlockSpec((CONV_NB, length, hw), lambda i, j: (i, block, 0)),
                  pl.BlockSpec((2 * FB, length), lambda i, j: (j, 0)),
                  pl.BlockSpec((length, 2 * FB), lambda i, j: (0, j)),
                  pl.BlockSpec((2 * FB, 2 * hw), lambda i, j: (j, 0))],
        out_specs=pl.BlockSpec((CONV_NB, length, hw), lambda i, j: (i, 0, 0)),
        out_shape=jax.ShapeDtypeStruct((b, length, hw), F32),
        compiler_params=_cparams("parallel", "arbitrary"),
        name="hyena_long_conv",
    )(s, wf, wft, hspec)


MLP_HC = 1024


def _mlp_body(x, m, nw_ref, w1_ref, w2_ref, fw_ref, final):
    h = _rms_mod(x, nw_ref[...], m[3:4], m[4:5]).astype(BF16)
    acc = jnp.zeros_like(x)
    for c in range(w1_ref.shape[1] // MLP_HC):
        hid = _dot(h, w1_ref[:, c * MLP_HC:(c + 1) * MLP_HC])
        hid = jnp.square(jnp.maximum(hid, 0.0))
        acc = acc + _dot(hid.astype(BF16), w2_ref[c * MLP_HC:(c + 1) * MLP_HC, :])
    o = x + m[5:6] * acc
    if final:
        o = o * lax.rsqrt(jnp.mean(o * o, axis=-1, keepdims=True) + EPS) * fw_ref[...]
    return o


def _ab_tail_kernel(y0_ref, y1_ref, bv_ref, g_ref, x0_ref, cvl_ref, cvc_ref, s_ref, x_ref, ctx_ref, mod_ref,
                    lnw_ref, lnb_ref, skip_ref, bd_ref, wo_ref, nw_ref, w1_ref, w2_ref, fw_ref, o_ref):
    bd = bd_ref[...]
    inv = 1.0 / RW_HEAD_DIM
    y = y0_ref[0].astype(F32) + y1_ref[0].astype(F32)
    mean = _segsum(y, bd) * inv
    yc = y - mean
    var = _segsum(yc * yc, bd) * inv
    yn = yc * lax.rsqrt(var + RW_GN_EPS)
    a = (yn * lnw_ref[...] + lnb_ref[...] + bv_ref[0].astype(F32)) * g_ref[0].astype(F32)
    s = s_ref[0].astype(F32)
    bh = x0_ref[0].astype(F32) * (_token_block(cvl_ref, cvc_ref) + s * skip_ref[...])
    w = RW_WIDTH
    out = _dot(a.astype(BF16), wo_ref[:w, :]) + _dot(bh.astype(BF16), wo_ref[w:, :])
    m = mod_ref[0, 0]
    xo = _token_block(x_ref, ctx_ref) + m[2:3] * out
    o_ref[0] = _mlp_body(xo, m, nw_ref, w1_ref, w2_ref, fw_ref, False)


def ab_tail(y0, y1, bv, g, x0, cv_lat, cv_ctx, s, x, ctx, modv, ln_w, ln_b, skip, bd64, w_out, nw, w1, w2, fw,
            layer):
    b, _, d = x.shape
    w = RW_WIDTH
    tok = pl.BlockSpec((1, TB, w), lambda i, j: (i, j, 0))
    full = lambda a: pl.BlockSpec(a.shape, lambda i, j: (0,) * a.ndim)
    params = (ln_w, ln_b, skip, bd64, w_out, nw.reshape(1, d), w1, w2, fw.reshape(1, d))
    param_specs = [full(a) for a in params]
    param_specs[6] = pl.BlockSpec((None,) + w1.shape[1:], lambda i, j: (layer, 0, 0))
    param_specs[7] = pl.BlockSpec((None,) + w2.shape[1:], lambda i, j: (layer, 0, 0))
    return pl.pallas_call(
        _ab_tail_kernel,
        grid=(b, NB),
        in_specs=[tok] * 5 + _split_specs(w) + [tok] + _split_specs(d)
        + [pl.BlockSpec((1, 1, ADA_CHUNKS, d), lambda i, j: (i, j // NB_LAT, 0, 0))] + param_specs,
        out_specs=pl.BlockSpec((1, TB, d), lambda i, j: (i, j, 0)),
        out_shape=jax.ShapeDtypeStruct((b, T_ALL, d), F32),
        compiler_params=_cparams("parallel", "parallel"),
        name="ab_tail",
    )(y0, y1, bv, g, x0, cv_lat, cv_ctx, s, x, ctx, modv, *params)


def _dn_front_kernel(x_ref, xp_ref, xn_ref, nw_ref, mod_ref, w_ref, cw_ref, alog_ref, dtb_ref, bd_ref,
                     q_ref, k_ref, v_ref, z_ref, gb_ref):
    p, pprev, pnext = _project_with_neighbours(x_ref[0], xp_ref, xn_ref, nw_ref, mod_ref, w_ref)
    wq = 3 * DN_DIM
    c = cw_ref[0:1, :] * pprev[:, :wq] + cw_ref[1:2, :] * p[:, :wq] + cw_ref[2:3, :] * pnext[:, :wq]
    c = c * _sigmoid(c)
    dd = DN_DIM
    bd = bd_ref[...]
    q, k = c[:, :dd], c[:, dd:2 * dd]
    q_ref[0] = (q * lax.rsqrt(_segsum(q * q, bd) + EPS) * (DN_HEAD_DIM ** -0.5)).astype(BF16)
    k_ref[0] = (k * lax.rsqrt(_segsum(k * k, bd) + EPS)).astype(BF16)
    v_ref[0] = c[:, 2 * dd:].astype(BF16)
    z_ref[0] = p[:, wq:wq + dd].astype(BF16)
    slab = p[:, wq + dd:]
    lane = lax.broadcasted_iota(jnp.int32, (1, 128), 1)
    gdec = -jnp.exp(alog_ref[...]) * _softplus(slab + dtb_ref[...])
    gb_ref[0] = jnp.where(lane < 2 * DN_HEADS, gdec, _sigmoid(slab))


def dn_front(xs, nw, modv, w_in, conv_w, alog_row, dtb_row, bd128):
    b, t, d = xs.shape
    full = lambda a: pl.BlockSpec(a.shape, lambda i, j: (0,) * a.ndim)
    params = (nw.reshape(1, d), modv, w_in, conv_w, alog_row, dtb_row, bd128)
    param_specs = [full(a) for a in params]
    param_specs[1] = pl.BlockSpec((1, 1, ADA_CHUNKS, d), lambda i, j: (i, j // NB_LAT, 0, 0))
    tok = pl.BlockSpec((1, TB, DN_DIM), lambda i, j: (i, j, 0))
    s1 = jax.ShapeDtypeStruct((b, t, DN_DIM), BF16)
    return pl.pallas_call(
        _dn_front_kernel,
        grid=(b, t // TB),
        in_specs=[pl.BlockSpec((1, TB, d), lambda i, j: (i, j, 0))] + _halo_specs(d, t // 8) + param_specs,
        out_specs=[tok, tok, tok, tok, pl.BlockSpec((1, TB, 128), lambda i, j: (i, j, 0))],
        out_shape=[s1, s1, s1, s1, jax.ShapeDtypeStruct((b, t, 128), F32)],
        compiler_params=_cparams("parallel", "parallel"),
        name="dn_front",
    )(xs, xs, xs, *params)


def _dn_scan_kernel(qf_ref, kf_ref, vf_ref, gbf_ref, qr_ref, kr_ref, vr_ref, gbr_ref, yf_ref, yr_ref, s_ref):
    c, kdim = CH, DN_HEAD_DIM

    @pl.when(pl.program_id(1) == 0)
    def _():
        s_ref[...] = jnp.zeros_like(s_ref)

    sls = [slice(h * kdim, (h + 1) * kdim) for h in range(DN_HEADS)]
    dirs = ((qf_ref, kf_ref, vf_ref, gbf_ref), (qr_ref, kr_ref, vr_ref, gbr_ref))
    lhs, rhs, ks, qs, kps, bbs, vhs = ([] for _ in range(7))
    for d, (q_ref, k_ref, v_ref, gb_ref) in enumerate(dirs):
        for bi in range(SCAN_NB):
            gb = gb_ref[bi]
            q, k, v = q_ref[bi].astype(F32), k_ref[bi].astype(F32), v_ref[bi]
            for h, sl in enumerate(sls):
                col = d * DN_HEADS + h
                kp = k[:, sl] * gb[:, 2 * DN_HEADS + col:2 * DN_HEADS + col + 1]
                bb = kp * jnp.exp(gb[:, col:col + 1])
                lhs.append(jnp.concatenate([k[:, sl], q[:, sl]], axis=0))
                rhs.append(jnp.concatenate([bb, kp] if h % 2 == 0 else [kp, bb], axis=0))
                ks.append(k[:, sl])
                qs.append(q[:, sl])
                kps.append(kp)
                bbs.append(bb)
                vhs.append(v[:, sl])
    a_raw = _mm_many(lhs, rhs, NT)
    dfull, kkt, rt, khs, bhs, decs = ([] for _ in range(6))
    ti2 = lax.broadcasted_iota(jnp.int32, (c, 2 * c), 0)
    ji2 = lax.broadcasted_iota(jnp.int32, (c, 2 * c), 1) & (c - 1)
    for d, (_, _, _, gb_ref) in enumerate(dirs):
        reverse = d == 1
        _, _, _, incl = _chunk_masks(reverse)
        strict2 = (ji2 > ti2) if reverse else (ji2 < ti2)
        incl2 = jnp.logical_or(strict2, ti2 == ji2)
        inclt2 = (jnp.logical_or(ji2 < ti2, ti2 == ji2) if reverse else jnp.logical_or(ji2 > ti2, ti2 == ji2))
        last = 0 if reverse else c - 1
        gbs = [gb_ref[bi] for bi in range(SCAN_NB)]
        gcols = [_mm(incl.astype(BF16), gb, NN, 1, 2) for gb in gbs]
        grows = [_mm(gb, inclt2.astype(BF16), TN, 2, 1) for gb in gbs]
        for bi in range(SCAN_NB):
            gb, gcol, grow = gbs[bi], gcols[bi], grows[bi]
            for h in range(DN_HEADS):
                i = (d * SCAN_NB + bi) * DN_HEADS + h
                col = d * DN_HEADS + h
                gt = gcol[:, col:col + 1]
                gxt = gt - gb[:, col:col + 1]
                gj = grow[col:col + 1, :]
                gc = gcol[last:last + 1, col:col + 1]
                d3 = jnp.where(incl2, jnp.exp(jnp.where(incl2, gt - gj, 0.0)), 0.0)
                d1 = jnp.where(strict2, jnp.exp(jnp.where(strict2, gxt - gj, 0.0)), 0.0)
                dfull.append(jnp.concatenate([d1, d3], axis=0))
                egc = jnp.exp(gc - gt)
                kkt.append(ks[i] * jnp.exp(gxt))
                rt.append(qs[i] * jnp.exp(gt))
                khs.append(kps[i] * egc)
                bhs.append(bbs[i] * egc)
                decs.append(jnp.exp(gc))
    a = [x * dm for x, dm in zip(a_raw, dfull)]
    ys = _chunk_update_many(s_ref, a, kkt, rt, khs, bhs, vhs, decs, kdim)
    for d, y_ref in enumerate((yf_ref, yr_ref)):
        for bi in range(SCAN_NB):
            for h, sl in enumerate(sls):
                y_ref[bi, :, sl] = ys[(d * SCAN_NB + bi) * DN_HEADS + h].astype(BF16)


def dn_scan(q, k, v, gb):
    b, t, w = q.shape
    assert b % SCAN_NB == 0 and t == T_ALL and w == DN_DIM
    specs = []
    for d in range(2):
        cidx = _chunk_index(d == 1)
        tok = pl.BlockSpec((SCAN_NB, CH, w), lambda i, j, cidx=cidx: (i, cidx(j), 0))
        specs += [tok, tok, tok, pl.BlockSpec((SCAN_NB, CH, 128), lambda i, j, cidx=cidx: (i, cidx(j), 0))]
    out = jax.ShapeDtypeStruct((b, t, w), BF16)
    return pl.pallas_call(
        _dn_scan_kernel,
        grid=(b // SCAN_NB, NC),
        in_specs=specs,
        out_specs=[specs[0], specs[4]],
        out_shape=[out, out],
        scratch_shapes=[pltpu.VMEM((2 * SCAN_NB * DN_HEADS, DN_HEAD_DIM, DN_HEAD_DIM), F32)],
        compiler_params=_cparams("parallel", "arbitrary"),
        name="dn_scan",
    )(q, k, v, gb, q, k, v, gb)


def _dn_tail_kernel(o0_ref, o1_ref, z_ref, x_ref, mod_ref, gnw_ref, bd_ref, wo_ref, nw_ref, w1_ref, w2_ref, fw_ref,
                    o_ref):
    o = o0_ref[0].astype(F32) + o1_ref[0].astype(F32)
    ms = _segsum(o * o, bd_ref[...]) * (1.0 / DN_HEAD_DIM)
    on = o * lax.rsqrt(ms + EPS) * gnw_ref[...]
    z = z_ref[0].astype(F32)
    gated = on * (z * _sigmoid(z))
    m = mod_ref[0, 0]
    xo = x_ref[0] + m[2:3] * _dot(gated.astype(BF16), wo_ref[...])
    o_ref[0] = _mlp_body(xo, m, nw_ref, w1_ref, w2_ref, fw_ref, True)


def dn_tail(o0, o1, z, xs, modv, gnw_tiled, bd128, w_out, nw, w1, w2, fw, layer):
    b, _, d = xs.shape
    tok = pl.BlockSpec((1, TB, DN_DIM), lambda i, j: (i, j, 0))
    full = lambda a: pl.BlockSpec(a.shape, lambda i, j: (0,) * a.ndim)
    params = (gnw_tiled, bd128, w_out, nw.reshape(1, d), w1, w2, fw.reshape(1, d))
    param_specs = [full(a) for a in params]
    param_specs[4] = pl.BlockSpec((None,) + w1.shape[1:], lambda i, j: (layer, 0, 0))
    param_specs[5] = pl.BlockSpec((None,) + w2.shape[1:], lambda i, j: (layer, 0, 0))
    return pl.pallas_call(
        _dn_tail_kernel,
        grid=(b, NB_LAT),
        in_specs=[tok, tok, tok, pl.BlockSpec((1, TB, d), lambda i, j: (i, j, 0)),
                  pl.BlockSpec((1, 1, ADA_CHUNKS, d), lambda i, j: (i, 0, 0, 0))] + param_specs,
        out_specs=pl.BlockSpec((1, TB, d), lambda i, j: (i, j, 0)),
        out_shape=jax.ShapeDtypeStruct((b, SEQ, d), F32),
        compiler_params=_cparams("parallel", "parallel"),
        name="dn_tail",
    )(o0, o1, z, xs, modv, *params)


def _block_diag_ones(head):
    i = jnp.arange(128) // head
    return (i[:, None] == i[None, :]).astype(BF16)


def kernel(x, c, ctx, c_ctx, ada_w, ada_b, norm_mix, norm_mlp, mlp_w1, mlp_w2, final_norm, ab_w_in, ab_w_out, rw_mu, rw_w0, rw_w_up, rw_a0, rw_a_up, rw_g_up, rw_k_k, rw_k_a, rw_r_k, rw_ln_w, rw_ln_b, hy_conv_w, hy_conv_b, hy_f_w1, hy_f_b1, hy_f_w2, hy_f_b2, hy_f_w3, hy_f_b3, hy_f_w4, hy_freq, hy_skip, dn_w_in, dn_conv_w, dn_A_log, dn_dt_bias, dn_norm, dn_w_out):
    bsz = x.shape[0]
    d = D_MODEL
    w = RW_WIDTH
    row = lambda a: a.reshape(1, -1)

    cs = jnp.concatenate([c, c_ctx[None, :], jnp.zeros((16 - bsz - 1, d), F32)], axis=0)
    mod = ada_modulation(cs, ada_w, ada_b)

    def mod_vectors(layer):
        lat = mod[layer, :bsz].reshape(bsz, 1, ADA_CHUNKS, d)
        cx = jnp.broadcast_to(mod[layer, bsz].reshape(1, 1, ADA_CHUNKS, d), (bsz, 1, ADA_CHUNKS, d))
        return jnp.concatenate([lat, cx], axis=1)

    w1_all, w2_all = mlp_w1.astype(BF16), mlp_w2.astype(BF16)
    bd64 = _block_diag_ones(RW_HEAD_DIM)
    bd128 = _block_diag_ones(DN_HEAD_DIM)

    modv = mod_vectors(0)
    zeros = jnp.zeros((64, 2 * w), F32)
    lora = jnp.concatenate([jnp.concatenate([rw_w_up[0, 0], rw_w_up[0, 1], zeros], axis=1),
                            jnp.concatenate([zeros, rw_a_up[0, 0], rw_a_up[0, 1]], axis=1)], axis=0)
    r, v, kk, g, bv, lw, kd, bb, s, x0 = ab_front(
        x, ctx, norm_mix[0], modv, ab_w_in[0].astype(BF16), row(rw_mu[0]), lora.astype(BF16), row(rw_w0[0]),
        row(rw_a0[0]), rw_g_up[0].astype(BF16), row(rw_k_k[0]), row(rw_k_a[0]), row(rw_r_k[0]), bd64,
        hy_conv_w[0], row(hy_conv_b[0]))
    y0, y1 = rwkv_scan(r, v, kk, lw, kd, bb)
    filt = (hy_f_w1[0], hy_f_b1[0], hy_f_w2[0], hy_f_b2[0], hy_f_w3[0], hy_f_b3[0], hy_f_w4[0], hy_freq[0])
    convs = []
    for length, block in ((SEQ, 0), (CTX_LEN, SEQ // CTX_LEN)):
        wf, wft = dft_matrices(length)
        hspec = filter_spectrum(wf, hyena_filter_halves(length, *filt))
        convs.append(hyena_long_conv(s, wf, wft, hspec, length, block))
    xs = ab_tail(y0, y1, bv, g, x0, convs[0], convs[1], s, x, ctx, modv, row(rw_ln_w[0]), row(rw_ln_b[0]),
                 row(hy_skip[0]), bd64, ab_w_out[0].astype(BF16), norm_mlp[0], w1_all, w2_all, final_norm, 0)

    modv = mod_vectors(1)
    w_in = jnp.pad(dn_w_in[0], ((0, 0), (0, DN_COLS_PAD - DN_COLS))).astype(BF16)
    pad_row = lambda a: jnp.pad(a.reshape(1, -1), ((0, 0), (0, 128 - 2 * DN_HEADS)))
    q, k, vv, z, gb = dn_front(xs, norm_mix[1], modv, w_in, dn_conv_w[0], pad_row(dn_A_log[0]),
                               pad_row(dn_dt_bias[0]), bd128)
    o0, o1 = dn_scan(q, k, vv, gb)
    return dn_tail(o0, o1, z, xs, modv, row(jnp.tile(dn_norm[0], DN_HEADS)), bd128, dn_w_out[0].astype(BF16),
                   norm_mlp[1], w1_all, w2_all, final_norm, 1)
```

```python
import functools
import math

import jax
import jax.numpy as jnp
from jax import lax
from jax.experimental import pallas as pl
from jax.experimental.pallas import tpu as pltpu

F32, BF16 = jnp.float32, jnp.bfloat16

D_MODEL = 1024
SEQ = 2048
CTX_LEN = 256
ADA_CHUNKS = 6
EPS = 1e-6
RW_WIDTH = 512
RW_HEAD_DIM = 64
RW_HEADS = 8
RW_COLS = 1792
RW_GN_EPS = 64e-5
HY_WIDTH = 512
HY_COLS = 1536
HY_BANDS = 16
HY_EMB = 33
HY_ORDER = 64
HY_FAST_DECAY = 0.3
HY_SLOW_DECAY = 1.5
HY_TARGET = 1e-2
AB_COLS = RW_COLS + HY_COLS
DN_HEADS = 8
DN_HEAD_DIM = 128
DN_DIM = 1024
DN_COLS = 4 * DN_DIM + 4 * DN_HEADS
DN_COLS_PAD = 4 * DN_DIM + 128

TB = 256
CH = 64
T_ALL = SEQ + CTX_LEN
NB = T_ALL // TB
NB_LAT = SEQ // TB
NC = T_ALL // CH
NC_LAT = SEQ // CH
FB_MAX = 256
CONV_NB = 2
SCAN_NB = 4
RW_SCAN_NB = 8
VMEM_LIMIT = 56 * 1024 * 1024

NN = ((1,), (0,))
NT = ((1,), (1,))
TN = ((0,), (0,))


def _dot(a, b, dims=NN):
    return lax.dot_general(a, b, (dims, ((), ())), preferred_element_type=F32)


def _split(x):
    hi = x.astype(BF16)
    lo = (x - hi.astype(F32)).astype(BF16)
    return hi, lo


def _mm(a, b, dims=NN, pa=1, pb=1):
    if a.dtype == BF16:
        a_hi, a_lo, pa = a, None, 1
    elif pa == 2:
        a_hi, a_lo = _split(a)
    else:
        a_hi, a_lo = a.astype(BF16), None
    if b.dtype == BF16:
        b_hi, b_lo, pb = b, None, 1
    elif pb == 2:
        b_hi, b_lo = _split(b)
    else:
        b_hi, b_lo = b.astype(BF16), None
    out = _dot(a_hi, b_hi, dims)
    if pa == 2:
        out = out + _dot(a_lo, b_hi, dims)
    if pb == 2:
        out = out + _dot(a_hi, b_lo, dims)
    return out


def _segsum(x, bd):
    parts = [_dot(x[:, g * 128:(g + 1) * 128].astype(BF16), bd) for g in range(x.shape[1] // 128)]
    return jnp.concatenate(parts, axis=1)


def _sigmoid(x):
    return 1.0 / (1.0 + jnp.exp(-x))


def _softplus(x):
    return jnp.maximum(x, 0.0) + jnp.log(1.0 + jnp.exp(-jnp.abs(x)))


def _rms_mod(x, nw, shift, scale):
    y = x * lax.rsqrt(jnp.mean(x * x, axis=-1, keepdims=True) + EPS)
    return (y * nw) * (1.0 + scale) + shift


def _cparams(*sem):
    return pltpu.CompilerParams(dimension_semantics=sem, vmem_limit_bytes=VMEM_LIMIT)


ADA_TN = 1536


def _ada_kernel(c_ref, w_ref, b_ref, o_ref):
    c = c_ref[...]
    a = c * _sigmoid(c)
    a_hi, a_lo = _split(a)
    w_hi, w_lo = _split(w_ref[0])
    rows = a.shape[0]
    both = _dot(jnp.concatenate([a_hi, a_lo], axis=0), w_hi)
    o_ref[0] = both[:rows] + both[rows:] + _dot(a_hi, w_lo) + b_ref[0]


def ada_modulation(cs, ada_w, ada_b):
    depth, d, n = ada_w.shape
    rows = cs.shape[0]
    return pl.pallas_call(
        _ada_kernel,
        grid=(depth, n // ADA_TN),
        in_specs=[pl.BlockSpec((rows, d), lambda l, j: (0, 0)),
                  pl.BlockSpec((1, d, ADA_TN), lambda l, j: (l, 0, j)),
                  pl.BlockSpec((1, 1, ADA_TN), lambda l, j: (l, 0, j))],
        out_specs=pl.BlockSpec((1, rows, ADA_TN), lambda l, j: (l, 0, j)),
        out_shape=jax.ShapeDtypeStruct((depth, rows, n), F32),
        compiler_params=_cparams("parallel", "parallel"),
        name="ada_modulation",
    )(cs, ada_w, ada_b.reshape(depth, 1, n))


def _token_block(x_ref, ctx_ref):
    return jnp.where(pl.program_id(1) == NB - 1, ctx_ref[0], x_ref[0])


def _split_specs(d):
    return [pl.BlockSpec((1, TB, d), lambda i, j: (i, jnp.minimum(j, NB_LAT - 1), 0)),
            pl.BlockSpec((1, TB, d), lambda i, j: (i, 0, 0))]


def _halo_specs(width, n_row_blocks):
    prev = pl.BlockSpec((1, 8, width), lambda i, j: (i, jnp.maximum(j * (TB // 8) - 1, 0), 0))
    nxt = pl.BlockSpec((1, 8, width), lambda i, j: (i, jnp.minimum((j + 1) * (TB // 8), n_row_blocks - 1), 0))
    return [prev, nxt]


def _project_with_neighbours(x, prev_ref, next_ref, nw_ref, mod_ref, w_ref):
    j = pl.program_id(1)
    m = mod_ref[0, 0]
    x_ext = jnp.concatenate([x, prev_ref[0], next_ref[0]], axis=0)
    h = _rms_mod(x_ext, nw_ref[...], m[0:1], m[1:2])
    p_ext = _dot(h.astype(BF16), w_ref[...])
    p = p_ext[:TB]
    pv = jnp.where(jnp.logical_and(j != 0, j != NB - 1), 1.0, 0.0).astype(F32)
    nv = jnp.where(j < NB - 2, 1.0, 0.0).astype(F32)
    prow = p_ext[TB + 7:TB + 8] * pv
    nrow = p_ext[TB + 8:TB + 9] * nv
    rows = lax.broadcasted_iota(jnp.int32, (8, 1), 0)
    pprev = pltpu.roll(p, 1, 0)
    pprev = jnp.concatenate([jnp.where(rows == 0, prow, pprev[:8]), pprev[8:]], axis=0)
    pnext = pltpu.roll(p, TB - 1, 0)
    pnext = jnp.concatenate([pnext[:TB - 8], jnp.where(rows == 7, nrow, pnext[TB - 8:])], axis=0)
    return p, pprev, pnext


def _ab_front_kernel(x_ref, ctx_ref, xp_ref, xn_ref, nw_ref, mod_ref, w_ref,
                     mu_ref, lora_ref, w0_ref, a0_ref, gup_ref, kk_ref, ka_ref, rk_ref, bd_ref, cw_ref, cb_ref,
                     r_ref, v_ref, kkn_ref, g_ref, bv_ref, lw_ref, kd_ref, b_ref, s_ref, x0_ref):
    x, xprev, xnext = _project_with_neighbours(_token_block(x_ref, ctx_ref), xp_ref, xn_ref, nw_ref, mod_ref, w_ref)
    w = RW_WIDTH
    xr = x[:, :RW_COLS]
    ps = xr + mu_ref[...] * (0.5 * (xprev[:, :RW_COLS] + xnext[:, :RW_COLS]) - xr)
    r, k, v = ps[:, :w], ps[:, w:2 * w], ps[:, 2 * w:3 * w]
    slab = ps[:, 3 * w:3 * w + 128]
    gl = ps[:, 3 * w + 128:3 * w + 256]
    lane = lax.broadcasted_iota(jnp.int32, (1, 128), 1)
    z = jnp.where(lane < 64, jnp.tanh(slab), slab)
    lora = _dot(z.astype(BF16), lora_ref[...])
    bd = bd_ref[...]
    kkraw = k * kk_ref[...]
    kk = kkraw * lax.rsqrt(_segsum(kkraw * kkraw, bd) + EPS)
    ksum = jnp.zeros_like(k)
    for d in range(2):
        lw_ref[d, 0] = (-math.exp(-0.5)) * _sigmoid(w0_ref[:, d * w:(d + 1) * w] + lora[:, d * w:(d + 1) * w])
        a = _sigmoid(a0_ref[:, d * w:(d + 1) * w] + lora[:, (2 + d) * w:(3 + d) * w])
        kd = k * (1.0 + (a - 1.0) * ka_ref[...])
        kd_ref[d, 0] = kd.astype(BF16)
        b_ref[d, 0] = (a * kk).astype(BF16)
        ksum = ksum + kd
    r_ref[0] = r.astype(BF16)
    v_ref[0] = v.astype(BF16)
    kkn_ref[0] = kk.astype(BF16)
    g_ref[0] = _dot(_sigmoid(gl).astype(BF16), gup_ref[...]).astype(BF16)
    bv_ref[0] = (_segsum(r * ksum * rk_ref[...], bd) * v).astype(BF16)
    u = x[:, RW_COLS:]
    c = (cw_ref[0:1, :] * xprev[:, RW_COLS:] + cw_ref[1:2, :] * u + cw_ref[2:3, :] * xnext[:, RW_COLS:]
         + cb_ref[...])
    hw = HY_WIDTH
    x0_ref[0] = c[:, :hw].astype(BF16)
    s_ref[0] = (c[:, hw:2 * hw] * c[:, 2 * hw:]).astype(BF16)


def ab_front(x, ctx, nw, modv, w_in, mu, lora, w0, a0, gup, k_k, k_a, r_k, bd64, conv_w, conv_b):
    b, _, d = x.shape
    w = RW_WIDTH
    full = lambda a: pl.BlockSpec(a.shape, lambda i, j: (0,) * a.ndim)
    params = (nw.reshape(1, d), modv, w_in, mu, lora, w0, a0, gup, k_k, k_a, r_k, bd64, conv_w, conv_b)
    param_specs = [full(a) for a in params]
    param_specs[1] = pl.BlockSpec((1, 1, ADA_CHUNKS, d), lambda i, j: (i, j // NB_LAT, 0, 0))
    tok = pl.BlockSpec((1, TB, w), lambda i, j: (i, j, 0))
    tok2 = pl.BlockSpec((2, 1, TB, w), lambda i, j: (0, i, j, 0))
    s1 = jax.ShapeDtypeStruct((b, T_ALL, w), BF16)
    s2 = jax.ShapeDtypeStruct((2, b, T_ALL, w), BF16)
    s2f = jax.ShapeDtypeStruct((2, b, T_ALL, w), F32)
    return pl.pallas_call(
        _ab_front_kernel,
        grid=(b, NB),
        in_specs=_split_specs(d) + _halo_specs(d, SEQ // 8) + param_specs,
        out_specs=[tok, tok, tok, tok, tok, tok2, tok2, tok2, tok, tok],
        out_shape=[s1, s1, s1, s1, s1, s2f, s2, s2, s1, s1],
        compiler_params=_cparams("parallel", "parallel"),
        name="ab_front",
    )(x, ctx, x, x, *params)


def _chunk_masks(reverse):
    c = CH
    ti = lax.broadcasted_iota(jnp.int32, (c, c), 0)
    ji = lax.broadcasted_iota(jnp.int32, (c, c), 1)
    strict = (ji > ti) if reverse else (ji < ti)
    incl = jnp.logical_or(strict, ti == ji)
    return ti, ji, strict, incl


def _mm_many(a_list, b_list, dims=NN):
    ops = [(a.astype(BF16), b.astype(BF16)) for a, b in zip(a_list, b_list)]
    return [_dot(a, b, dims) for a, b in ops]


def _tri_inverse_pairs(n_pairs):
    c = CH
    ti = lax.broadcasted_iota(jnp.int32, (c, 2 * c), 0)
    li = lax.broadcasted_iota(jnp.int32, (c, 2 * c), 1)
    ji = li & (c - 1)
    left = li < c

    def block_diag(x):
        xb = x.astype(BF16)
        zero = jnp.zeros_like(xb)
        return jnp.concatenate([jnp.where(left, xb, zero), jnp.where(left, zero, xb)], axis=0)

    eye = (ti == ji).astype(F32)
    pair = (ti >> 1) == (ji >> 1)
    dinvs = [eye - jnp.where(pair, n, 0.0) for n in n_pairs]
    s = 1
    while (2 << s) <= c:
        off = jnp.logical_and((ti >> (s + 1)) == (ji >> (s + 1)), (ti >> s) != (ji >> s))
        coffs = [block_diag(jnp.where(off, n, 0.0)) for n in n_pairs]
        xs = _mm_many(dinvs, coffs)
        corr = _mm_many(xs, [block_diag(d) for d in dinvs])
        dinvs = [d - t for d, t in zip(dinvs, corr)]
        s += 1
    return dinvs


def _chunk_update_many(s_ref, a, kkt, rt, kh, bh, vh, dec_c, kdim):
    n = len(a)
    c = CH
    left = lax.broadcasted_iota(jnp.int32, (c, 2 * c), 1) < c
    tpairs = _tri_inverse_pairs([jnp.where(left, a[i][:c], a[i + 1][:c]) for i in range(0, n, 2)])
    tms = [tpairs[i // 2][:, :c] if i % 2 == 0 else tpairs[i // 2][:, c:] for i in range(n)]
    a_v = [a[i][:, c:] if i % 2 == 0 else a[i][:, :c] for i in range(n)]
    a_rb = [a[i][c:, :c] if i % 2 == 0 else a[i][c:, c:] for i in range(n)]
    av = _mm_many(a_v, vh)
    tw = _mm_many(tms, [jnp.concatenate([kkt[i], av[i][:c]], axis=1) for i in range(n)])
    states = [s_ref[i] for i in range(n)]
    p = _mm_many([jnp.concatenate([tw[i][:, :kdim], rt[i]], axis=0) for i in range(n)], states, NT)
    u = [p[i][:c] + tw[i][:, kdim:] for i in range(n)]
    au = _mm_many(a_rb, u)
    ds = _mm_many([jnp.concatenate([vh[i], -u[i]], axis=0) for i in range(n)],
                  [jnp.concatenate([kh[i], bh[i]], axis=0) for i in range(n)], TN)
    for i in range(n):
        s_ref[i] = states[i] * dec_c[i] + ds[i]
    return [p[i][c:] + av[i][c:] - au[i] for i in range(n)]


def _pair_update_many(s_ref, a, kkt, rt, kh, bh, vh, dec_c):
    n = len(kkt)
    c = CH
    left = lax.broadcasted_iota(jnp.int32, (1, 2 * c), 1) < c

    def diag(x, anti=False):
        xb = x.astype(BF16)
        zero = jnp.zeros_like(xb)
        top, bottom = jnp.where(left, xb, zero), jnp.where(left, zero, xb)
        return jnp.concatenate([bottom, top] if anti else [top, bottom], axis=0)

    tps = _tri_inverse_pairs([jnp.where(left, a[2 * i + 1][:c], a[2 * i][:c]) for i in range(n)])
    av = _mm_many([jnp.where(left, a[2 * i], a[2 * i + 1]) for i in range(n)], [diag(x) for x in vh])
    tw = _mm_many(tps, [jnp.concatenate([diag(kkt[i], True), diag(av[i][:c], True)], axis=1) for i in range(n)])
    states = [s_ref[i] for i in range(n)]
    p = _mm_many([jnp.concatenate([tw[i][:, :2 * c], rt[i]], axis=0) for i in range(n)],
                 [diag(x) for x in states], NT)
    u = [p[i][:c] + tw[i][:, 2 * c:] for i in range(n)]
    au = _mm_many([jnp.where(left, a[2 * i + 1][c:], a[2 * i][c:]) for i in range(n)], [diag(x, True) for x in u])
    ds = _mm_many([jnp.concatenate([vh[i], -u[i]], axis=0) for i in range(n)],
                  [jnp.concatenate([kh[i], bh[i]], axis=0) for i in range(n)], TN)
    for i in range(n):
        s_ref[i] = states[i] * dec_c[i] + jnp.where(left, ds[i][:c], ds[i][c:])
    return [p[i][c:] + av[i][c:] - au[i] for i in range(n)]


def _chunk_index(reverse):
    if reverse:
        return lambda i: NC - 1 - i
    return lambda i: (i + NC_LAT) % NC


def _rwkv_scan_kernel(rf_ref, vf_ref, kkf_ref, rr_ref, vr_ref, kkr_ref, lwf_ref, kdf_ref, bf_ref,
                      lwr_ref, kdr_ref, br_ref, yf_ref, yr_ref, s_ref):
    c, kdim = CH, RW_HEAD_DIM

    @pl.when(pl.program_id(1) == 0)
    def _():
        s_ref[...] = jnp.zeros_like(s_ref)

    sls = [slice(m * 2 * kdim, (m + 1) * 2 * kdim) for m in range(RW_HEADS // 2)]
    slabs = lambda x: [x[:, sl] for sl in sls]
    left = lax.broadcasted_iota(jnp.int32, (1, 2 * kdim), 1) < kdim
    ri = lax.broadcasted_iota(jnp.int32, (2 * c, 2 * c), 0)
    ci = lax.broadcasted_iota(jnp.int32, (2 * c, 2 * c), 1)
    rt_, ct_ = ri & (c - 1), ci & (c - 1)
    lhs, rhs, amasks, kkt, rt, kh, bh, vh, dec = ([] for _ in range(9))
    dirs = ((rf_ref, vf_ref, kkf_ref, lwf_ref, kdf_ref, bf_ref), (rr_ref, vr_ref, kkr_ref, lwr_ref, kdr_ref, br_ref))
    for d, (r_ref, v_ref, kk_ref, lw_ref, kd_ref, b_ref) in enumerate(dirs):
        reverse = d == 1
        _, _, _, incl = _chunk_masks(reverse)
        before = (ct_ > rt_) if reverse else (ct_ < rt_)
        amask = jnp.logical_or(before, jnp.logical_and(rt_ == ct_, ri >= c))
        last = 0 if reverse else c - 1
        lws = [lw_ref[0, bi] for bi in range(RW_SCAN_NB)]
        gs = [_mm(incl.astype(BF16), lw, NN, 1, 2) for lw in lws]
        for bi in range(RW_SCAN_NB):
            lw, g = lws[bi], gs[bi]
            gc = g[last:last + 1, :]
            eng = jnp.exp(-g)
            egc = jnp.exp(gc - g)
            kd, bb = kd_ref[0, bi].astype(F32), b_ref[0, bi].astype(F32)
            kkt_all = kk_ref[bi].astype(F32) * jnp.exp(g - lw)
            rt_all = r_ref[bi].astype(F32) * jnp.exp(g)
            for x, y, kn, bn in zip(slabs(kkt_all), slabs(rt_all), slabs(kd * eng), slabs(bb * eng)):
                both = jnp.concatenate([x, y], axis=0)
                lhs += [jnp.where(left, both, 0.0), jnp.where(left, 0.0, both)]
                rhs += [jnp.concatenate([kn, bn], axis=0), jnp.concatenate([bn, kn], axis=0)]
            amasks += [amask] * RW_HEADS
            kkt += slabs(kkt_all)
            rt += slabs(rt_all)
            kh += slabs(kd * egc)
            bh += slabs(bb * egc)
            vh += slabs(v_ref[bi])
            dec += slabs(jnp.exp(gc))
    a = [jnp.where(m, x, 0.0) for m, x in zip(amasks, _mm_many(lhs, rhs, NT))]
    ys = _pair_update_many(s_ref, a, kkt, rt, kh, bh, vh, dec)
    for d, y_ref in enumerate((yf_ref, yr_ref)):
        for bi in range(RW_SCAN_NB):
            for m, sl in enumerate(sls):
                y_ref[bi, :, sl] = ys[(d * RW_SCAN_NB + bi) * len(sls) + m].astype(BF16)


def rwkv_scan(r, v, kk, lw, kd, bb):
    b, t, w = r.shape
    assert b % RW_SCAN_NB == 0 and t == T_ALL and w == RW_WIDTH
    specs, specs_d = [], []
    for d in range(2):
        cidx = _chunk_index(d == 1)
        specs += [pl.BlockSpec((RW_SCAN_NB, CH, w), lambda i, j, cidx=cidx: (i, cidx(j), 0))] * 3
        specs_d += [pl.BlockSpec((1, RW_SCAN_NB, CH, w), lambda i, j, cidx=cidx, d=d: (d, i, cidx(j), 0))] * 3
    out = jax.ShapeDtypeStruct((b, t, w), BF16)
    return pl.pallas_call(
        _rwkv_scan_kernel,
        grid=(b // RW_SCAN_NB, NC),
        in_specs=specs + specs_d,
        out_specs=[specs[0], specs[3]],
        out_shape=[out, out],
        scratch_shapes=[pltpu.VMEM((RW_SCAN_NB * RW_HEADS, RW_HEAD_DIM, 2 * RW_HEAD_DIM), F32)],
        compiler_params=_cparams("parallel", "arbitrary"),
        name="rwkv_scan",
    )(r, v, kk, r, v, kk, lw, kd, bb, lw, kd, bb)


def _hy_filter_kernel(z_ref, t_ref, w1_ref, b1_ref, w2_ref, b2_ref, w3_ref, b3_ref, w4_ref, fr_ref, dl_ref, o_ref):
    fr = fr_ref[...]
    h = jnp.sin(fr * (_mm(z_ref[...], w1_ref[...], NN, 2, 2) + b1_ref[...]))
    h = jnp.sin(fr * (_mm(h, w2_ref[...], NN, 2, 2) + b2_ref[...]))
    h = jnp.sin(fr * (_mm(h, w3_ref[...], NN, 2, 2) + b3_ref[...]))
    h = _mm(h, w4_ref[...], NN, 2, 2)
    h = h * jnp.exp(-t_ref[...] * dl_ref[...])
    hw = HY_WIDTH
    rows = lax.broadcasted_iota(jnp.int32, (h.shape[0], 1), 0)
    hf = h[:, :hw]
    hb = jnp.where(rows == 0, 0.0, h[:, hw:])
    norm = jnp.sum(jnp.abs(hf), axis=0, keepdims=True) + jnp.sum(jnp.abs(hb), axis=0, keepdims=True)
    inv_norm = 1.0 / norm
    o_ref[:, :hw] = (hf + hb) * inv_norm
    o_ref[:, hw:] = (hf - hb) * inv_norm


def hyena_filter_halves(length, w1, b1, w2, b2, w3, b3, w4, freq):
    t = jnp.linspace(0.0, 1.0, length, dtype=F32)[:, None]
    wv = 2 * math.pi * jnp.arange(length, dtype=F32)[:, None] / length
    f = jnp.linspace(1e-4, HY_BANDS - 1, HY_BANDS, dtype=F32)[None, :]
    z = jnp.concatenate([t, jnp.cos(f * wv), -jnp.sin(f * wv)], axis=-1)
    z = jnp.pad(z, ((0, 0), (0, HY_ORDER - HY_EMB)))
    w1p = jnp.pad(w1, ((0, HY_ORDER - HY_EMB), (0, 0)))
    deltas = jnp.abs(jnp.linspace(math.log(HY_TARGET) / HY_SLOW_DECAY, math.log(HY_TARGET) / HY_FAST_DECAY,
                                  HY_WIDTH, dtype=F32))
    dl = jnp.tile(deltas, 2)[None, :]
    row = lambda a: a.reshape(1, -1)
    return pl.pallas_call(
        _hy_filter_kernel,
        out_shape=jax.ShapeDtypeStruct((length, 2 * HY_WIDTH), F32),
        compiler_params=pltpu.CompilerParams(vmem_limit_bytes=VMEM_LIMIT),
        name="hyena_filter",
    )(z, t, w1p, row(b1), w2, row(b2), w3, row(b3), w4, row(freq), dl)


def _freq_block(length):
    return min(FB_MAX, length)


def dft_matrices(length):
    n = 2 * length
    FB = _freq_block(length)
    nfb = length // FB
    t = jnp.arange(length, dtype=jnp.int32)
    ang = lambda f: ((f[:, None] * t[None, :]) % n).astype(F32) * (2 * math.pi / n)
    a_hi = ang(jnp.arange(nfb, dtype=jnp.int32) * FB)
    a_lo = ang(jnp.arange(FB, dtype=jnp.int32))
    nyq = jnp.where(t % 2 == 0, 1.0, -1.0).astype(F32)

    def build(ch, sh, cl, sl, first, nyq_b, axis):
        cosm = ch * cl - sh * sl
        msin = -(sh * cl + ch * sl)
        msin = jnp.where(first, nyq_b, msin)
        return jnp.concatenate([cosm, msin], axis=axis).astype(BF16)

    bi = lax.broadcasted_iota
    first = jnp.logical_and(bi(jnp.int32, (nfb, FB, 1), 0) == 0, bi(jnp.int32, (nfb, FB, 1), 1) == 0)
    wf = build(jnp.cos(a_hi)[:, None, :], jnp.sin(a_hi)[:, None, :], jnp.cos(a_lo)[None], jnp.sin(a_lo)[None],
               first, nyq[None, None, :], 1).reshape(n, length)
    first_t = jnp.logical_and(bi(jnp.int32, (1, nfb, FB), 1) == 0, bi(jnp.int32, (1, nfb, FB), 2) == 0)
    wft = build(jnp.cos(a_hi).T[:, :, None], jnp.sin(a_hi).T[:, :, None], jnp.cos(a_lo).T[:, None, :],
                jnp.sin(a_lo).T[:, None, :], first_t, nyq[:, None, None], 2).reshape(length, n)
    return wf, wft


def _spectrum_kernel(w_ref, h_ref, o_ref):
    o_ref[...] = _dot(w_ref[...], h_ref[...].astype(BF16))


def filter_spectrum(wf, hpm):
    n, length = wf.shape
    cols = hpm.shape[1]
    FB = _freq_block(length)
    return pl.pallas_call(
        _spectrum_kernel,
        grid=(n // (2 * FB),),
        in_specs=[pl.BlockSpec((2 * FB, length), lambda i: (i, 0)),
                  pl.BlockSpec((length, cols), lambda i: (0, 0))],
        out_specs=pl.BlockSpec((2 * FB, cols), lambda i: (i, 0)),
        out_shape=jax.ShapeDtypeStruct((n, cols), F32),
        compiler_params=_cparams("parallel"),
        name="filter_spectrum",
    )(wf, hpm)


def _hy_conv_kernel(s_ref, wf_ref, wft_ref, h_ref, o_ref, acc_ref, *, n_fft):
    fb = pl.program_id(1)
    FB = wf_ref.shape[0] // 2

    @pl.when(fb == 0)
    def _():
        acc_ref[...] = jnp.zeros_like(acc_ref)

    hw = HY_WIDTH
    hraw = h_ref[...]
    hr, hi, hn = hraw[:FB, :hw], hraw[FB:, hw:], hraw[FB:, :hw]
    fidx = lax.broadcasted_iota(jnp.int32, (FB, 1), 0) + fb * FB
    m = (fidx != 0).astype(F32)
    cf = jnp.where(fidx != 0, 2.0 / n_fft, 1.0 / n_fft).astype(F32)
    hr_c = hr * cf
    him = hi * m * cf
    hc = (hr * m + hn * (1.0 - m)) * cf
    for bi in range(s_ref.shape[0]):
        x = _dot(wf_ref[...], s_ref[bi])
        xr, xi = x[:FB], x[FB:]
        y = jnp.concatenate([xr * hr_c - xi * him, xr * him + xi * hc], axis=0).astype(BF16)
        acc_ref[bi] += _dot(wft_ref[...], y)

    @pl.when(fb == pl.num_programs(1) - 1)
    def _():
        o_ref[...] = acc_ref[...].astype(BF16)


def hyena_long_conv(s, wf, wft, hspec, length, block):
    b = s.shape[0]
    hw = HY_WIDTH
    FB = _freq_block(length)
    assert b % CONV_NB == 0 and length % FB == 0
    return pl.pallas_call(
        functools.partial(_hy_conv_kernel, n_fft=2 * length),
        grid=(b // CONV_NB, length // FB),
        in_specs=[pl.BlockSpec((CONV_NB, length, hw), lambda i, j: (i, block, 0)),
                  pl.BlockSpec((2 * FB, length), lambda i, j: (j, 0)),
                  pl.BlockSpec((length, 2 * FB), lambda i, j: (0, j)),
                  pl.BlockSpec((2 * FB, 2 * hw), lambda i, j: (j, 0))],
        out_specs=pl.BlockSpec((CONV_NB, length, hw), lambda i, j: (i, 0, 0)),
        out_shape=jax.ShapeDtypeStruct((b, length, hw), BF16),
        scratch_shapes=[pltpu.VMEM((CONV_NB, length, hw), F32)],
        compiler_params=_cparams("parallel", "arbitrary"),
        name="hyena_long_conv",
    )(s, wf, wft, hspec)


MLP_HC = 1024


def _mlp_body(x, m, nw_ref, w1_ref, w2_ref, fw_ref, final):
    h = _rms_mod(x, nw_ref[...], m[3:4], m[4:5]).astype(BF16)
    acc = jnp.zeros_like(x)
    for c in range(w1_ref.shape[1] // MLP_HC):
        hid = _dot(h, w1_ref[:, c * MLP_HC:(c + 1) * MLP_HC])
        hid = jnp.square(jnp.maximum(hid, 0.0))
        acc = acc + _dot(hid.astype(BF16), w2_ref[c * MLP_HC:(c + 1) * MLP_HC, :])
    o = x + m[5:6] * acc
    if final:
        o = o * lax.rsqrt(jnp.mean(o * o, axis=-1, keepdims=True) + EPS) * fw_ref[...]
    return o


def _ab_tail_kernel(y0_ref, y1_ref, bv_ref, g_ref, x0_ref, cvl_ref, cvc_ref, s_ref, x_ref, ctx_ref, mod_ref,
                    lnw_ref, lnb_ref, skip_ref, bd_ref, wo_ref, nw_ref, w1_ref, w2_ref, fw_ref, o_ref):
    bd = bd_ref[...]
    inv = 1.0 / RW_HEAD_DIM
    y = y0_ref[0].astype(F32) + y1_ref[0].astype(F32)
    mean = _segsum(y, bd) * inv
    yc = y - mean
    var = _segsum(yc * yc, bd) * inv
    yn = yc * lax.rsqrt(var + RW_GN_EPS)
    a = (yn * lnw_ref[...] + lnb_ref[...] + bv_ref[0].astype(F32)) * g_ref[0].astype(F32)
    s = s_ref[0].astype(F32)
    bh = x0_ref[0].astype(F32) * (_token_block(cvl_ref, cvc_ref).astype(F32) + s * skip_ref[...])
    w = RW_WIDTH
    out = _dot(a.astype(BF16), wo_ref[:w, :]) + _dot(bh.astype(BF16), wo_ref[w:, :])
    m = mod_ref[0, 0]
    xo = _token_block(x_ref, ctx_ref) + m[2:3] * out
    o_ref[0] = _mlp_body(xo, m, nw_ref, w1_ref, w2_ref, fw_ref, False)


def ab_tail(y0, y1, bv, g, x0, cv_lat, cv_ctx, s, x, ctx, modv, ln_w, ln_b, skip, bd64, w_out, nw, w1, w2, fw,
            layer):
    b, _, d = x.shape
    w = RW_WIDTH
    tok = pl.BlockSpec((1, TB, w), lambda i, j: (i, j, 0))
    full = lambda a: pl.BlockSpec(a.shape, lambda i, j: (0,) * a.ndim)
    params = (ln_w, ln_b, skip, bd64, w_out, nw.reshape(1, d), w1, w2, fw.reshape(1, d))
    param_specs = [full(a) for a in params]
    param_specs[6] = pl.BlockSpec((None,) + w1.shape[1:], lambda i, j: (layer, 0, 0))
    param_specs[7] = pl.BlockSpec((None,) + w2.shape[1:], lambda i, j: (layer, 0, 0))
    return pl.pallas_call(
        _ab_tail_kernel,
        grid=(b, NB),
        in_specs=[tok] * 5 + _split_specs(w) + [tok] + _split_specs(d)
        + [pl.BlockSpec((1, 1, ADA_CHUNKS, d), lambda i, j: (i, j // NB_LAT, 0, 0))] + param_specs,
        out_specs=pl.BlockSpec((1, TB, d), lambda i, j: (i, j, 0)),
        out_shape=jax.ShapeDtypeStruct((b, T_ALL, d), F32),
        compiler_params=_cparams("parallel", "parallel"),
        name="ab_tail",
    )(y0, y1, bv, g, x0, cv_lat, cv_ctx, s, x, ctx, modv, *params)


def _dn_front_kernel(x_ref, xp_ref, xn_ref, nw_ref, mod_ref, w_ref, cw_ref, alog_ref, dtb_ref, bd_ref,
                     q_ref, k_ref, v_ref, z_ref, gb_ref):
    p, pprev, pnext = _project_with_neighbours(x_ref[0], xp_ref, xn_ref, nw_ref, mod_ref, w_ref)
    wq = 3 * DN_DIM
    c = cw_ref[0:1, :] * pprev[:, :wq] + cw_ref[1:2, :] * p[:, :wq] + cw_ref[2:3, :] * pnext[:, :wq]
    c = c * _sigmoid(c)
    dd = DN_DIM
    bd = bd_ref[...]
    q, k = c[:, :dd], c[:, dd:2 * dd]
    q_ref[0] = (q * lax.rsqrt(_segsum(q * q, bd) + EPS) * (DN_HEAD_DIM ** -0.5)).astype(BF16)
    k_ref[0] = (k * lax.rsqrt(_segsum(k * k, bd) + EPS)).astype(BF16)
    v_ref[0] = c[:, 2 * dd:].astype(BF16)
    z_ref[0] = p[:, wq:wq + dd].astype(BF16)
    slab = p[:, wq + dd:]
    lane = lax.broadcasted_iota(jnp.int32, (1, 128), 1)
    gdec = -jnp.exp(alog_ref[...]) * _softplus(slab + dtb_ref[...])
    gb_ref[0] = jnp.where(lane < 2 * DN_HEADS, gdec, _sigmoid(slab))


def dn_front(xs, nw, modv, w_in, conv_w, alog_row, dtb_row, bd128):
    b, t, d = xs.shape
    full = lambda a: pl.BlockSpec(a.shape, lambda i, j: (0,) * a.ndim)
    params = (nw.reshape(1, d), modv, w_in, conv_w, alog_row, dtb_row, bd128)
    param_specs = [full(a) for a in params]
    param_specs[1] = pl.BlockSpec((1, 1, ADA_CHUNKS, d), lambda i, j: (i, j // NB_LAT, 0, 0))
    tok = pl.BlockSpec((1, TB, DN_DIM), lambda i, j: (i, j, 0))
    s1 = jax.ShapeDtypeStruct((b, t, DN_DIM), BF16)
    return pl.pallas_call(
        _dn_front_kernel,
        grid=(b, t // TB),
        in_specs=[pl.BlockSpec((1, TB, d), lambda i, j: (i, j, 0))] + _halo_specs(d, t // 8) + param_specs,
        out_specs=[tok, tok, tok, tok, pl.BlockSpec((1, TB, 128), lambda i, j: (i, j, 0))],
        out_shape=[s1, s1, s1, s1, jax.ShapeDtypeStruct((b, t, 128), F32)],
        compiler_params=_cparams("parallel", "parallel"),
        name="dn_front",
    )(xs, xs, xs, *params)


def _dn_scan_kernel(qf_ref, kf_ref, vf_ref, gbf_ref, qr_ref, kr_ref, vr_ref, gbr_ref, yf_ref, yr_ref, s_ref):
    c, kdim = CH, DN_HEAD_DIM

    @pl.when(pl.program_id(1) == 0)
    def _():
        s_ref[...] = jnp.zeros_like(s_ref)

    sls = [slice(h * kdim, (h + 1) * kdim) for h in range(DN_HEADS)]
    dirs = ((qf_ref, kf_ref, vf_ref, gbf_ref), (qr_ref, kr_ref, vr_ref, gbr_ref))
    lhs, rhs, ks, qs, kps, bbs, vhs = ([] for _ in range(7))
    for d, (q_ref, k_ref, v_ref, gb_ref) in enumerate(dirs):
        for bi in range(SCAN_NB):
            gb = gb_ref[bi]
            q, k, v = q_ref[bi].astype(F32), k_ref[bi].astype(F32), v_ref[bi]
            for h, sl in enumerate(sls):
                col = d * DN_HEADS + h
                kp = k[:, sl] * gb[:, 2 * DN_HEADS + col:2 * DN_HEADS + col + 1]
                bb = kp * jnp.exp(gb[:, col:col + 1])
                lhs.append(jnp.concatenate([k[:, sl], q[:, sl]], axis=0))
                rhs.append(jnp.concatenate([bb, kp] if h % 2 == 0 else [kp, bb], axis=0))
                ks.append(k[:, sl])
                qs.append(q[:, sl])
                kps.append(kp)
                bbs.append(bb)
                vhs.append(v[:, sl])
    a_raw = _mm_many(lhs, rhs, NT)
    dfull, kkt, rt, khs, bhs, decs = ([] for _ in range(6))
    ti2 = lax.broadcasted_iota(jnp.int32, (c, 2 * c), 0)
    ji2 = lax.broadcasted_iota(jnp.int32, (c, 2 * c), 1) & (c - 1)
    for d, (_, _, _, gb_ref) in enumerate(dirs):
        reverse = d == 1
        _, _, _, incl = _chunk_masks(reverse)
        strict2 = (ji2 > ti2) if reverse else (ji2 < ti2)
        incl2 = jnp.logical_or(strict2, ti2 == ji2)
        inclt2 = (jnp.logical_or(ji2 < ti2, ti2 == ji2) if reverse else jnp.logical_or(ji2 > ti2, ti2 == ji2))
        last = 0 if reverse else c - 1
        gbs = [gb_ref[bi] for bi in range(SCAN_NB)]
        gcols = [_mm(incl.astype(BF16), gb, NN, 1, 2) for gb in gbs]
        grows = [_mm(gb, inclt2.astype(BF16), TN, 2, 1) for gb in gbs]
        for bi in range(SCAN_NB):
            gb, gcol, grow = gbs[bi], gcols[bi], grows[bi]
            for h in range(DN_HEADS):
                i = (d * SCAN_NB + bi) * DN_HEADS + h
                col = d * DN_HEADS + h
                gt = gcol[:, col:col + 1]
                gxt = gt - gb[:, col:col + 1]
                gj = grow[col:col + 1, :]
                gc = gcol[last:last + 1, col:col + 1]
                d3 = jnp.where(incl2, jnp.exp(jnp.where(incl2, gt - gj, 0.0)), 0.0)
                d1 = jnp.where(strict2, jnp.exp(jnp.where(strict2, gxt - gj, 0.0)), 0.0)
                dfull.append(jnp.concatenate([d1, d3], axis=0))
                egc = jnp.exp(gc - gt)
                kkt.append(ks[i] * jnp.exp(gxt))
                rt.append(qs[i] * jnp.exp(gt))
                khs.append(kps[i] * egc)
                bhs.append(bbs[i] * egc)
                decs.append(jnp.exp(gc))
    a = [x * dm for x, dm in zip(a_raw, dfull)]
    ys = _chunk_update_many(s_ref, a, kkt, rt, khs, bhs, vhs, decs, kdim)
    for d, y_ref in enumerate((yf_ref, yr_ref)):
        for bi in range(SCAN_NB):
            for h, sl in enumerate(sls):
                y_ref[bi, :, sl] = ys[(d * SCAN_NB + bi) * DN_HEADS + h].astype(BF16)


def dn_scan(q, k, v, gb):
    b, t, w = q.shape
    assert b % SCAN_NB == 0 and t == T_ALL and w == DN_DIM
    specs = []
    for d in range(2):
        cidx = _chunk_index(d == 1)
        tok = pl.BlockSpec((SCAN_NB, CH, w), lambda i, j, cidx=cidx: (i, cidx(j), 0))
        specs += [tok, tok, tok, pl.BlockSpec((SCAN_NB, CH, 128), lambda i, j, cidx=cidx: (i, cidx(j), 0))]
    out = jax.ShapeDtypeStruct((b, t, w), BF16)
    return pl.pallas_call(
        _dn_scan_kernel,
        grid=(b // SCAN_NB, NC),
        in_specs=specs,
        out_specs=[specs[0], specs[4]],
        out_shape=[out, out],
        scratch_shapes=[pltpu.VMEM((2 * SCAN_NB * DN_HEADS, DN_HEAD_DIM, DN_HEAD_DIM), F32)],
        compiler_params=_cparams("parallel", "arbitrary"),
        name="dn_scan",
    )(q, k, v, gb, q, k, v, gb)


def _dn_tail_kernel(o0_ref, o1_ref, z_ref, x_ref, mod_ref, gnw_ref, bd_ref, wo_ref, nw_ref, w1_ref, w2_ref, fw_ref,
                    o_ref):
    o = o0_ref[0].astype(F32) + o1_ref[0].astype(F32)
    ms = _segsum(o * o, bd_ref[...]) * (1.0 / DN_HEAD_DIM)
    on = o * lax.rsqrt(ms + EPS) * gnw_ref[...]
    z = z_ref[0].astype(F32)
    gated = on * (z * _sigmoid(z))
    m = mod_ref[0, 0]
    xo = x_ref[0] + m[2:3] * _dot(gated.astype(BF16), wo_ref[...])
    o_ref[0] = _mlp_body(xo, m, nw_ref, w1_ref, w2_ref, fw_ref, True)


def dn_tail(o0, o1, z, xs, modv, gnw_tiled, bd128, w_out, nw, w1, w2, fw, layer):
    b, _, d = xs.shape
    tok = pl.BlockSpec((1, TB, DN_DIM), lambda i, j: (i, j, 0))
    full = lambda a: pl.BlockSpec(a.shape, lambda i, j: (0,) * a.ndim)
    params = (gnw_tiled, bd128, w_out, nw.reshape(1, d), w1, w2, fw.reshape(1, d))
    param_specs = [full(a) for a in params]
    param_specs[4] = pl.BlockSpec((None,) + w1.shape[1:], lambda i, j: (layer, 0, 0))
    param_specs[5] = pl.BlockSpec((None,) + w2.shape[1:], lambda i, j: (layer, 0, 0))
    return pl.pallas_call(
        _dn_tail_kernel,
        grid=(b, NB_LAT),
        in_specs=[tok, tok, tok, pl.BlockSpec((1, TB, d), lambda i, j: (i, j, 0)),
                  pl.BlockSpec((1, 1, ADA_CHUNKS, d), lambda i, j: (i, 0, 0, 0))] + param_specs,
        out_specs=pl.BlockSpec((1, TB, d), lambda i, j: (i, j, 0)),
        out_shape=jax.ShapeDtypeStruct((b, SEQ, d), F32),
        compiler_params=_cparams("parallel", "parallel"),
        name="dn_tail",
    )(o0, o1, z, xs, modv, *params)


def _block_diag_ones(head):
    i = jnp.arange(128) // head
    return (i[:, None] == i[None, :]).astype(BF16)


def kernel(x, c, ctx, c_ctx, ada_w, ada_b, norm_mix, norm_mlp, mlp_w1, mlp_w2, final_norm, ab_w_in, ab_w_out, rw_mu, rw_w0, rw_w_up, rw_a0, rw_a_up, rw_g_up, rw_k_k, rw_k_a, rw_r_k, rw_ln_w, rw_ln_b, hy_conv_w, hy_conv_b, hy_f_w1, hy_f_b1, hy_f_w2, hy_f_b2, hy_f_w3, hy_f_b3, hy_f_w4, hy_freq, hy_skip, dn_w_in, dn_conv_w, dn_A_log, dn_dt_bias, dn_norm, dn_w_out):
    bsz = x.shape[0]
    d = D_MODEL
    w = RW_WIDTH
    row = lambda a: a.reshape(1, -1)

    cs = jnp.concatenate([c, c_ctx[None, :], jnp.zeros((16 - bsz - 1, d), F32)], axis=0)
    mod = ada_modulation(cs, ada_w, ada_b)

    def mod_vectors(layer):
        lat = mod[layer, :bsz].reshape(bsz, 1, ADA_CHUNKS, d)
        cx = jnp.broadcast_to(mod[layer, bsz].reshape(1, 1, ADA_CHUNKS, d), (bsz, 1, ADA_CHUNKS, d))
        return jnp.concatenate([lat, cx], axis=1)

    w1_all, w2_all = mlp_w1.astype(BF16), mlp_w2.astype(BF16)
    bd64 = _block_diag_ones(RW_HEAD_DIM)
    bd128 = _block_diag_ones(DN_HEAD_DIM)

    modv = mod_vectors(0)
    zeros = jnp.zeros((64, 2 * w), F32)
    lora = jnp.concatenate([jnp.concatenate([rw_w_up[0, 0], rw_w_up[0, 1], zeros], axis=1),
                            jnp.concatenate([zeros, rw_a_up[0, 0], rw_a_up[0, 1]], axis=1)], axis=0)
    r, v, kk, g, bv, lw, kd, bb, s, x0 = ab_front(
        x, ctx, norm_mix[0], modv, ab_w_in[0].astype(BF16), row(rw_mu[0]), lora.astype(BF16), row(rw_w0[0]),
        row(rw_a0[0]), rw_g_up[0].astype(BF16), row(rw_k_k[0]), row(rw_k_a[0]), row(rw_r_k[0]), bd64,
        hy_conv_w[0], row(hy_conv_b[0]))
    y0, y1 = rwkv_scan(r, v, kk, lw, kd, bb)
    filt = (hy_f_w1[0], hy_f_b1[0], hy_f_w2[0], hy_f_b2[0], hy_f_w3[0], hy_f_b3[0], hy_f_w4[0], hy_freq[0])
    convs = []
    for length, block in ((SEQ, 0), (CTX_LEN, SEQ // CTX_LEN)):
        wf, wft = dft_matrices(length)
        hspec = filter_spectrum(wf, hyena_filter_halves(length, *filt))
        convs.append(hyena_long_conv(s, wf, wft, hspec, length, block))
    xs = ab_tail(y0, y1, bv, g, x0, convs[0], convs[1], s, x, ctx, modv, row(rw_ln_w[0]), row(rw_ln_b[0]),
                 row(hy_skip[0]), bd64, ab_w_out[0].astype(BF16), norm_mlp[0], w1_all, w2_all, final_norm, 0)

    modv = mod_vectors(1)
    w_in = jnp.pad(dn_w_in[0], ((0, 0), (0, DN_COLS_PAD - DN_COLS))).astype(BF16)
    pad_row = lambda a: jnp.pad(a.reshape(1, -1), ((0, 0), (0, 128 - 2 * DN_HEADS)))
    q, k, vv, z, gb = dn_front(xs, norm_mix[1], modv, w_in, dn_conv_w[0], pad_row(dn_A_log[0]),
                               pad_row(dn_dt_bias[0]), bd128)
    o0, o1 = dn_scan(q, k, vv, gb)
    return dn_tail(o0, o1, z, xs, modv, row(jnp.tile(dn_norm[0], DN_HEADS)), bd128, dn_w_out[0].astype(BF16),
                   norm_mlp[1], w1_all, w2_all, final_norm, 1)
```
